```python
import jax, jax.numpy as jnp
from jax import lax
import numpy as np

D_MODEL = 1024
BATCH = 8
SEQ = 4096
DEPTH = 1

G_GROUPS = 8
G_WIDTH = 512
G_HEAD = G_WIDTH // G_GROUPS
CHUNK = 128
R_WIDTH = 512
R_HEAD = 64
R_HEADS = R_WIDTH // R_HEAD
DECAY_LORA = 32
AAA_LORA = 32
GATE_LORA = 96
D_FF = 4 * D_MODEL
ALPHA = (2.0 * DEPTH) ** 0.25
BETA = (8.0 * DEPTH) ** -0.25
LN_EPS = 1e-5
GN_EPS = 64e-5
R_SHIFT_COLS = 3 * R_WIDTH + DECAY_LORA + AAA_LORA + GATE_LORA
IN_COLS = 2 * G_WIDTH + R_SHIFT_COLS + 2 * D_MODEL

kernel_name = 'hybrid_gmlp_rwkv7_deepnorm_adaln'


def _layer_norm(x, g, b, eps):
    xf = x.astype(jnp.float32)
    mu = xf.mean(-1, keepdims=True)
    var = jnp.square(xf - mu).mean(-1, keepdims=True)
    return (xf - mu) * lax.rsqrt(var + eps) * g + b


def _gmlp_branch(z, g_ln_v, b_ln_v, w_spatial, b_spatial):
    B, S, _ = z.shape
    z = jax.nn.gelu(z)
    u, v = jnp.split(z, 2, axis=-1)
    v = _layer_norm(v, g_ln_v, b_ln_v, LN_EPS)
    v = v.reshape(B, S // CHUNK, CHUNK, G_GROUPS, G_HEAD)
    mask = jnp.tril(jnp.ones((CHUNK, CHUNK), dtype=bool))
    ws = jnp.where(mask[None], w_spatial, 0.0)
    s = jnp.einsum('gts,bcsgd->bctgd', ws, v) + b_spatial.T[:, :, None]
    return u * s.reshape(B, S, G_WIDTH)


def _rwkv7_branch(z, mu_shift, w0, w_decay_up, a0, w_aaa_up, w_gate_up,
                  k_k, k_a, r_k, gn_gain, gn_bias):
    B, S, _ = z.shape
    prev = jnp.pad(z, ((0, 0), (1, 0), (0, 0)))[:, :-1]
    z = z + (prev - z) * mu_shift
    i1 = R_WIDTH
    i2 = 2 * R_WIDTH
    i3 = 3 * R_WIDTH
    i4 = i3 + DECAY_LORA
    i5 = i4 + AAA_LORA
    r, k, v, xw, xa, xg = jnp.split(z, [i1, i2, i3, i4, i5], axis=-1)
    w_log = -jax.nn.softplus(-(w0 + jnp.tanh(xw) @ w_decay_up)) - 0.5
    decay = jnp.exp(-jnp.exp(w_log.astype(jnp.float32)))
    a = jax.nn.sigmoid(a0 + xa @ w_aaa_up)
    g = jax.nn.sigmoid(xg) @ w_gate_up

    def heads(t):
        return t.astype(jnp.float32).reshape(B, S, R_HEADS, R_HEAD)

    kk = heads(k * k_k)
    kk = kk / jnp.maximum(jnp.linalg.norm(kk, axis=-1, keepdims=True), 1e-12)
    k = k * (1.0 + (a - 1.0) * k_a)
    r_h, k_h, v_h, a_h = heads(r), heads(k), heads(v), heads(a)

    def tm(t):
        return jnp.transpose(t, (1, 0, 2, 3))

    def step(state, inp):
        rt, wt, kt, vt, at, bt = inp
        sa = jnp.einsum('bhij,bhj->bhi', state, at)
        state = (state * wt[:, :, None, :] + sa[..., None] * bt[:, :, None, :]
                 + vt[..., None] * kt[:, :, None, :])
        yt = jnp.einsum('bhij,bhj->bhi', state, rt)
        return state, yt

    s0 = jnp.zeros((B, R_HEADS, R_HEAD, R_HEAD), jnp.float32)
    _, y = lax.scan(step, s0, (tm(r_h), tm(heads(decay)), tm(k_h), tm(v_h),
                               tm(-kk), tm(kk * a_h)))
    y = jnp.transpose(y, (1, 0, 2, 3))
    mu = y.mean(-1, keepdims=True)
    var = jnp.square(y - mu).mean(-1, keepdims=True)
    y = ((y - mu) * lax.rsqrt(var + GN_EPS)).reshape(B, S, R_WIDTH) * gn_gain + gn_bias
    bonus = (r_h * k_h * r_k).sum(-1, keepdims=True) * v_h
    return (y + bonus.reshape(B, S, R_WIDTH)) * g


def setup_inputs(seed: int = 0) -> dict:
    key = jax.random.key(seed)
    ks = jax.random.split(key, 36)
    L = DEPTH

    def nrm(k, shape, scale):
        return jax.random.normal(k, shape, jnp.float32) * scale

    w_in = nrm(ks[4], (L, D_MODEL, IN_COLS), D_MODEL ** -0.5)
    v_lo = 2 * G_WIDTH + 2 * R_WIDTH
    w_in = w_in.at[:, :, v_lo:v_lo + R_WIDTH].multiply(BETA)
    return {
        'x': nrm(ks[0], (BATCH, SEQ, D_MODEL), 1.0),
        'c': nrm(ks[1], (BATCH, D_MODEL), 1.0),
        'w_ada': nrm(ks[2], (L, D_MODEL, 6 * D_MODEL), 0.5 * D_MODEL ** -0.5),
        'b_ada': nrm(ks[3], (L, 6 * D_MODEL), 0.02),
        'w_in': w_in,
        'b_in': nrm(ks[5], (L, IN_COLS), 0.02),
        'g_ln_v': 1.0 + nrm(ks[6], (L, G_WIDTH), 0.05),
        'b_ln_v': nrm(ks[7], (L, G_WIDTH), 0.02),
        'w_spatial': nrm(ks[8], (L, G_GROUPS, CHUNK, CHUNK), CHUNK ** -0.5),
        'b_spatial': 1.0 + nrm(ks[9], (L, G_GROUPS, CHUNK), 0.1),
        'mu_shift': jax.random.uniform(ks[10], (L, R_SHIFT_COLS), jnp.float32),
        'w0': jax.random.uniform(ks[11], (L, R_WIDTH), jnp.float32, minval=-5.0, maxval=0.0),
        'w_decay_up': nrm(ks[12], (L, DECAY_LORA, R_WIDTH), DECAY_LORA ** -0.5),
        'a0': nrm(ks[13], (L, R_WIDTH), 0.1),
        'w_aaa_up': nrm(ks[14], (L, AAA_LORA, R_WIDTH), AAA_LORA ** -0.5),
        'w_gate_up': nrm(ks[15], (L, GATE_LORA, R_WIDTH), GATE_LORA ** -0.5),
        'k_k': 0.85 + nrm(ks[16], (L, R_WIDTH), 0.05),
        'k_a': 1.0 + nrm(ks[17], (L, R_WIDTH), 0.05),
        'r_k': nrm(ks[18], (L, R_HEADS, R_HEAD), 0.1),
        'gn_gain': 1.0 + nrm(ks[19], (L, R_WIDTH), 0.05),
        'gn_bias': nrm(ks[20], (L, R_WIDTH), 0.02),
        'w_branch_a': nrm(ks[21], (L, G_WIDTH, D_MODEL), BETA * G_WIDTH ** -0.5),
        'w_branch_b': nrm(ks[22], (L, R_WIDTH, D_MODEL), BETA * R_WIDTH ** -0.5),
        'w_out': nrm(ks[23], (L, D_MODEL, D_MODEL), BETA * D_MODEL ** -0.5),
        'b_out': nrm(ks[24], (L, D_MODEL), 0.01),
        'ln1_g': 1.0 + nrm(ks[25], (L, D_MODEL), 0.05),
        'ln1_b': nrm(ks[26], (L, D_MODEL), 0.02),
        'w_ff1': nrm(ks[27], (L, D_MODEL, D_FF), BETA * D_MODEL ** -0.5),
        'b_ff1': nrm(ks[28], (L, D_FF), 0.01),
        'w_ff2': nrm(ks[29], (L, D_FF, D_MODEL), BETA * D_FF ** -0.5),
        'b_ff2': nrm(ks[30], (L, D_MODEL), 0.01),
        'ln2_g': 1.0 + nrm(ks[31], (L, D_MODEL), 0.05),
        'ln2_b': nrm(ks[32], (L, D_MODEL), 0.02),
    }


def reference(x, c, w_ada, b_ada, w_in, b_in, g_ln_v, b_ln_v, w_spatial, b_spatial,
              mu_shift, w0, w_decay_up, a0, w_aaa_up, w_gate_up, k_k, k_a, r_k,
              gn_gain, gn_bias, w_branch_a, w_branch_b, w_out, b_out, ln1_g, ln1_b,
              w_ff1, b_ff1, w_ff2, b_ff2, ln2_g, ln2_b):
    out_dtype = x.dtype
    h_res = x.astype(jnp.float32)
    c_act = jax.nn.silu(c.astype(jnp.float32))
    g_end = 2 * G_WIDTH
    r_end = g_end + R_SHIFT_COLS
    for l in range(DEPTH):
        mod = c_act @ w_ada[l] + b_ada[l]
        sh1, sc1, gt1, sh2, sc2, gt2 = [m[:, None, :] for m in jnp.split(mod, 6, axis=-1)]
        h = h_res * (1.0 + sc1) + sh1
        proj = h @ w_in[l] + b_in[l]
        y_a = _gmlp_branch(proj[..., :g_end], g_ln_v[l], b_ln_v[l], w_spatial[l], b_spatial[l])
        y_b = _rwkv7_branch(proj[..., g_end:r_end], mu_shift[l], w0[l], w_decay_up[l], a0[l],
                            w_aaa_up[l], w_gate_up[l], k_k[l], k_a[l], r_k[l],
                            gn_gain[l], gn_bias[l])
        gate_a, gate_b = jnp.split(proj[..., r_end:], 2, axis=-1)
        merged = (jax.nn.sigmoid(gate_a) * (y_a @ w_branch_a[l])
                  + jax.nn.sigmoid(gate_b) * (y_b @ w_branch_b[l]))
        mix = merged @ w_out[l] + b_out[l]
        h_res = _layer_norm(ALPHA * h_res + gt1 * mix, ln1_g[l], ln1_b[l], LN_EPS)
        h = h_res * (1.0 + sc2) + sh2
        ff = jnp.square(jax.nn.relu(h @ w_ff1[l] + b_ff1[l])) @ w_ff2[l] + b_ff2[l]
        h_res = _layer_norm(ALPHA * h_res + gt2 * ff, ln2_g[l], ln2_b[l], LN_EPS)
    return h_res.astype(out_dtype)
```

```python
import functools

import jax
import jax.numpy as jnp
from jax import lax
from jax.experimental import pallas as pl
from jax.experimental.pallas import tpu as pltpu

D_MODEL = 1024
G_GROUPS = 8
G_WIDTH = 512
CHUNK = 128
R_WIDTH = 512
R_HEAD = 64
R_HEADS = R_WIDTH // R_HEAD
DECAY_LORA = 32
AAA_LORA = 32
GATE_LORA = 96
LORA_COLS = DECAY_LORA + AAA_LORA + GATE_LORA
D_FF = 4 * D_MODEL
DEPTH = 1
ALPHA = (2.0 * DEPTH) ** 0.25
LN_EPS = 1e-5
GN_EPS = 64e-5

LANES = 128
PAIRS = R_WIDTH // LANES
LORA_PAD = 2 * LANES
R_COLS = 3 * R_WIDTH + LORA_PAD
SCAN_CHUNK = 64
STACK = 2 * SCAN_CHUNK
VMEM_LIMIT = 56 * 1024 * 1024

F32 = jnp.float32
BF16 = jnp.bfloat16


def _dot(a, b):
    return jnp.dot(a, b, preferred_element_type=F32)


def _dot_nt(a, b):
    return lax.dot_general(a, b, (((1,), (1,)), ((), ())), preferred_element_type=F32)


def _sigmoid(x):
    return 1.0 / (1.0 + jnp.exp(-x))


def _softplus(x):
    return jnp.maximum(x, 0.0) + jnp.log(1.0 + jnp.exp(-jnp.abs(x)))


def _gelu_tanh(x):
    return 0.5 * x * (1.0 + jnp.tanh(0.7978845608028654 * (x + 0.044715 * (x * x * x))))


def _layer_norm(x, g, b, eps):
    mu = jnp.mean(x, axis=-1, keepdims=True)
    d = x - mu
    var = jnp.mean(d * d, axis=-1, keepdims=True)
    return d * lax.rsqrt(var + eps) * g + b


def _const_spec(shape):
    n = len(shape)
    return pl.BlockSpec(shape, lambda *_: (0,) * n)


def _mod_kernel(c_ref, w_ref, b_ref, o_ref):
    c = c_ref[...]
    c_act = c * _sigmoid(c)
    o_ref[...] = jnp.dot(c_act, w_ref[...], preferred_element_type=F32,
                         precision=lax.Precision.HIGHEST) + b_ref[...]


def _modulation(c, w_ada, b_ada):
    bsz = c.shape[0]
    n = w_ada.shape[1]
    tn = D_MODEL
    return pl.pallas_call(
        _mod_kernel,
        grid=(n // tn,),
        in_specs=[pl.BlockSpec((bsz, D_MODEL), lambda j: (0, 0)),
                  pl.BlockSpec((D_MODEL, tn), lambda j: (0, j)),
                  pl.BlockSpec((1, tn), lambda j: (0, j))],
        out_specs=pl.BlockSpec((bsz, tn), lambda j: (0, j)),
        out_shape=jax.ShapeDtypeStruct((bsz, n), F32),
        name="mod",
    )(c, w_ada, b_ada.reshape(1, n))


def _inproj_kernel(x_ref, mod_ref, wg_ref, bg_ref, wr_ref, br_ref, wgate_ref, bgate_ref,
                   glnv_ref, blnv_ref, ws_ref, bs_ref, mu_ref, w0_ref, a0_ref, wup_ref,
                   kk_ref, ka_ref, bd_ref, wba_ref,
                   ma_ref, gb_ref, r_ref, lw_ref, k_ref, v_ref, a_ref, b_ref, g_ref,
                   zsh_ref):
    tl = x_ref.shape[1]
    x = x_ref[0]
    sh1 = mod_ref[0, 0:1, :]
    sc1 = mod_ref[0, 1:2, :]
    h = (x * (1.0 + sc1) + sh1).astype(BF16)

    zg = _gelu_tanh(_dot(h, wg_ref[...]) + bg_ref[...])
    u = zg[:, :G_WIDTH]
    v = _layer_norm(zg[:, G_WIDTH:], glnv_ref[...], blnv_ref[...], LN_EPS)
    lane = lax.broadcasted_iota(jnp.int32, (CHUNK, LANES), 1)
    first_head = lane < R_HEAD
    trow = lax.broadcasted_iota(jnp.int32, (CHUNK, 2 * CHUNK), 0)
    scol = lax.broadcasted_iota(jnp.int32, (CHUNK, 2 * CHUNK), 1) % CHUNK
    causal = trow >= scol
    ws = [jnp.where(causal, ws_ref[q], 0.0).astype(BF16) for q in range(PAIRS)]
    ya_rows = []
    for c in range(tl // CHUNK):
        vc = v[c * CHUNK:(c + 1) * CHUNK, :]
        s_parts = []
        for q in range(PAIRS):
            v2 = vc[:, q * LANES:(q + 1) * LANES]
            vm = jnp.concatenate([jnp.where(first_head, v2, 0.0),
                                  jnp.where(first_head, 0.0, v2)], axis=0).astype(BF16)
            s_parts.append(_dot(ws[q], vm))
        s = jnp.concatenate(s_parts, axis=1) + bs_ref[...]
        ya_rows.append(u[c * CHUNK:(c + 1) * CHUNK, :] * s)
    ya = jnp.concatenate(ya_rows, axis=0).astype(BF16)
    pa = _dot(ya, wba_ref[...])

    zgate = _dot(h, wgate_ref[...]) + bgate_ref[...]
    ma_ref[0] = _sigmoid(zgate[:, :D_MODEL]) * pa
    gb_ref[0] = _sigmoid(zgate[:, D_MODEL:]).astype(gb_ref.dtype)

    zr = _dot(h, wr_ref[...]) + br_ref[...]

    @pl.when(pl.program_id(1) == 0)
    def _():
        zsh_ref[0:8, :] = jnp.zeros((8, R_COLS), F32)

    zsh_ref[8:tl + 8, :] = zr
    prev = zsh_ref[7:tl + 7, :]
    zsh_ref[7:8, :] = zr[tl - 1:tl, :]
    z = zr + (prev - zr) * mu_ref[...]

    r = z[:, 0:R_WIDTH]
    k = z[:, R_WIDTH:2 * R_WIDTH]
    vv = z[:, 2 * R_WIDTH:3 * R_WIDTH]
    xl = z[:, 3 * R_WIDTH:]
    llane = lax.broadcasted_iota(jnp.int32, xl.shape, 1)
    lin = jnp.where(llane < DECAY_LORA, jnp.tanh(xl),
                    jnp.where(llane < DECAY_LORA + AAA_LORA, xl, _sigmoid(xl)))
    up = _dot(lin.astype(BF16), wup_ref[...])
    w_log = -_softplus(-(w0_ref[...] + up[:, 0:R_WIDTH])) - 0.5
    lw = -jnp.exp(w_log)
    a = _sigmoid(a0_ref[...] + up[:, R_WIDTH:2 * R_WIDTH])
    g_ref[0] = up[:, 2 * R_WIDTH:]

    kk = k * kk_ref[...]
    n2 = _dot((kk * kk).astype(BF16), bd_ref[...])
    kk = kk / jnp.maximum(jnp.sqrt(n2), 1e-12)
    k2 = k * (1.0 + (a - 1.0) * ka_ref[...])
    nkk = -kk
    kka = kk * a
    for q in range(PAIRS):
        sl = slice(q * LANES, (q + 1) * LANES)
        r_ref[0, q] = r[:, sl]
        lw_ref[0, q] = lw[:, sl]
        k_ref[0, q] = k2[:, sl]
        v_ref[0, q] = vv[:, sl]
        a_ref[0, q] = nkk[:, sl]
        b_ref[0, q] = kka[:, sl]


def _inproj(x, mod, p, tl):
    bsz, seq, _ = x.shape
    grid = (bsz, seq // tl)
    row = lambda b, s: (b, s, 0)
    pair = lambda b, s: (b, 0, s, 0)
    pair_shape = jax.ShapeDtypeStruct((bsz, PAIRS, seq, LANES), F32)
    pair_spec = pl.BlockSpec((1, PAIRS, tl, LANES), pair)
    consts = [p["wg"], p["bg"], p["wr"], p["br"], p["wgate"], p["bgate"], p["glnv"], p["blnv"],
              p["ws"], p["bs"], p["mu"], p["w0"], p["a0"], p["wup"], p["kk"], p["ka"], p["bd"],
              p["wba"]]
    return pl.pallas_call(
        _inproj_kernel,
        grid=grid,
        in_specs=[pl.BlockSpec((1, tl, D_MODEL), row),
                  pl.BlockSpec((1, 6, D_MODEL), lambda b, s: (b, 0, 0))]
                 + [_const_spec(c.shape) for c in consts],
        out_specs=[pl.BlockSpec((1, tl, D_MODEL), row),
                   pl.BlockSpec((1, tl, D_MODEL), row)]
                  + [pair_spec] * 6
                  + [pl.BlockSpec((1, tl, R_WIDTH), row)],
        out_shape=[jax.ShapeDtypeStruct((bsz, seq, D_MODEL), F32),
                   jax.ShapeDtypeStruct((bsz, seq, D_MODEL), BF16)]
                  + [pair_shape] * 6
                  + [jax.ShapeDtypeStruct((bsz, seq, R_WIDTH), F32)],
        scratch_shapes=[pltpu.VMEM((tl + 8, R_COLS), F32)],
        compiler_params=pltpu.CompilerParams(
            dimension_semantics=("arbitrary", "arbitrary"), vmem_limit_bytes=VMEM_LIMIT),
        name="inproj",
    )(x, mod, *consts)


def _scan_kernel(r_ref, lw_ref, k_ref, v_ref, a_ref, b_ref, rk_ref, gng_ref, gnb_ref,
                 y_ref,
                 st_ref, pm_ref, qm_ref, ra_ref, bkt_ref, vm_ref, y0_ref, wm_ref):
    bt = r_ref.shape[0]
    lt = r_ref.shape[2]
    nc = lt // SCAN_CHUNK
    cs = SCAN_CHUNK

    @pl.when(pl.program_id(1) == 0)
    def _():
        st_ref[...] = jnp.zeros(st_ref.shape, F32)

    lane = lax.broadcasted_iota(jnp.int32, (cs, LANES), 1)
    first_head = lane < R_HEAD

    def stack(t):
        return jnp.concatenate([jnp.where(first_head, t, 0.0),
                                jnp.where(first_head, 0.0, t)], axis=0)

    ri = lax.broadcasted_iota(jnp.int32, (cs, cs), 0)
    ci = lax.broadcasted_iota(jnp.int32, (cs, cs), 1)
    tri_ones = jnp.where(ri >= ci, 1.0, 0.0).astype(BF16)
    si = lax.broadcasted_iota(jnp.int32, (STACK, STACK), 0)
    sj = lax.broadcasted_iota(jnp.int32, (STACK, STACK), 1)
    strict = (si % cs) > (sj % cs)
    incl = (si % cs) >= (sj % cs)
    eye = jnp.where(si == sj, 1.0, 0.0).astype(F32)

    def prepare(i, carry):
        b = i // nc
        c = i % nc
        rows = pl.ds(pl.multiple_of(c * cs, cs), cs)
        lw = lw_ref[b, 0, rows, :]
        p1 = lw.astype(BF16)
        e1 = lw - p1.astype(F32)
        p2 = e1.astype(BF16)
        p3 = (e1 - p2.astype(F32)).astype(BF16)
        cum = _dot(tri_ones, p1) + _dot(tri_ones, p2) + _dot(tri_ones, p3)
        last = cum[cs - 1:cs, :]
        e_pos = jnp.exp(cum)
        e_pos_ex = jnp.exp(cum - lw)
        e_neg = jnp.exp(-cum)
        e_end = jnp.exp(last - cum)
        wc = jnp.exp(last)

        r = r_ref[b, 0, rows, :]
        k = k_ref[b, 0, rows, :]
        v = v_ref[b, 0, rows, :]
        a = a_ref[b, 0, rows, :]
        bb = b_ref[b, 0, rows, :]
        am = stack(a * e_pos_ex)
        rm = stack(r * e_pos)
        bm = stack(bb * e_neg)
        km = stack(k * e_neg)
        vm = stack(v).astype(BF16)
        bbar = stack(bb * e_end)
        kbar = stack(k * e_end)

        am_b = am.astype(BF16)
        rm_b = rm.astype(BF16)
        aa = _dot_nt(jnp.concatenate([am_b, rm_b], axis=0),
                     jnp.concatenate([bm, km], axis=0).astype(BF16))
        a_ab = jnp.where(strict, aa[:STACK, :STACK], 0.0)
        a_ak = jnp.where(strict, aa[:STACK, STACK:], 0.0)
        a_rb = jnp.where(incl, aa[STACK:, :STACK], 0.0)
        a_rk = jnp.where(incl, aa[STACK:, STACK:], 0.0)

        ab = a_ab.astype(BF16)
        xk = _dot(ab, ab)
        tk = eye + a_ab
        n_sq = (cs - 1).bit_length() - 1
        for step in range(n_sq):
            xb = xk.astype(BF16)
            if step < n_sq - 1:
                xt = _dot(jnp.concatenate([xb, tk.astype(BF16)], axis=0), xb)
                xk = xt[:STACK]
                tk = tk + xt[STACK:]
            else:
                tk = tk + _dot(tk.astype(BF16), xb)

        akv_y0 = _dot(jnp.concatenate([a_ak, a_rk], axis=0).astype(BF16), vm)
        akv = akv_y0[:STACK]
        pq = _dot(tk.astype(BF16), jnp.concatenate([am_b, akv.astype(BF16)], axis=1))

        pm_ref[i] = pq[:, :LANES].astype(BF16)
        qm_ref[i] = pq[:, LANES:]
        ra_ref[i] = jnp.concatenate([rm_b, a_rb.astype(BF16)], axis=1)
        bkt_ref[i] = jnp.concatenate([bbar.T, kbar.T], axis=1).astype(BF16)
        vm_ref[i] = vm
        y0_ref[i] = akv_y0[STACK:]
        wm_ref[i] = jnp.broadcast_to(wc, (LANES, LANES)).T
        return carry

    lax.fori_loop(0, bt * nc, prepare, 0)

    bsi = lax.broadcasted_iota(jnp.int32, (LANES, LANES), 0) // R_HEAD
    bsj = lax.broadcasted_iota(jnp.int32, (LANES, LANES), 1) // R_HEAD
    head_ones = jnp.where(bsi == bsj, 1.0, 0.0).astype(BF16)
    head_mean = jnp.where(bsi == bsj, 1.0 / R_HEAD, 0.0).astype(BF16)
    rk = rk_ref[...]
    gng = gng_ref[...]
    gnb = gnb_ref[...]

    def advance(c, carry):
        rows = pl.ds(pl.multiple_of(c * cs, cs), cs)
        for b in range(bt):
            i = b * nc + c
            st = st_ref[b]
            st_b = st.astype(BF16)
            u = _dot(pm_ref[i], st_b) + qm_ref[i]
            u_b = u.astype(BF16)
            ym = _dot(ra_ref[i], jnp.concatenate([st_b, u_b], axis=0)) + y0_ref[i]
            st_ref[b] = wm_ref[i] * st + _dot(bkt_ref[i], jnp.concatenate([u_b, vm_ref[i]], axis=0))
            y2 = ym[:cs] + ym[cs:]
            d = y2 - _dot(y2.astype(BF16), head_mean)
            var = _dot((d * d).astype(BF16), head_mean)
            yn = d * lax.rsqrt(var + GN_EPS) * gng + gnb
            r = r_ref[b, 0, rows, :]
            k = k_ref[b, 0, rows, :]
            v = v_ref[b, 0, rows, :]
            bonus = _dot((r * k * rk).astype(BF16), head_ones) * v
            y_ref[b, 0, rows, :] = yn + bonus
        return carry

    lax.fori_loop(0, nc, advance, 0)


def _scan(r, lw, k, v, a, b, rk, gng, gnb, lt):
    bsz, _, seq, _ = r.shape
    nc = lt // SCAN_CHUNK
    n = bsz * nc
    blk = pl.BlockSpec((bsz, 1, lt, LANES), lambda q, s: (0, q, s, 0))
    vec = pl.BlockSpec((1, LANES), lambda q, s: (0, q))
    return pl.pallas_call(
        _scan_kernel,
        grid=(PAIRS, seq // lt),
        in_specs=[blk] * 6 + [vec] * 3,
        out_specs=blk,
        out_shape=jax.ShapeDtypeStruct((bsz, PAIRS, seq, LANES), F32),
        scratch_shapes=[pltpu.VMEM((bsz, LANES, LANES), F32),
                        pltpu.VMEM((n, STACK, LANES), BF16),
                        pltpu.VMEM((n, STACK, LANES), F32),
                        pltpu.VMEM((n, STACK, 2 * LANES), BF16),
                        pltpu.VMEM((n, LANES, 2 * STACK), BF16),
                        pltpu.VMEM((n, STACK, LANES), BF16),
                        pltpu.VMEM((n, STACK, LANES), F32),
                        pltpu.VMEM((n, LANES, LANES), F32)],
        compiler_params=pltpu.CompilerParams(
            dimension_semantics=("arbitrary", "arbitrary"), vmem_limit_bytes=VMEM_LIMIT),
        name="scan",
    )(r, lw, k, v, a, b, rk, gng, gnb)


def _mix_kernel(x_ref, mod_ref, yb_ref, g_ref, gb_ref, ma_ref, wbb_ref, wout_ref, bout_ref,
                lng_ref, lnb_ref, h1_ref):
    yb = jnp.concatenate([yb_ref[0, q] for q in range(PAIRS)], axis=1) * g_ref[0]
    pb = _dot(yb.astype(BF16), wbb_ref[...])
    merged = ma_ref[0] + gb_ref[0].astype(F32) * pb
    mix = _dot(merged.astype(BF16), wout_ref[...]) + bout_ref[...]
    gt1 = mod_ref[0, 2:3, :]
    h1_ref[0] = _layer_norm(ALPHA * x_ref[0] + gt1 * mix, lng_ref[...], lnb_ref[...], LN_EPS)


def _mix(x, mod, yb, g, gb, ma, p, tl):
    bsz, seq, _ = x.shape
    row = lambda b, s: (b, s, 0)
    consts = [p["wbb"], p["wout"], p["bout"], p["ln1g"], p["ln1b"]]
    return pl.pallas_call(
        _mix_kernel,
        grid=(bsz, seq // tl),
        in_specs=[pl.BlockSpec((1, tl, D_MODEL), row),
                  pl.BlockSpec((1, 6, D_MODEL), lambda b, s: (b, 0, 0)),
                  pl.BlockSpec((1, PAIRS, tl, LANES), lambda b, s: (b, 0, s, 0)),
                  pl.BlockSpec((1, tl, R_WIDTH), row),
                  pl.BlockSpec((1, tl, D_MODEL), row),
                  pl.BlockSpec((1, tl, D_MODEL), row)]
                 + [_const_spec(c.shape) for c in consts],
        out_specs=pl.BlockSpec((1, tl, D_MODEL), row),
        out_shape=jax.ShapeDtypeStruct((bsz, seq, D_MODEL), F32),
        compiler_params=pltpu.CompilerParams(
            dimension_semantics=("arbitrary", "arbitrary"), vmem_limit_bytes=VMEM_LIMIT),
        name="mix",
    )(x, mod, yb, g, gb, ma, *consts)


def _ffn_kernel(h1_ref, mod_ref, w1_ref, b1_ref, w2_ref, b2_ref, lng_ref, lnb_ref, o_ref):
    h1 = h1_ref[0]
    sh2 = mod_ref[0, 3:4, :]
    sc2 = mod_ref[0, 4:5, :]
    gt2 = mod_ref[0, 5:6, :]
    h = (h1 * (1.0 + sc2) + sh2).astype(BF16)
    t = jnp.maximum(_dot(h, w1_ref[...]) + b1_ref[...], 0.0)
    ff = _dot((t * t).astype(BF16), w2_ref[...]) + b2_ref[...]
    o_ref[0] = _layer_norm(ALPHA * h1 + gt2 * ff, lng_ref[...], lnb_ref[...], LN_EPS).astype(o_ref.dtype)


def _ffn(h1, mod, p, tl, out_dtype):
    bsz, seq, _ = h1.shape
    row = lambda b, s: (b, s, 0)
    consts = [p["w1"], p["b1"], p["w2"], p["b2"], p["ln2g"], p["ln2b"]]
    return pl.pallas_call(
        _ffn_kernel,
        grid=(bsz, seq // tl),
        in_specs=[pl.BlockSpec((1, tl, D_MODEL), row),
                  pl.BlockSpec((1, 6, D_MODEL), lambda b, s: (b, 0, 0))]
                 + [_const_spec(c.shape) for c in consts],
        out_specs=pl.BlockSpec((1, tl, D_MODEL), row),
        out_shape=jax.ShapeDtypeStruct((bsz, seq, D_MODEL), out_dtype),
        compiler_params=pltpu.CompilerParams(
            dimension_semantics=("arbitrary", "arbitrary"), vmem_limit_bytes=VMEM_LIMIT),
        name="ffn",
    )(h1, mod, *consts)


def _prepare_params(l, w_in, b_in, g_ln_v, b_ln_v, w_spatial, b_spatial, mu_shift, w0,
                    w_decay_up, a0, w_aaa_up, w_gate_up, k_k, k_a, r_k, gn_gain, gn_bias,
                    w_branch_a, w_branch_b, w_out, b_out, ln1_g, ln1_b, w_ff1, b_ff1, w_ff2,
                    b_ff2, ln2_g, ln2_b):
    g_end = 2 * G_WIDTH
    rkv_end = g_end + 3 * R_WIDTH
    r_end = rkv_end + LORA_COLS
    pad = LORA_PAD - LORA_COLS
    row2 = lambda t: t.reshape(1, -1)
    wi, bi = w_in[l], b_in[l]
    wr = jnp.concatenate([wi[:, g_end:r_end], jnp.zeros((D_MODEL, pad), F32)], axis=1)
    br = jnp.concatenate([bi[g_end:r_end], jnp.zeros((pad,), F32)])
    mu = jnp.concatenate([mu_shift[l], jnp.zeros((pad,), F32)])
    wup = jnp.zeros((LORA_PAD, 3 * R_WIDTH), F32)
    wup = wup.at[0:DECAY_LORA, 0:R_WIDTH].set(w_decay_up[l])
    wup = wup.at[DECAY_LORA:DECAY_LORA + AAA_LORA, R_WIDTH:2 * R_WIDTH].set(w_aaa_up[l])
    wup = wup.at[DECAY_LORA + AAA_LORA:LORA_COLS, 2 * R_WIDTH:].set(w_gate_up[l])
    ws = w_spatial[l].reshape(PAIRS, 2, CHUNK, CHUNK).transpose(0, 2, 1, 3).reshape(PAIRS, CHUNK, 2 * CHUNK)
    bs = jnp.repeat(b_spatial[l].T, G_WIDTH // G_GROUPS, axis=1)
    hid = jnp.arange(R_WIDTH) // R_HEAD
    bd = (hid[:, None] == hid[None, :]).astype(BF16)
    return dict(
        wg=wi[:, :g_end].astype(BF16), bg=row2(bi[:g_end]),
        wr=wr.astype(BF16), br=row2(br),
        wgate=wi[:, r_end:].astype(BF16), bgate=row2(bi[r_end:]),
        glnv=row2(g_ln_v[l]), blnv=row2(b_ln_v[l]), ws=ws, bs=bs, mu=row2(mu),
        w0=row2(w0[l]), a0=row2(a0[l]), wup=wup.astype(BF16), kk=row2(k_k[l]), ka=row2(k_a[l]),
        bd=bd, wba=w_branch_a[l].astype(BF16),
        rk=row2(r_k[l]), gng=row2(gn_gain[l]), gnb=row2(gn_bias[l]),
        wbb=w_branch_b[l].astype(BF16), wout=w_out[l].astype(BF16), bout=row2(b_out[l]),
        ln1g=row2(ln1_g[l]), ln1b=row2(ln1_b[l]),
        w1=w_ff1[l].astype(BF16), b1=row2(b_ff1[l]), w2=w_ff2[l].astype(BF16), b2=row2(b_ff2[l]),
        ln2g=row2(ln2_g[l]), ln2b=row2(ln2_b[l]),
    )


def _tile(seq, want):
    t = min(want, seq)
    while seq % t:
        t //= 2
    return t


def kernel(x, c, w_ada, b_ada, w_in, b_in, g_ln_v, b_ln_v, w_spatial, b_spatial, mu_shift, w0, w_decay_up, a0, w_aaa_up, w_gate_up, k_k, k_a, r_k, gn_gain, gn_bias, w_branch_a, w_branch_b, w_out, b_out, ln1_g, ln1_b, w_ff1, b_ff1, w_ff2, b_ff2, ln2_g, ln2_b):
    bsz, seq, _ = x.shape
    assert seq % CHUNK == 0 and x.shape[2] == D_MODEL
    out_dtype = x.dtype
    h_res = x.astype(F32)
    tl = _tile(seq, 256)
    for l in range(DEPTH):
        p = _prepare_params(l, w_in, b_in, g_ln_v, b_ln_v, w_spatial, b_spatial, mu_shift, w0,
                            w_decay_up, a0, w_aaa_up, w_gate_up, k_k, k_a, r_k, gn_gain, gn_bias,
                            w_branch_a, w_branch_b, w_out, b_out, ln1_g, ln1_b, w_ff1, b_ff1,
                            w_ff2, b_ff2, ln2_g, ln2_b)
        mod = _modulation(c.astype(F32), w_ada[l], b_ada[l]).reshape(bsz, 6, D_MODEL)
        ma, gb, r, lw, k, v, a, b, g = _inproj(h_res, mod, p, tl)
        yb = _scan(r, lw, k, v, a, b, p["rk"], p["gng"], p["gnb"], _tile(seq, 256))
        h1 = _mix(h_res, mod, yb, g, gb, ma, p, tl)
        h_res = _ffn(h1, mod, p, tl, F32)
    return h_res.astype(out_dtype)
```

```python
import functools

import jax
import jax.numpy as jnp
from jax import lax
from jax.experimental import pallas as pl
from jax.experimental.pallas import tpu as pltpu

D_MODEL = 1024
G_GROUPS = 8
G_WIDTH = 512
CHUNK = 128
R_WIDTH = 512
R_HEAD = 64
R_HEADS = R_WIDTH // R_HEAD
DECAY_LORA = 32
AAA_LORA = 32
GATE_LORA = 96
LORA_COLS = DECAY_LORA + AAA_LORA + GATE_LORA
D_FF = 4 * D_MODEL
DEPTH = 1
ALPHA = (2.0 * DEPTH) ** 0.25
LN_EPS = 1e-5
GN_EPS = 64e-5

LANES = 128
PAIRS = R_WIDTH // LANES
LORA_PAD = 2 * LANES
R_COLS = 3 * R_WIDTH + LORA_PAD
SCAN_CHUNK = 64
STACK = 2 * SCAN_CHUNK
VMEM_LIMIT = 56 * 1024 * 1024

F32 = jnp.float32
BF16 = jnp.bfloat16


def _dot(a, b):
    return jnp.dot(a, b, preferred_element_type=F32)


def _dot_nt(a, b):
    return lax.dot_general(a, b, (((1,), (1,)), ((), ())), preferred_element_type=F32)


def _sigmoid(x):
    return 1.0 / (1.0 + jnp.exp(-x))


def _softplus(x):
    return jnp.maximum(x, 0.0) + jnp.log(1.0 + jnp.exp(-jnp.abs(x)))


def _gelu_tanh(x):
    return 0.5 * x * (1.0 + jnp.tanh(0.7978845608028654 * (x + 0.044715 * (x * x * x))))


def _layer_norm(x, g, b, eps):
    mu = jnp.mean(x, axis=-1, keepdims=True)
    d = x - mu
    var = jnp.mean(d * d, axis=-1, keepdims=True)
    return d * lax.rsqrt(var + eps) * g + b


def _const_spec(shape):
    n = len(shape)
    return pl.BlockSpec(shape, lambda *_: (0,) * n)


def _mod_kernel(c_ref, w_ref, b_ref, o_ref):
    c = c_ref[...]
    c_act = c * _sigmoid(c)
    o_ref[...] = jnp.dot(c_act, w_ref[...], preferred_element_type=F32,
                         precision=lax.Precision.HIGHEST) + b_ref[...]


def _modulation(c, w_ada, b_ada):
    bsz = c.shape[0]
    n = w_ada.shape[1]
    tn = D_MODEL
    return pl.pallas_call(
        _mod_kernel,
        grid=(n // tn,),
        in_specs=[pl.BlockSpec((bsz, D_MODEL), lambda j: (0, 0)),
                  pl.BlockSpec((D_MODEL, tn), lambda j: (0, j)),
                  pl.BlockSpec((1, tn), lambda j: (0, j))],
        out_specs=pl.BlockSpec((bsz, tn), lambda j: (0, j)),
        out_shape=jax.ShapeDtypeStruct((bsz, n), F32),
        name="mod",
    )(c, w_ada, b_ada.reshape(1, n))


def _inproj_kernel(x_ref, mod_ref, wg_ref, bg_ref, wr_ref, br_ref, wgate_ref, bgate_ref,
                   glnv_ref, blnv_ref, ws_ref, bs_ref, mu_ref, w0_ref, a0_ref, wup_ref,
                   kk_ref, ka_ref, bd_ref, wba_ref,
                   ma_ref, gb_ref, r_ref, lw_ref, k_ref, v_ref, a_ref, b_ref, g_ref,
                   zsh_ref):
    tl = x_ref.shape[1]
    x = x_ref[0]
    sh1 = mod_ref[0, 0:1, :]
    sc1 = mod_ref[0, 1:2, :]
    h = (x * (1.0 + sc1) + sh1).astype(BF16)

    zg = _gelu_tanh(_dot(h, wg_ref[...]) + bg_ref[...])
    u = zg[:, :G_WIDTH]
    v = _layer_norm(zg[:, G_WIDTH:], glnv_ref[...], blnv_ref[...], LN_EPS)
    lane = lax.broadcasted_iota(jnp.int32, (CHUNK, LANES), 1)
    first_head = lane < R_HEAD
    trow = lax.broadcasted_iota(jnp.int32, (CHUNK, 2 * CHUNK), 0)
    scol = lax.broadcasted_iota(jnp.int32, (CHUNK, 2 * CHUNK), 1) % CHUNK
    causal = trow >= scol
    ws = [jnp.where(causal, ws_ref[q], 0.0).astype(BF16) for q in range(PAIRS)]
    ya_rows = []
    for c in range(tl // CHUNK):
        vc = v[c * CHUNK:(c + 1) * CHUNK, :]
        s_parts = []
        for q in range(PAIRS):
            v2 = vc[:, q * LANES:(q + 1) * LANES]
            vm = jnp.concatenate([jnp.where(first_head, v2, 0.0),
                                  jnp.where(first_head, 0.0, v2)], axis=0).astype(BF16)
            s_parts.append(_dot(ws[q], vm))
        s = jnp.concatenate(s_parts, axis=1) + bs_ref[...]
        ya_rows.append(u[c * CHUNK:(c + 1) * CHUNK, :] * s)
    ya = jnp.concatenate(ya_rows, axis=0).astype(BF16)
    pa = _dot(ya, wba_ref[...])

    zgate = _dot(h, wgate_ref[...]) + bgate_ref[...]
    ma_ref[0] = _sigmoid(zgate[:, :D_MODEL]) * pa
    gb_ref[0] = _sigmoid(zgate[:, D_MODEL:]).astype(gb_ref.dtype)

    zr = _dot(h, wr_ref[...]) + br_ref[...]

    @pl.when(pl.program_id(1) == 0)
    def _():
        zsh_ref[0:8, :] = jnp.zeros((8, R_COLS), F32)

    zsh_ref[8:tl + 8, :] = zr
    prev = zsh_ref[7:tl + 7, :]
    zsh_ref[7:8, :] = zr[tl - 1:tl, :]
    z = zr + (prev - zr) * mu_ref[...]

    r = z[:, 0:R_WIDTH]
    k = z[:, R_WIDTH:2 * R_WIDTH]
    vv = z[:, 2 * R_WIDTH:3 * R_WIDTH]
    xl = z[:, 3 * R_WIDTH:]
    llane = lax.broadcasted_iota(jnp.int32, xl.shape, 1)
    lin = jnp.where(llane < DECAY_LORA, jnp.tanh(xl),
                    jnp.where(llane < DECAY_LORA + AAA_LORA, xl, _sigmoid(xl)))
    up = _dot(lin.astype(BF16), wup_ref[...])
    w_log = -_softplus(-(w0_ref[...] + up[:, 0:R_WIDTH])) - 0.5
    lw = -jnp.exp(w_log)
    a = _sigmoid(a0_ref[...] + up[:, R_WIDTH:2 * R_WIDTH])
    g_ref[0] = up[:, 2 * R_WIDTH:]

    kk = k * kk_ref[...]
    n2 = _dot((kk * kk).astype(BF16), bd_ref[...])
    kk = kk / jnp.maximum(jnp.sqrt(n2), 1e-12)
    k2 = k * (1.0 + (a - 1.0) * ka_ref[...])
    nkk = -kk
    kka = kk * a
    for q in range(PAIRS):
        sl = slice(q * LANES, (q + 1) * LANES)
        r_ref[0, q] = r[:, sl]
        lw_ref[0, q] = lw[:, sl]
        k_ref[0, q] = k2[:, sl]
        v_ref[0, q] = vv[:, sl]
        a_ref[0, q] = nkk[:, sl]
        b_ref[0, q] = kka[:, sl]


def _inproj(x, mod, p, tl):
    bsz, seq, _ = x.shape
    grid = (bsz, seq // tl)
    row = lambda b, s: (b, s, 0)
    pair = lambda b, s: (b, 0, s, 0)
    pair_shape = jax.ShapeDtypeStruct((bsz, PAIRS, seq, LANES), F32)
    pair_spec = pl.BlockSpec((1, PAIRS, tl, LANES), pair)
    consts = [p["wg"], p["bg"], p["wr"], p["br"], p["wgate"], p["bgate"], p["glnv"], p["blnv"],
              p["ws"], p["bs"], p["mu"], p["w0"], p["a0"], p["wup"], p["kk"], p["ka"], p["bd"],
              p["wba"]]
    return pl.pallas_call(
        _inproj_kernel,
        grid=grid,
        in_specs=[pl.BlockSpec((1, tl, D_MODEL), row),
                  pl.BlockSpec((1, 6, D_MODEL), lambda b, s: (b, 0, 0))]
                 + [_const_spec(c.shape) for c in consts],
        out_specs=[pl.BlockSpec((1, tl, D_MODEL), row),
                   pl.BlockSpec((1, tl, D_MODEL), row)]
                  + [pair_spec] * 6
                  + [pl.BlockSpec((1, tl, R_WIDTH), row)],
        out_shape=[jax.ShapeDtypeStruct((bsz, seq, D_MODEL), F32),
                   jax.ShapeDtypeStruct((bsz, seq, D_MODEL), BF16)]
                  + [pair_shape] * 6
                  + [jax.ShapeDtypeStruct((bsz, seq, R_WIDTH), F32)],
        scratch_shapes=[pltpu.VMEM((tl + 8, R_COLS), F32)],
        compiler_params=pltpu.CompilerParams(
            dimension_semantics=("arbitrary", "arbitrary"), vmem_limit_bytes=VMEM_LIMIT),
        name="inproj",
    )(x, mod, *consts)


def _scan_kernel(r_ref, lw_ref, k_ref, v_ref, a_ref, b_ref, rk_ref, gng_ref, gnb_ref,
                 y_ref,
                 st_ref, pm_ref, qm_ref, ra_ref, bkt_ref, vm_ref, y0_ref, wm_ref):
    bt = r_ref.shape[0]
    lt = r_ref.shape[2]
    nc = lt // SCAN_CHUNK
    cs = SCAN_CHUNK

    @pl.when(pl.program_id(1) == 0)
    def _():
        st_ref[...] = jnp.zeros(st_ref.shape, F32)

    lane = lax.broadcasted_iota(jnp.int32, (cs, LANES), 1)
    first_head = lane < R_HEAD

    def stack(t):
        return jnp.concatenate([jnp.where(first_head, t, 0.0),
                                jnp.where(first_head, 0.0, t)], axis=0)

    ri = lax.broadcasted_iota(jnp.int32, (cs, cs), 0)
    ci = lax.broadcasted_iota(jnp.int32, (cs, cs), 1)
    tri_ones = jnp.where(ri >= ci, 1.0, 0.0).astype(BF16)
    si = lax.broadcasted_iota(jnp.int32, (STACK, STACK), 0)
    sj = lax.broadcasted_iota(jnp.int32, (STACK, STACK), 1)
    strict = (si % cs) > (sj % cs)
    incl = (si % cs) >= (sj % cs)
    eye = jnp.where(si == sj, 1.0, 0.0).astype(F32)

    chains = range(bt)

    def prepare(c, carry):
        rows = pl.ds(pl.multiple_of(c * cs, cs), cs)
        lw = [lw_ref[b, 0, rows, :] for b in chains]
        cum = []
        for b in chains:
            p1 = lw[b].astype(BF16)
            e1 = lw[b] - p1.astype(F32)
            p2 = e1.astype(BF16)
            p3 = (e1 - p2.astype(F32)).astype(BF16)
            cum.append(_dot(tri_ones, p1) + _dot(tri_ones, p2) + _dot(tri_ones, p3))

        am_b, rm_b, vm, bkt, wm, aa = [], [], [], [], [], []
        for b in chains:
            last = cum[b][cs - 1:cs, :]
            e_pos = jnp.exp(cum[b])
            e_pos_ex = jnp.exp(cum[b] - lw[b])
            e_neg = jnp.exp(-cum[b])
            e_end = jnp.exp(last - cum[b])
            wc = jnp.exp(last)
            r = r_ref[b, 0, rows, :]
            k = k_ref[b, 0, rows, :]
            v = v_ref[b, 0, rows, :]
            a = a_ref[b, 0, rows, :]
            bb = b_ref[b, 0, rows, :]
            am_b.append(stack(a * e_pos_ex).astype(BF16))
            rm_b.append(stack(r * e_pos).astype(BF16))
            bk = jnp.concatenate([stack(bb * e_neg), stack(k * e_neg)], axis=0).astype(BF16)
            vm.append(stack(v).astype(BF16))
            bkt.append(jnp.concatenate([stack(bb * e_end).T, stack(k * e_end).T], axis=1).astype(BF16))
            wm.append(jnp.broadcast_to(wc, (LANES, LANES)).T)
            aa.append(_dot_nt(jnp.concatenate([am_b[b], rm_b[b]], axis=0), bk))

        a_ab = [jnp.where(strict, aa[b][:STACK, :STACK], 0.0) for b in chains]
        a_rb = [jnp.where(incl, aa[b][STACK:, :STACK], 0.0).astype(BF16) for b in chains]
        akv_y0 = [_dot(jnp.concatenate([jnp.where(strict, aa[b][:STACK, STACK:], 0.0),
                                        jnp.where(incl, aa[b][STACK:, STACK:], 0.0)],
                                       axis=0).astype(BF16), vm[b]) for b in chains]

        xk = []
        for b in chains:
            ab = a_ab[b].astype(BF16)
            xk.append(_dot(ab, ab))
        tk = [eye + a_ab[b] for b in chains]
        n_sq = (cs - 1).bit_length() - 1
        for step in range(n_sq):
            for b in chains:
                xb = xk[b].astype(BF16)
                if step < n_sq - 1:
                    xt = _dot(jnp.concatenate([xb, tk[b].astype(BF16)], axis=0), xb)
                    xk[b] = xt[:STACK]
                    tk[b] = tk[b] + xt[STACK:]
                else:
                    tk[b] = tk[b] + _dot(tk[b].astype(BF16), xb)

        pq = [_dot(tk[b].astype(BF16),
                   jnp.concatenate([am_b[b], akv_y0[b][:STACK].astype(BF16)], axis=1)) for b in chains]
        for b in chains:
            i = b * nc + c
            pm_ref[i] = pq[b][:, :LANES].astype(BF16)
            qm_ref[i] = pq[b][:, LANES:]
            ra_ref[i] = jnp.concatenate([rm_b[b], a_rb[b]], axis=1)
            bkt_ref[i] = bkt[b]
            vm_ref[i] = vm[b]
            y0_ref[i] = akv_y0[b][STACK:]
            wm_ref[i] = wm[b]
        return carry

    lax.fori_loop(0, nc, prepare, 0)

    bsi = lax.broadcasted_iota(jnp.int32, (LANES, LANES), 0) // R_HEAD
    bsj = lax.broadcasted_iota(jnp.int32, (LANES, LANES), 1) // R_HEAD
    head_ones = jnp.where(bsi == bsj, 1.0, 0.0).astype(BF16)
    head_mean = jnp.where(bsi == bsj, 1.0 / R_HEAD, 0.0).astype(BF16)
    rk = rk_ref[...]
    gng = gng_ref[...]
    gnb = gnb_ref[...]

    def advance(c, carry):
        rows = pl.ds(pl.multiple_of(c * cs, cs), cs)
        idx = [b * nc + c for b in chains]
        st = [st_ref[b] for b in chains]
        st_b = [st[b].astype(BF16) for b in chains]
        u_b = [(_dot(pm_ref[idx[b]], st_b[b]) + qm_ref[idx[b]]).astype(BF16) for b in chains]
        bonus = []
        for b in chains:
            r = r_ref[b, 0, rows, :]
            k = k_ref[b, 0, rows, :]
            bonus.append(_dot((r * k * rk).astype(BF16), head_ones) * v_ref[b, 0, rows, :])
        y2, st_new = [], []
        for b in chains:
            i = idx[b]
            ym = _dot(ra_ref[i], jnp.concatenate([st_b[b], u_b[b]], axis=0)) + y0_ref[i]
            st_new.append(wm_ref[i] * st[b]
                          + _dot(bkt_ref[i], jnp.concatenate([u_b[b], vm_ref[i]], axis=0)))
            y2.append(ym[:cs] + ym[cs:])
        d = [y2[b] - _dot(y2[b].astype(BF16), head_mean) for b in chains]
        var = [_dot((d[b] * d[b]).astype(BF16), head_mean) for b in chains]
        for b in chains:
            st_ref[b] = st_new[b]
            y_ref[b, 0, rows, :] = d[b] * lax.rsqrt(var[b] + GN_EPS) * gng + gnb + bonus[b]
        return carry

    lax.fori_loop(0, nc, advance, 0)


def _scan(r, lw, k, v, a, b, rk, gng, gnb, lt):
    bsz, _, seq, _ = r.shape
    nc = lt // SCAN_CHUNK
    n = bsz * nc
    blk = pl.BlockSpec((bsz, 1, lt, LANES), lambda q, s: (0, q, s, 0))
    vec = pl.BlockSpec((1, LANES), lambda q, s: (0, q))
    return pl.pallas_call(
        _scan_kernel,
        grid=(PAIRS, seq // lt),
        in_specs=[blk] * 6 + [vec] * 3,
        out_specs=blk,
        out_shape=jax.ShapeDtypeStruct((bsz, PAIRS, seq, LANES), F32),
        scratch_shapes=[pltpu.VMEM((bsz, LANES, LANES), F32),
                        pltpu.VMEM((n, STACK, LANES), BF16),
                        pltpu.VMEM((n, STACK, LANES), F32),
                        pltpu.VMEM((n, STACK, 2 * LANES), BF16),
                        pltpu.VMEM((n, LANES, 2 * STACK), BF16),
                        pltpu.VMEM((n, STACK, LANES), BF16),
                        pltpu.VMEM((n, STACK, LANES), F32),
                        pltpu.VMEM((n, LANES, LANES), F32)],
        compiler_params=pltpu.CompilerParams(
            dimension_semantics=("arbitrary", "arbitrary"), vmem_limit_bytes=VMEM_LIMIT),
        name="scan",
    )(r, lw, k, v, a, b, rk, gng, gnb)


def _mix_kernel(x_ref, mod_ref, yb_ref, g_ref, gb_ref, ma_ref, wbb_ref, wout_ref, bout_ref,
                lng_ref, lnb_ref, h1_ref):
    yb = jnp.concatenate([yb_ref[0, q] for q in range(PAIRS)], axis=1) * g_ref[0]
    pb = _dot(yb.astype(BF16), wbb_ref[...])
    merged = ma_ref[0] + gb_ref[0].astype(F32) * pb
    mix = _dot(merged.astype(BF16), wout_ref[...]) + bout_ref[...]
    gt1 = mod_ref[0, 2:3, :]
    h1_ref[0] = _layer_norm(ALPHA * x_ref[0] + gt1 * mix, lng_ref[...], lnb_ref[...], LN_EPS)


def _mix(x, mod, yb, g, gb, ma, p, tl):
    bsz, seq, _ = x.shape
    row = lambda b, s: (b, s, 0)
    consts = [p["wbb"], p["wout"], p["bout"], p["ln1g"], p["ln1b"]]
    return pl.pallas_call(
        _mix_kernel,
        grid=(bsz, seq // tl),
        in_specs=[pl.BlockSpec((1, tl, D_MODEL), row),
                  pl.BlockSpec((1, 6, D_MODEL), lambda b, s: (b, 0, 0)),
                  pl.BlockSpec((1, PAIRS, tl, LANES), lambda b, s: (b, 0, s, 0)),
                  pl.BlockSpec((1, tl, R_WIDTH), row),
                  pl.BlockSpec((1, tl, D_MODEL), row),
                  pl.BlockSpec((1, tl, D_MODEL), row)]
                 + [_const_spec(c.shape) for c in consts],
        out_specs=pl.BlockSpec((1, tl, D_MODEL), row),
        out_shape=jax.ShapeDtypeStruct((bsz, seq, D_MODEL), F32),
        compiler_params=pltpu.CompilerParams(
            dimension_semantics=("arbitrary", "arbitrary"), vmem_limit_bytes=VMEM_LIMIT),
        name="mix",
    )(x, mod, yb, g, gb, ma, *consts)


def _ffn_kernel(h1_ref, mod_ref, w1_ref, b1_ref, w2_ref, b2_ref, lng_ref, lnb_ref, o_ref):
    h1 = h1_ref[0]
    sh2 = mod_ref[0, 3:4, :]
    sc2 = mod_ref[0, 4:5, :]
    gt2 = mod_ref[0, 5:6, :]
    h = (h1 * (1.0 + sc2) + sh2).astype(BF16)
    t = jnp.maximum(_dot(h, w1_ref[...]) + b1_ref[...], 0.0)
    ff = _dot((t * t).astype(BF16), w2_ref[...]) + b2_ref[...]
    o_ref[0] = _layer_norm(ALPHA * h1 + gt2 * ff, lng_ref[...], lnb_ref[...], LN_EPS).astype(o_ref.dtype)


def _ffn(h1, mod, p, tl, out_dtype):
    bsz, seq, _ = h1.shape
    row = lambda b, s: (b, s, 0)
    consts = [p["w1"], p["b1"], p["w2"], p["b2"], p["ln2g"], p["ln2b"]]
    return pl.pallas_call(
        _ffn_kernel,
        grid=(bsz, seq // tl),
        in_specs=[pl.BlockSpec((1, tl, D_MODEL), row),
                  pl.BlockSpec((1, 6, D_MODEL), lambda b, s: (b, 0, 0))]
                 + [_const_spec(c.shape) for c in consts],
        out_specs=pl.BlockSpec((1, tl, D_MODEL), row),
        out_shape=jax.ShapeDtypeStruct((bsz, seq, D_MODEL), out_dtype),
        compiler_params=pltpu.CompilerParams(
            dimension_semantics=("arbitrary", "arbitrary"), vmem_limit_bytes=VMEM_LIMIT),
        name="ffn",
    )(h1, mod, *consts)


def _prepare_params(l, w_in, b_in, g_ln_v, b_ln_v, w_spatial, b_spatial, mu_shift, w0,
                    w_decay_up, a0, w_aaa_up, w_gate_up, k_k, k_a, r_k, gn_gain, gn_bias,
                    w_branch_a, w_branch_b, w_out, b_out, ln1_g, ln1_b, w_ff1, b_ff1, w_ff2,
                    b_ff2, ln2_g, ln2_b):
    g_end = 2 * G_WIDTH
    rkv_end = g_end + 3 * R_WIDTH
    r_end = rkv_end + LORA_COLS
    pad = LORA_PAD - LORA_COLS
    row2 = lambda t: t.reshape(1, -1)
    wi, bi = w_in[l], b_in[l]
    wr = jnp.concatenate([wi[:, g_end:r_end], jnp.zeros((D_MODEL, pad), F32)], axis=1)
    br = jnp.concatenate([bi[g_end:r_end], jnp.zeros((pad,), F32)])
    mu = jnp.concatenate([mu_shift[l], jnp.zeros((pad,), F32)])
    wup = jnp.zeros((LORA_PAD, 3 * R_WIDTH), F32)
    wup = wup.at[0:DECAY_LORA, 0:R_WIDTH].set(w_decay_up[l])
    wup = wup.at[DECAY_LORA:DECAY_LORA + AAA_LORA, R_WIDTH:2 * R_WIDTH].set(w_aaa_up[l])
    wup = wup.at[DECAY_LORA + AAA_LORA:LORA_COLS, 2 * R_WIDTH:].set(w_gate_up[l])
    ws = w_spatial[l].reshape(PAIRS, 2, CHUNK, CHUNK).transpose(0, 2, 1, 3).reshape(PAIRS, CHUNK, 2 * CHUNK)
    bs = jnp.repeat(b_spatial[l].T, G_WIDTH // G_GROUPS, axis=1)
    hid = jnp.arange(R_WIDTH) // R_HEAD
    bd = (hid[:, None] == hid[None, :]).astype(BF16)
    return dict(
        wg=wi[:, :g_end].astype(BF16), bg=row2(bi[:g_end]),
        wr=wr.astype(BF16), br=row2(br),
        wgate=wi[:, r_end:].astype(BF16), bgate=row2(bi[r_end:]),
        glnv=row2(g_ln_v[l]), blnv=row2(b_ln_v[l]), ws=ws, bs=bs, mu=row2(mu),
        w0=row2(w0[l]), a0=row2(a0[l]), wup=wup.astype(BF16), kk=row2(k_k[l]), ka=row2(k_a[l]),
        bd=bd, wba=w_branch_a[l].astype(BF16),
        rk=row2(r_k[l]), gng=row2(gn_gain[l]), gnb=row2(gn_bias[l]),
        wbb=w_branch_b[l].astype(BF16), wout=w_out[l].astype(BF16), bout=row2(b_out[l]),
        ln1g=row2(ln1_g[l]), ln1b=row2(ln1_b[l]),
        w1=w_ff1[l].astype(BF16), b1=row2(b_ff1[l]), w2=w_ff2[l].astype(BF16), b2=row2(b_ff2[l]),
        ln2g=row2(ln2_g[l]), ln2b=row2(ln2_b[l]),
    )


def _tile(seq, want):
    t = min(want, seq)
    while seq % t:
        t //= 2
    return t


def kernel(x, c, w_ada, b_ada, w_in, b_in, g_ln_v, b_ln_v, w_spatial, b_spatial, mu_shift, w0, w_decay_up, a0, w_aaa_up, w_gate_up, k_k, k_a, r_k, gn_gain, gn_bias, w_branch_a, w_branch_b, w_out, b_out, ln1_g, ln1_b, w_ff1, b_ff1, w_ff2, b_ff2, ln2_g, ln2_b):
    bsz, seq, _ = x.shape
    assert seq % CHUNK == 0 and x.shape[2] == D_MODEL
    out_dtype = x.dtype
    h_res = x.astype(F32)
    tl = _tile(seq, 256)
    for l in range(DEPTH):
        p = _prepare_params(l, w_in, b_in, g_ln_v, b_ln_v, w_spatial, b_spatial, mu_shift, w0,
                            w_decay_up, a0, w_aaa_up, w_gate_up, k_k, k_a, r_k, gn_gain, gn_bias,
                            w_branch_a, w_branch_b, w_out, b_out, ln1_g, ln1_b, w_ff1, b_ff1,
                            w_ff2, b_ff2, ln2_g, ln2_b)
        mod = _modulation(c.astype(F32), w_ada[l], b_ada[l]).reshape(bsz, 6, D_MODEL)
        ma, gb, r, lw, k, v, a, b, g = _inproj(h_res, mod, p, tl)
        yb = _scan(r, lw, k, v, a, b, p["rk"], p["gng"], p["gnb"], _tile(seq, 256))
        h1 = _mix(h_res, mod, yb, g, gb, ma, p, tl)
        h_res = _ffn(h1, mod, p, tl, F32)
    return h_res.astype(out_dtype)
```

```python
import functools

import jax
import jax.numpy as jnp
from jax import lax
from jax.experimental import pallas as pl
from jax.experimental.pallas import tpu as pltpu

D_MODEL = 1024
G_GROUPS = 8
G_WIDTH = 512
CHUNK = 128
R_WIDTH = 512
R_HEAD = 64
R_HEADS = R_WIDTH // R_HEAD
DECAY_LORA = 32
AAA_LORA = 32
GATE_LORA = 96
LORA_COLS = DECAY_LORA + AAA_LORA + GATE_LORA
D_FF = 4 * D_MODEL
DEPTH = 1
ALPHA = (2.0 * DEPTH) ** 0.25
LN_EPS = 1e-5
GN_EPS = 64e-5

LANES = 128
PAIRS = R_WIDTH // LANES
LORA_PAD = 2 * LANES
R_COLS = 3 * R_WIDTH + LORA_PAD
SCAN_CHUNK = 64
STACK = 2 * SCAN_CHUNK
SCAN_TILE = 128
SCAN_PAIRS = 2
VMEM_LIMIT = 56 * 1024 * 1024

F32 = jnp.float32
BF16 = jnp.bfloat16


def _dot(a, b):
    return jnp.dot(a, b, preferred_element_type=F32)


def _dot_nt(a, b):
    return lax.dot_general(a, b, (((1,), (1,)), ((), ())), preferred_element_type=F32)


def _sigmoid(x):
    return 1.0 / (1.0 + jnp.exp(-x))


def _softplus(x):
    return jnp.maximum(x, 0.0) + jnp.log(1.0 + jnp.exp(-jnp.abs(x)))


def _gelu_tanh(x):
    return 0.5 * x * (1.0 + jnp.tanh(0.7978845608028654 * (x + 0.044715 * (x * x * x))))


def _layer_norm(x, g, b, eps):
    mu = jnp.mean(x, axis=-1, keepdims=True)
    d = x - mu
    var = jnp.mean(d * d, axis=-1, keepdims=True)
    return d * lax.rsqrt(var + eps) * g + b


def _const_spec(shape):
    n = len(shape)
    return pl.BlockSpec(shape, lambda *_: (0,) * n)


def _mod_kernel(c_ref, w_ref, b_ref, o_ref):
    c = c_ref[...]
    c_act = c * _sigmoid(c)
    o_ref[...] = jnp.dot(c_act, w_ref[...], preferred_element_type=F32,
                         precision=lax.Precision.HIGHEST) + b_ref[...]


def _modulation(c, w_ada, b_ada):
    bsz = c.shape[0]
    n = w_ada.shape[1]
    tn = D_MODEL
    return pl.pallas_call(
        _mod_kernel,
        grid=(n // tn,),
        in_specs=[pl.BlockSpec((bsz, D_MODEL), lambda j: (0, 0)),
                  pl.BlockSpec((D_MODEL, tn), lambda j: (0, j)),
                  pl.BlockSpec((1, tn), lambda j: (0, j))],
        out_specs=pl.BlockSpec((bsz, tn), lambda j: (0, j)),
        out_shape=jax.ShapeDtypeStruct((bsz, n), F32),
        name="mod",
    )(c, w_ada, b_ada.reshape(1, n))


def _inproj_kernel(x_ref, mod_ref, wg_ref, bg_ref, wr_ref, br_ref, wgate_ref, bgate_ref,
                   glnv_ref, blnv_ref, ws_ref, bs_ref, mu_ref, w0_ref, a0_ref, wup_ref,
                   kk_ref, ka_ref, bd_ref, wba_ref,
                   ma_ref, gb_ref, r_ref, lw_ref, k_ref, v_ref, a_ref, b_ref, g_ref,
                   zsh_ref):
    tl = x_ref.shape[1]
    x = x_ref[0]
    sh1 = mod_ref[0, 0:1, :]
    sc1 = mod_ref[0, 1:2, :]
    h = (x * (1.0 + sc1) + sh1).astype(BF16)

    zg = _gelu_tanh(_dot(h, wg_ref[...]) + bg_ref[...])
    u = zg[:, :G_WIDTH]
    v = _layer_norm(zg[:, G_WIDTH:], glnv_ref[...], blnv_ref[...], LN_EPS)
    lane = lax.broadcasted_iota(jnp.int32, (CHUNK, LANES), 1)
    first_head = lane < R_HEAD
    trow = lax.broadcasted_iota(jnp.int32, (CHUNK, 2 * CHUNK), 0)
    scol = lax.broadcasted_iota(jnp.int32, (CHUNK, 2 * CHUNK), 1) % CHUNK
    causal = trow >= scol
    ws = [jnp.where(causal, ws_ref[q], 0.0).astype(BF16) for q in range(PAIRS)]
    ya_rows = []
    for c in range(tl // CHUNK):
        vc = v[c * CHUNK:(c + 1) * CHUNK, :]
        s_parts = []
        for q in range(PAIRS):
            v2 = vc[:, q * LANES:(q + 1) * LANES]
            vm = jnp.concatenate([jnp.where(first_head, v2, 0.0),
                                  jnp.where(first_head, 0.0, v2)], axis=0).astype(BF16)
            s_parts.append(_dot(ws[q], vm))
        s = jnp.concatenate(s_parts, axis=1) + bs_ref[...]
        ya_rows.append(u[c * CHUNK:(c + 1) * CHUNK, :] * s)
    ya = jnp.concatenate(ya_rows, axis=0).astype(BF16)
    pa = _dot(ya, wba_ref[...])

    zgate = _dot(h, wgate_ref[...]) + bgate_ref[...]
    ma_ref[0] = _sigmoid(zgate[:, :D_MODEL]) * pa
    gb_ref[0] = _sigmoid(zgate[:, D_MODEL:]).astype(gb_ref.dtype)

    zr = _dot(h, wr_ref[...]) + br_ref[...]

    @pl.when(pl.program_id(1) == 0)
    def _():
        zsh_ref[0:8, :] = jnp.zeros((8, R_COLS), F32)

    zsh_ref[8:tl + 8, :] = zr
    prev = zsh_ref[7:tl + 7, :]
    zsh_ref[7:8, :] = zr[tl - 1:tl, :]
    z = zr + (prev - zr) * mu_ref[...]

    r = z[:, 0:R_WIDTH]
    k = z[:, R_WIDTH:2 * R_WIDTH]
    vv = z[:, 2 * R_WIDTH:3 * R_WIDTH]
    xl = z[:, 3 * R_WIDTH:]
    llane = lax.broadcasted_iota(jnp.int32, xl.shape, 1)
    lin = jnp.where(llane < DECAY_LORA, jnp.tanh(xl),
                    jnp.where(llane < DECAY_LORA + AAA_LORA, xl, _sigmoid(xl)))
    up = _dot(lin.astype(BF16), wup_ref[...])
    w_log = -_softplus(-(w0_ref[...] + up[:, 0:R_WIDTH])) - 0.5
    lw = -jnp.exp(w_log)
    a = _sigmoid(a0_ref[...] + up[:, R_WIDTH:2 * R_WIDTH])
    g_ref[0] = up[:, 2 * R_WIDTH:]

    kk = k * kk_ref[...]
    n2 = _dot((kk * kk).astype(BF16), bd_ref[...])
    kk = kk / jnp.maximum(jnp.sqrt(n2), 1e-12)
    k2 = k * (1.0 + (a - 1.0) * ka_ref[...])
    nkk = -kk
    kka = kk * a
    for q in range(PAIRS):
        sl = slice(q * LANES, (q + 1) * LANES)
        r_ref[0, q] = r[:, sl]
        lw_ref[0, q] = lw[:, sl]
        k_ref[0, q] = k2[:, sl]
        v_ref[0, q] = vv[:, sl]
        a_ref[0, q] = nkk[:, sl]
        b_ref[0, q] = kka[:, sl]


def _inproj(x, mod, p, tl):
    bsz, seq, _ = x.shape
    grid = (bsz, seq // tl)
    row = lambda b, s: (b, s, 0)
    pair = lambda b, s: (b, 0, s, 0)
    pair_shape = jax.ShapeDtypeStruct((bsz, PAIRS, seq, LANES), F32)
    pair_spec = pl.BlockSpec((1, PAIRS, tl, LANES), pair)
    consts = [p["wg"], p["bg"], p["wr"], p["br"], p["wgate"], p["bgate"], p["glnv"], p["blnv"],
              p["ws"], p["bs"], p["mu"], p["w0"], p["a0"], p["wup"], p["kk"], p["ka"], p["bd"],
              p["wba"]]
    return pl.pallas_call(
        _inproj_kernel,
        grid=grid,
        in_specs=[pl.BlockSpec((1, tl, D_MODEL), row),
                  pl.BlockSpec((1, 6, D_MODEL), lambda b, s: (b, 0, 0))]
                 + [_const_spec(c.shape) for c in consts],
        out_specs=[pl.BlockSpec((1, tl, D_MODEL), row),
                   pl.BlockSpec((1, tl, D_MODEL), row)]
                  + [pair_spec] * 6
                  + [pl.BlockSpec((1, tl, R_WIDTH), row)],
        out_shape=[jax.ShapeDtypeStruct((bsz, seq, D_MODEL), F32),
                   jax.ShapeDtypeStruct((bsz, seq, D_MODEL), BF16)]
                  + [pair_shape] * 6
                  + [jax.ShapeDtypeStruct((bsz, seq, R_WIDTH), F32)],
        scratch_shapes=[pltpu.VMEM((tl + 8, R_COLS), F32)],
        compiler_params=pltpu.CompilerParams(
            dimension_semantics=("arbitrary", "arbitrary"), vmem_limit_bytes=VMEM_LIMIT),
        name="inproj",
    )(x, mod, *consts)


def _scan_kernel(r_ref, lw_ref, k_ref, v_ref, a_ref, b_ref, rk_ref, gng_ref, gnb_ref,
                 y_ref,
                 st_ref, pm_ref, qm_ref, ra_ref, bkt_ref, vm_ref, y0_ref, wm_ref):
    bt, pp, lt, _ = r_ref.shape
    nc = lt // SCAN_CHUNK
    cs = SCAN_CHUNK

    @pl.when(pl.program_id(1) == 0)
    def _():
        st_ref[...] = jnp.zeros(st_ref.shape, F32)

    lane = lax.broadcasted_iota(jnp.int32, (cs, LANES), 1)
    first_head = lane < R_HEAD

    def stack(t):
        return jnp.concatenate([jnp.where(first_head, t, 0.0),
                                jnp.where(first_head, 0.0, t)], axis=0)

    ri = lax.broadcasted_iota(jnp.int32, (cs, cs), 0)
    ci = lax.broadcasted_iota(jnp.int32, (cs, cs), 1)
    tri_ones = jnp.where(ri >= ci, 1.0, 0.0).astype(BF16)
    si = lax.broadcasted_iota(jnp.int32, (cs, LANES), 0)
    sj = lax.broadcasted_iota(jnp.int32, (cs, LANES), 1) % cs
    strict = si > sj
    incl = si >= sj
    eye = jnp.where(si == sj, 1.0, 0.0).astype(F32)

    chains = [(b, q) for q in range(pp) for b in range(bt)]
    ids = range(len(chains))

    def prepare(c, carry):
        rows = pl.ds(pl.multiple_of(c * cs, cs), cs)
        lw = [lw_ref[b, q, rows, :] for b, q in chains]
        cum = []
        for j in ids:
            p1 = lw[j].astype(BF16)
            e1 = lw[j] - p1.astype(F32)
            p2 = e1.astype(BF16)
            p3 = (e1 - p2.astype(F32)).astype(BF16)
            cum.append(_dot(tri_ones, p1) + _dot(tri_ones, p2) + _dot(tri_ones, p3))

        at, rt_b, vm, bkt, wm, aa = [], [], [], [], [], []
        for j, (b, q) in enumerate(chains):
            last = cum[j][cs - 1:cs, :]
            e_pos = jnp.exp(cum[j])
            e_pos_ex = jnp.exp(cum[j] - lw[j])
            e_neg = jnp.exp(-cum[j])
            e_end = jnp.exp(last - cum[j])
            wc = jnp.exp(last)
            r = r_ref[b, q, rows, :]
            k = k_ref[b, q, rows, :]
            v = v_ref[b, q, rows, :]
            a = a_ref[b, q, rows, :]
            bb = b_ref[b, q, rows, :]
            at.append(a * e_pos_ex)
            rt_b.append((r * e_pos).astype(BF16))
            bk = jnp.concatenate([stack(bb * e_neg), stack(k * e_neg)], axis=0).astype(BF16)
            vm.append(stack(v).astype(BF16))
            bkt.append(jnp.concatenate([stack(bb * e_end).T, stack(k * e_end).T], axis=1).astype(BF16))
            wm.append(jnp.broadcast_to(wc, (LANES, LANES)).T)
            aa.append(_dot_nt(jnp.concatenate([at[j].astype(BF16), rt_b[j]], axis=0), bk))

        a_ab = [jnp.where(strict, aa[j][:cs, :LANES], 0.0) for j in ids]
        a_rb = [jnp.where(incl, aa[j][cs:, :LANES], 0.0).astype(BF16) for j in ids]
        akv_y0 = [_dot(jnp.concatenate([jnp.where(strict, aa[j][:cs, LANES:], 0.0),
                                        jnp.where(incl, aa[j][cs:, LANES:], 0.0)],
                                       axis=0).astype(BF16), vm[j]) for j in ids]

        xk = [_dot(a_ab[j].astype(BF16), stack(a_ab[j]).astype(BF16)) for j in ids]
        tk = [eye + a_ab[j] for j in ids]
        n_sq = (cs - 1).bit_length() - 1
        for step in range(n_sq):
            for j in ids:
                x_bd = stack(xk[j]).astype(BF16)
                if step < n_sq - 1:
                    xt = _dot(jnp.concatenate([xk[j], tk[j]], axis=0).astype(BF16), x_bd)
                    xk[j] = xt[:cs]
                    tk[j] = tk[j] + xt[cs:]
                else:
                    tk[j] = tk[j] + _dot(tk[j].astype(BF16), x_bd)

        pq = [_dot(tk[j].astype(BF16),
                   jnp.concatenate([stack(at[j]), stack(akv_y0[j][:cs])], axis=1).astype(BF16))
              for j in ids]
        for j in ids:
            i = j * nc + c
            pm_ref[i] = pq[j][:, :LANES].astype(BF16)
            qm_ref[i] = pq[j][:, LANES:]
            ra_ref[i] = jnp.concatenate([rt_b[j], a_rb[j]], axis=1)
            bkt_ref[i] = bkt[j]
            vm_ref[i] = vm[j]
            y0_ref[i] = akv_y0[j][cs:]
            wm_ref[i] = wm[j]
        return carry

    lax.fori_loop(0, nc, prepare, 0)

    def advance(c, carry):
        rows = pl.ds(pl.multiple_of(c * cs, cs), cs)
        idx = [j * nc + c for j in ids]
        st = [st_ref[j] for j in ids]
        st_b = [st[j].astype(BF16) for j in ids]
        u_b = [stack(_dot(pm_ref[idx[j]], st_b[j]) + qm_ref[idx[j]]).astype(BF16) for j in ids]
        y2, st_new = [], []
        for j in ids:
            i = idx[j]
            y2.append(_dot(ra_ref[i], jnp.concatenate([st_b[j], u_b[j]], axis=0)) + y0_ref[i])
            st_new.append(wm_ref[i] * st[j]
                          + _dot(bkt_ref[i], jnp.concatenate([u_b[j], vm_ref[i]], axis=0)))
        for j, (b, q) in enumerate(chains):
            st_ref[j] = st_new[j]
            y_ref[b, q, rows, :] = y2[j]
        return carry

    lax.fori_loop(0, nc, advance, 0)

    bsi = lax.broadcasted_iota(jnp.int32, (LANES, LANES), 0) // R_HEAD
    bsj = lax.broadcasted_iota(jnp.int32, (LANES, LANES), 1) // R_HEAD
    head_ones = jnp.where(bsi == bsj, 1.0, 0.0).astype(BF16)
    head_mean = jnp.where(bsi == bsj, 1.0 / R_HEAD, 0.0).astype(BF16)
    y = [y_ref[b, q] for b, q in chains]
    d = [y[j] - _dot(y[j].astype(BF16), head_mean) for j in ids]
    var = [_dot((d[j] * d[j]).astype(BF16), head_mean) for j in ids]
    for j, (b, q) in enumerate(chains):
        lanes = slice(q * LANES, (q + 1) * LANES)
        rkr = (r_ref[b, q] * k_ref[b, q] * rk_ref[:, lanes]).astype(BF16)
        bonus = _dot(rkr, head_ones) * v_ref[b, q]
        y_ref[b, q] = (d[j] * lax.rsqrt(var[j] + GN_EPS) * gng_ref[:, lanes] + gnb_ref[:, lanes]
                       + bonus)


def _scan(r, lw, k, v, a, b, rk, gng, gnb, lt, pp):
    bsz, _, seq, _ = r.shape
    nc = lt // SCAN_CHUNK
    n = bsz * pp * nc
    blk = pl.BlockSpec((bsz, pp, lt, LANES), lambda q, s: (0, q, s, 0))
    vec = pl.BlockSpec((1, pp * LANES), lambda q, s: (0, q))
    return pl.pallas_call(
        _scan_kernel,
        grid=(PAIRS // pp, seq // lt),
        in_specs=[blk] * 6 + [vec] * 3,
        out_specs=blk,
        out_shape=jax.ShapeDtypeStruct((bsz, PAIRS, seq, LANES), F32),
        scratch_shapes=[pltpu.VMEM((bsz * pp, LANES, LANES), F32),
                        pltpu.VMEM((n, SCAN_CHUNK, LANES), BF16),
                        pltpu.VMEM((n, SCAN_CHUNK, LANES), F32),
                        pltpu.VMEM((n, SCAN_CHUNK, 2 * LANES), BF16),
                        pltpu.VMEM((n, LANES, 2 * STACK), BF16),
                        pltpu.VMEM((n, STACK, LANES), BF16),
                        pltpu.VMEM((n, SCAN_CHUNK, LANES), F32),
                        pltpu.VMEM((n, LANES, LANES), F32)],
        compiler_params=pltpu.CompilerParams(
            dimension_semantics=("arbitrary", "arbitrary"), vmem_limit_bytes=VMEM_LIMIT),
        name="scan",
    )(r, lw, k, v, a, b, rk, gng, gnb)


def _mix_kernel(x_ref, mod_ref, yb_ref, g_ref, gb_ref, ma_ref, wbb_ref, wout_ref, bout_ref,
                lng_ref, lnb_ref, h1_ref):
    yb = jnp.concatenate([yb_ref[0, q] for q in range(PAIRS)], axis=1) * g_ref[0]
    pb = _dot(yb.astype(BF16), wbb_ref[...])
    merged = ma_ref[0] + gb_ref[0].astype(F32) * pb
    mix = _dot(merged.astype(BF16), wout_ref[...]) + bout_ref[...]
    gt1 = mod_ref[0, 2:3, :]
    h1_ref[0] = _layer_norm(ALPHA * x_ref[0] + gt1 * mix, lng_ref[...], lnb_ref[...], LN_EPS)


def _mix(x, mod, yb, g, gb, ma, p, tl):
    bsz, seq, _ = x.shape
    row = lambda b, s: (b, s, 0)
    consts = [p["wbb"], p["wout"], p["bout"], p["ln1g"], p["ln1b"]]
    return pl.pallas_call(
        _mix_kernel,
        grid=(bsz, seq // tl),
        in_specs=[pl.BlockSpec((1, tl, D_MODEL), row),
                  pl.BlockSpec((1, 6, D_MODEL), lambda b, s: (b, 0, 0)),
                  pl.BlockSpec((1, PAIRS, tl, LANES), lambda b, s: (b, 0, s, 0)),
                  pl.BlockSpec((1, tl, R_WIDTH), row),
                  pl.BlockSpec((1, tl, D_MODEL), row),
                  pl.BlockSpec((1, tl, D_MODEL), row)]
                 + [_const_spec(c.shape) for c in consts],
        out_specs=pl.BlockSpec((1, tl, D_MODEL), row),
        out_shape=jax.ShapeDtypeStruct((bsz, seq, D_MODEL), F32),
        compiler_params=pltpu.CompilerParams(
            dimension_semantics=("arbitrary", "arbitrary"), vmem_limit_bytes=VMEM_LIMIT),
        name="mix",
    )(x, mod, yb, g, gb, ma, *consts)


def _ffn_kernel(h1_ref, mod_ref, w1_ref, b1_ref, w2_ref, b2_ref, lng_ref, lnb_ref, o_ref):
    h1 = h1_ref[0]
    sh2 = mod_ref[0, 3:4, :]
    sc2 = mod_ref[0, 4:5, :]
    gt2 = mod_ref[0, 5:6, :]
    h = (h1 * (1.0 + sc2) + sh2).astype(BF16)
    t = jnp.maximum(_dot(h, w1_ref[...]) + b1_ref[...], 0.0)
    ff = _dot((t * t).astype(BF16), w2_ref[...]) + b2_ref[...]
    o_ref[0] = _layer_norm(ALPHA * h1 + gt2 * ff, lng_ref[...], lnb_ref[...], LN_EPS).astype(o_ref.dtype)


def _ffn(h1, mod, p, tl, out_dtype):
    bsz, seq, _ = h1.shape
    row = lambda b, s: (b, s, 0)
    consts = [p["w1"], p["b1"], p["w2"], p["b2"], p["ln2g"], p["ln2b"]]
    return pl.pallas_call(
        _ffn_kernel,
        grid=(bsz, seq // tl),
        in_specs=[pl.BlockSpec((1, tl, D_MODEL), row),
                  pl.BlockSpec((1, 6, D_MODEL), lambda b, s: (b, 0, 0))]
                 + [_const_spec(c.shape) for c in consts],
        out_specs=pl.BlockSpec((1, tl, D_MODEL), row),
        out_shape=jax.ShapeDtypeStruct((bsz, seq, D_MODEL), out_dtype),
        compiler_params=pltpu.CompilerParams(
            dimension_semantics=("arbitrary", "arbitrary"), vmem_limit_bytes=VMEM_LIMIT),
        name="ffn",
    )(h1, mod, *consts)


def _prepare_params(l, w_in, b_in, g_ln_v, b_ln_v, w_spatial, b_spatial, mu_shift, w0,
                    w_decay_up, a0, w_aaa_up, w_gate_up, k_k, k_a, r_k, gn_gain, gn_bias,
                    w_branch_a, w_branch_b, w_out, b_out, ln1_g, ln1_b, w_ff1, b_ff1, w_ff2,
                    b_ff2, ln2_g, ln2_b):
    g_end = 2 * G_WIDTH
    rkv_end = g_end + 3 * R_WIDTH
    r_end = rkv_end + LORA_COLS
    pad = LORA_PAD - LORA_COLS
    row2 = lambda t: t.reshape(1, -1)
    wi, bi = w_in[l], b_in[l]
    wr = jnp.concatenate([wi[:, g_end:r_end], jnp.zeros((D_MODEL, pad), F32)], axis=1)
    br = jnp.concatenate([bi[g_end:r_end], jnp.zeros((pad,), F32)])
    mu = jnp.concatenate([mu_shift[l], jnp.zeros((pad,), F32)])
    wup = jnp.zeros((LORA_PAD, 3 * R_WIDTH), F32)
    wup = wup.at[0:DECAY_LORA, 0:R_WIDTH].set(w_decay_up[l])
    wup = wup.at[DECAY_LORA:DECAY_LORA + AAA_LORA, R_WIDTH:2 * R_WIDTH].set(w_aaa_up[l])
    wup = wup.at[DECAY_LORA + AAA_LORA:LORA_COLS, 2 * R_WIDTH:].set(w_gate_up[l])
    ws = w_spatial[l].reshape(PAIRS, 2, CHUNK, CHUNK).transpose(0, 2, 1, 3).reshape(PAIRS, CHUNK, 2 * CHUNK)
    bs = jnp.repeat(b_spatial[l].T, G_WIDTH // G_GROUPS, axis=1)
    hid = jnp.arange(R_WIDTH) // R_HEAD
    bd = (hid[:, None] == hid[None, :]).astype(BF16)
    return dict(
        wg=wi[:, :g_end].astype(BF16), bg=row2(bi[:g_end]),
        wr=wr.astype(BF16), br=row2(br),
        wgate=wi[:, r_end:].astype(BF16), bgate=row2(bi[r_end:]),
        glnv=row2(g_ln_v[l]), blnv=row2(b_ln_v[l]), ws=ws, bs=bs, mu=row2(mu),
        w0=row2(w0[l]), a0=row2(a0[l]), wup=wup.astype(BF16), kk=row2(k_k[l]), ka=row2(k_a[l]),
        bd=bd, wba=w_branch_a[l].astype(BF16),
        rk=row2(r_k[l]), gng=row2(gn_gain[l]), gnb=row2(gn_bias[l]),
        wbb=w_branch_b[l].astype(BF16), wout=w_out[l].astype(BF16), bout=row2(b_out[l]),
        ln1g=row2(ln1_g[l]), ln1b=row2(ln1_b[l]),
        w1=w_ff1[l].astype(BF16), b1=row2(b_ff1[l]), w2=w_ff2[l].astype(BF16), b2=row2(b_ff2[l]),
        ln2g=row2(ln2_g[l]), ln2b=row2(ln2_b[l]),
    )


def _tile(seq, want):
    t = min(want, seq)
    while seq % t:
        t //= 2
    return t


def kernel(x, c, w_ada, b_ada, w_in, b_in, g_ln_v, b_ln_v, w_spatial, b_spatial, mu_shift, w0, w_decay_up, a0, w_aaa_up, w_gate_up, k_k, k_a, r_k, gn_gain, gn_bias, w_branch_a, w_branch_b, w_out, b_out, ln1_g, ln1_b, w_ff1, b_ff1, w_ff2, b_ff2, ln2_g, ln2_b):
    bsz, seq, _ = x.shape
    assert seq % CHUNK == 0 and x.shape[2] == D_MODEL
    out_dtype = x.dtype
    h_res = x.astype(F32)
    tl = _tile(seq, 256)
    for l in range(DEPTH):
        p = _prepare_params(l, w_in, b_in, g_ln_v, b_ln_v, w_spatial, b_spatial, mu_shift, w0,
                            w_decay_up, a0, w_aaa_up, w_gate_up, k_k, k_a, r_k, gn_gain, gn_bias,
                            w_branch_a, w_branch_b, w_out, b_out, ln1_g, ln1_b, w_ff1, b_ff1,
                            w_ff2, b_ff2, ln2_g, ln2_b)
        mod = _modulation(c.astype(F32), w_ada[l], b_ada[l]).reshape(bsz, 6, D_MODEL)
        ma, gb, r, lw, k, v, a, b, g = _inproj(h_res, mod, p, tl)
        yb = _scan(r, lw, k, v, a, b, p["rk"], p["gng"], p["gnb"], _tile(seq, SCAN_TILE), SCAN_PAIRS)
        h1 = _mix(h_res, mod, yb, g, gb, ma, p, tl)
        h_res = _ffn(h1, mod, p, tl, F32)
    return h_res.astype(out_dtype)
```

```python
import math

import jax
import jax.numpy as jnp
from jax import lax
from jax.experimental import pallas as pl
from jax.experimental.pallas import tpu as pltpu

D_MODEL = 1024
G_GROUPS = 8
G_WIDTH = 512
CHUNK = 128
R_WIDTH = 512
R_HEAD = 64
R_HEADS = R_WIDTH // R_HEAD
DECAY_LORA = 32
AAA_LORA = 32
GATE_LORA = 96
LORA_COLS = DECAY_LORA + AAA_LORA + GATE_LORA
D_FF = 4 * D_MODEL
DEPTH = 1
ALPHA = (2.0 * DEPTH) ** 0.25
LN_EPS = 1e-5
GN_EPS = 64e-5
DECAY_SCALE = math.exp(-0.5)

LANES = 128
PAIRS = R_WIDTH // LANES
LORA_PAD = 2 * LANES
R_COLS = 3 * R_WIDTH + LORA_PAD
SCAN_CHUNK = 64
STACK = 2 * SCAN_CHUNK
SCAN_TILE = 128
SCAN_PAIRS = 2
VMEM_LIMIT = 56 * 1024 * 1024

F32 = jnp.float32
BF16 = jnp.bfloat16


def _dot(a, b):
    return jnp.dot(a, b, preferred_element_type=F32)


def _dot_nt(a, b):
    return lax.dot_general(a, b, (((1,), (1,)), ((), ())), preferred_element_type=F32)


def _sigmoid(x):
    return 1.0 / (1.0 + jnp.exp(-x))


def _gelu_tanh(x):
    c = 0.7978845608028654
    hx = 0.5 * x
    return hx + hx * jnp.tanh(x * (c + (c * 0.044715) * (x * x)))


def _layer_norm(x, g, b, eps):
    mu = jnp.mean(x, axis=-1, keepdims=True)
    d = x - mu
    var = jnp.mean(d * d, axis=-1, keepdims=True)
    return d * lax.rsqrt(var + eps) * g + b


def _const_spec(shape):
    n = len(shape)
    return pl.BlockSpec(shape, lambda *_: (0,) * n)


def _mod_kernel(c_ref, w_ref, b_ref, o_ref):
    c = c_ref[...]
    c_act = c * _sigmoid(c)
    o_ref[...] = jnp.dot(c_act, w_ref[...], preferred_element_type=F32,
                         precision=lax.Precision.HIGHEST) + b_ref[...]


def _modulation(c, w_ada, b_ada):
    bsz = c.shape[0]
    n = w_ada.shape[1]
    tn = D_MODEL
    return pl.pallas_call(
        _mod_kernel,
        grid=(n // tn,),
        in_specs=[pl.BlockSpec((bsz, D_MODEL), lambda j: (0, 0)),
                  pl.BlockSpec((D_MODEL, tn), lambda j: (0, j)),
                  pl.BlockSpec((1, tn), lambda j: (0, j))],
        out_specs=pl.BlockSpec((bsz, tn), lambda j: (0, j)),
        out_shape=jax.ShapeDtypeStruct((bsz, n), F32),
        name="mod",
    )(c, w_ada, b_ada.reshape(1, n))


def _inproj_kernel(x_ref, mod_ref, wgu_ref, bgu_ref, wgv_ref, bgv_ref, wr_ref, br_ref, wga_ref, bga_ref, wgb_ref, bgb_ref,
                   glnv_ref, blnv_ref, ws_ref, bs_ref, mu_ref, w0_ref, a0_ref, wup_ref,
                   kk_ref, ka_ref, bd_ref, wba_ref,
                   ma_ref, gb_ref, r_ref, lw_ref, k_ref, v_ref, a_ref, b_ref, g_ref,
                   zsh_ref):
    tl = x_ref.shape[1]

    @pl.when(pl.program_id(1) == 0)
    def _():
        zsh_ref[0:8, :] = jnp.zeros((8, R_COLS), F32)

    x = x_ref[0]
    sh1 = mod_ref[0, 0:1, :]
    sc1 = mod_ref[0, 1:2, :]
    h = (x * (1.0 + sc1) + sh1).astype(BF16)

    zr = _dot(h, wr_ref[...]) + br_ref[...]
    zu = _dot(h, wgu_ref[...]) + bgu_ref[...]
    zv = _dot(h, wgv_ref[...]) + bgv_ref[...]

    zsh_ref[8:tl + 8, :] = zr
    prev = zsh_ref[7:tl + 7, :]
    zsh_ref[7:8, :] = zr[tl - 1:tl, :]
    z = zr + (prev - zr) * mu_ref[...]
    r = z[:, 0:R_WIDTH]
    k = z[:, R_WIDTH:2 * R_WIDTH]
    vv = z[:, 2 * R_WIDTH:3 * R_WIDTH]
    xl = z[:, 3 * R_WIDTH:]
    llane = lax.broadcasted_iota(jnp.int32, xl.shape, 1)
    lin = jnp.where(llane < DECAY_LORA, jnp.tanh(xl),
                    jnp.where(llane < DECAY_LORA + AAA_LORA, xl, _sigmoid(xl)))
    zgb = _dot(h, wgb_ref[...]) + bgb_ref[...]
    up = _dot(lin.astype(BF16), wup_ref[...])

    u = _gelu_tanh(zu)
    v = _layer_norm(_gelu_tanh(zv), glnv_ref[...], blnv_ref[...], LN_EPS)

    lw = -DECAY_SCALE * _sigmoid(w0_ref[...] + up[:, 0:R_WIDTH])
    a = _sigmoid(a0_ref[...] + up[:, R_WIDTH:2 * R_WIDTH])
    g_ref[0] = up[:, 2 * R_WIDTH:]
    kk = k * kk_ref[...]
    n2 = _dot((kk * kk).astype(BF16), bd_ref[...])

    lane = lax.broadcasted_iota(jnp.int32, (CHUNK, LANES), 1)
    first_head = lane < R_HEAD
    trow = lax.broadcasted_iota(jnp.int32, (CHUNK, 2 * CHUNK), 0)
    scol = lax.broadcasted_iota(jnp.int32, (CHUNK, 2 * CHUNK), 1) % CHUNK
    causal = trow >= scol
    ws = [jnp.where(causal, ws_ref[q], 0.0).astype(BF16) for q in range(PAIRS)]
    ya_rows = []
    for c in range(tl // CHUNK):
        vc = v[c * CHUNK:(c + 1) * CHUNK, :]
        s_parts = []
        for q in range(PAIRS):
            v2 = vc[:, q * LANES:(q + 1) * LANES]
            vm = jnp.concatenate([jnp.where(first_head, v2, 0.0),
                                  jnp.where(first_head, 0.0, v2)], axis=0).astype(BF16)
            s_parts.append(_dot(ws[q], vm))
        s = jnp.concatenate(s_parts, axis=1) + bs_ref[...]
        ya_rows.append(u[c * CHUNK:(c + 1) * CHUNK, :] * s)
    ya = jnp.concatenate(ya_rows, axis=0).astype(BF16)
    zga = _dot(h, wga_ref[...]) + bga_ref[...]
    pa = _dot(ya, wba_ref[...])

    gb_ref[0] = _sigmoid(zgb).astype(gb_ref.dtype)
    kk = kk * lax.rsqrt(jnp.maximum(n2, 1e-24))
    k2 = k * (1.0 + (a - 1.0) * ka_ref[...])
    nkk = -kk
    kka = kk * a
    for q in range(PAIRS):
        sl = slice(q * LANES, (q + 1) * LANES)
        r_ref[0, q] = r[:, sl]
        lw_ref[0, q] = lw[:, sl]
        k_ref[0, q] = k2[:, sl]
        v_ref[0, q] = vv[:, sl]
        a_ref[0, q] = nkk[:, sl]
        b_ref[0, q] = kka[:, sl]
    ma_ref[0] = _sigmoid(zga) * pa


def _inproj(x, mod, p, tl):
    bsz, seq, _ = x.shape
    grid = (bsz, seq // tl)
    row = lambda b, s: (b, s, 0)
    pair = lambda b, s: (b, 0, s, 0)
    pair_shape = jax.ShapeDtypeStruct((bsz, PAIRS, seq, LANES), F32)
    pair_spec = pl.BlockSpec((1, PAIRS, tl, LANES), pair)
    consts = [p["wgu"], p["bgu"], p["wgv"], p["bgv"], p["wr"], p["br"], p["wga"], p["bga"], p["wgb"], p["bgb"],
              p["glnv"], p["blnv"],
              p["ws"], p["bs"], p["mu"], p["w0"], p["a0"], p["wup"], p["kk"], p["ka"], p["bd"],
              p["wba"]]
    return pl.pallas_call(
        _inproj_kernel,
        grid=grid,
        in_specs=[pl.BlockSpec((1, tl, D_MODEL), row),
                  pl.BlockSpec((1, 6, D_MODEL), lambda b, s: (b, 0, 0))]
                 + [_const_spec(c.shape) for c in consts],
        out_specs=[pl.BlockSpec((1, tl, D_MODEL), row),
                   pl.BlockSpec((1, tl, D_MODEL), row)]
                  + [pair_spec] * 6
                  + [pl.BlockSpec((1, tl, R_WIDTH), row)],
        out_shape=[jax.ShapeDtypeStruct((bsz, seq, D_MODEL), F32),
                   jax.ShapeDtypeStruct((bsz, seq, D_MODEL), BF16)]
                  + [pair_shape] * 6
                  + [jax.ShapeDtypeStruct((bsz, seq, R_WIDTH), F32)],
        scratch_shapes=[pltpu.VMEM((tl + 8, R_COLS), F32)],
        compiler_params=pltpu.CompilerParams(
            dimension_semantics=("arbitrary", "arbitrary"), vmem_limit_bytes=VMEM_LIMIT),
        name="inproj",
    )(x, mod, *consts)


def _scan_kernel(r_ref, lw_ref, k_ref, v_ref, a_ref, b_ref, rk_ref, gng_ref, gnb_ref,
                 y_ref,
                 st_ref, pm_ref, qm_ref, ra_ref, bkt_ref, vm_ref, y0_ref, wm_ref):
    bt, pp, lt, _ = r_ref.shape
    nc = lt // SCAN_CHUNK
    cs = SCAN_CHUNK

    @pl.when(pl.program_id(1) == 0)
    def _():
        st_ref[...] = jnp.zeros(st_ref.shape, F32)

    lane = lax.broadcasted_iota(jnp.int32, (cs, LANES), 1)
    first_head = lane < R_HEAD

    def stack(t):
        return jnp.concatenate([jnp.where(first_head, t, 0.0),
                                jnp.where(first_head, 0.0, t)], axis=0)

    ri = lax.broadcasted_iota(jnp.int32, (cs, cs), 0)
    ci = lax.broadcasted_iota(jnp.int32, (cs, cs), 1)
    tri_ones = jnp.where(ri >= ci, 1.0, 0.0).astype(BF16)
    si = lax.broadcasted_iota(jnp.int32, (cs, LANES), 0)
    sj = lax.broadcasted_iota(jnp.int32, (cs, LANES), 1) % cs
    strict = si > sj
    incl = si >= sj
    eye = jnp.where(si == sj, 1.0, 0.0).astype(F32)

    chains = [(b, q) for q in range(pp) for b in range(bt)]
    ids = range(len(chains))

    def prepare(c, carry):
        rows = pl.ds(pl.multiple_of(c * cs, cs), cs)
        lw = [lw_ref[b, q, rows, :] for b, q in chains]
        cum = []
        for j in ids:
            p1 = lw[j].astype(BF16)
            e1 = lw[j] - p1.astype(F32)
            p2 = e1.astype(BF16)
            p3 = (e1 - p2.astype(F32)).astype(BF16)
            cum.append(_dot(tri_ones, p1) + _dot(tri_ones, p2) + _dot(tri_ones, p3))

        at, rt_b, vm, bkt, wm, aa = [], [], [], [], [], []
        for j, (b, q) in enumerate(chains):
            last = cum[j][cs - 1:cs, :]
            e_pos = jnp.exp(cum[j])
            e_pos_ex = jnp.exp(cum[j] - lw[j])
            e_neg = jnp.exp(-cum[j])
            e_end = jnp.exp(last - cum[j])
            wc = jnp.exp(last)
            r = r_ref[b, q, rows, :]
            k = k_ref[b, q, rows, :]
            v = v_ref[b, q, rows, :]
            a = a_ref[b, q, rows, :]
            bb = b_ref[b, q, rows, :]
            at.append(a * e_pos_ex)
            rt_b.append((r * e_pos).astype(BF16))
            bk = jnp.concatenate([stack(bb * e_neg), stack(k * e_neg)], axis=0).astype(BF16)
            vm.append(stack(v).astype(BF16))
            bkt.append(jnp.concatenate([stack(bb * e_end).T, stack(k * e_end).T], axis=1).astype(BF16))
            wm.append(jnp.broadcast_to(wc, (LANES, LANES)).T)
            aa.append(_dot_nt(jnp.concatenate([at[j].astype(BF16), rt_b[j]], axis=0), bk))

        a_ab = [jnp.where(strict, aa[j][:cs, :LANES], 0.0) for j in ids]
        a_rb = [jnp.where(incl, aa[j][cs:, :LANES], 0.0).astype(BF16) for j in ids]
        akv_y0 = [_dot(jnp.concatenate([jnp.where(strict, aa[j][:cs, LANES:], 0.0),
                                        jnp.where(incl, aa[j][cs:, LANES:], 0.0)],
                                       axis=0).astype(BF16), vm[j]) for j in ids]

        xk = [_dot(a_ab[j].astype(BF16), stack(a_ab[j]).astype(BF16)) for j in ids]
        tk = [eye + a_ab[j] for j in ids]
        n_sq = (cs - 1).bit_length() - 1
        for step in range(n_sq):
            for j in ids:
                x_bd = stack(xk[j]).astype(BF16)
                if step < n_sq - 1:
                    xt = _dot(jnp.concatenate([xk[j], tk[j]], axis=0).astype(BF16), x_bd)
                    xk[j] = xt[:cs]
                    tk[j] = tk[j] + xt[cs:]
                else:
                    tk[j] = tk[j] + _dot(tk[j].astype(BF16), x_bd)

        pq = [_dot(tk[j].astype(BF16),
                   jnp.concatenate([stack(at[j]), stack(akv_y0[j][:cs])], axis=1).astype(BF16))
              for j in ids]
        for j in ids:
            i = j * nc + c
            pm_ref[i] = pq[j][:, :LANES].astype(BF16)
            qm_ref[i] = pq[j][:, LANES:]
            ra_ref[i] = jnp.concatenate([rt_b[j], a_rb[j]], axis=1)
            bkt_ref[i] = bkt[j]
            vm_ref[i] = vm[j]
            y0_ref[i] = akv_y0[j][cs:]
            wm_ref[i] = wm[j]
        return carry

    lax.fori_loop(0, nc, prepare, 0)

    def advance(c, carry):
        rows = pl.ds(pl.multiple_of(c * cs, cs), cs)
        idx = [j * nc + c for j in ids]
        st = [st_ref[j] for j in ids]
        st_b = [st[j].astype(BF16) for j in ids]
        u_b = [stack(_dot(pm_ref[idx[j]], st_b[j]) + qm_ref[idx[j]]).astype(BF16) for j in ids]
        y2, st_new = [], []
        for j in ids:
            i = idx[j]
            y2.append(_dot(ra_ref[i], jnp.concatenate([st_b[j], u_b[j]], axis=0)) + y0_ref[i])
            st_new.append(wm_ref[i] * st[j]
                          + _dot(bkt_ref[i], jnp.concatenate([u_b[j], vm_ref[i]], axis=0)))
        for j, (b, q) in enumerate(chains):
            st_ref[j] = st_new[j]
            y_ref[b, q, rows, :] = y2[j]
        return carry

    lax.fori_loop(0, nc, advance, 0)

    bsi = lax.broadcasted_iota(jnp.int32, (LANES, LANES), 0) // R_HEAD
    bsj = lax.broadcasted_iota(jnp.int32, (LANES, LANES), 1) // R_HEAD
    head_ones = jnp.where(bsi == bsj, 1.0, 0.0).astype(BF16)
    head_mean = jnp.where(bsi == bsj, 1.0 / R_HEAD, 0.0).astype(BF16)
    y = [y_ref[b, q] for b, q in chains]
    d = [y[j] - _dot(y[j].astype(BF16), head_mean) for j in ids]
    var = [_dot((d[j] * d[j]).astype(BF16), head_mean) for j in ids]
    for j, (b, q) in enumerate(chains):
        lanes = slice(q * LANES, (q + 1) * LANES)
        rkr = (r_ref[b, q] * k_ref[b, q] * rk_ref[:, lanes]).astype(BF16)
        bonus = _dot(rkr, head_ones) * v_ref[b, q]
        y_ref[b, q] = (d[j] * lax.rsqrt(var[j] + GN_EPS) * gng_ref[:, lanes] + gnb_ref[:, lanes]
                       + bonus)


def _scan(r, lw, k, v, a, b, rk, gng, gnb, lt, pp):
    bsz, _, seq, _ = r.shape
    nc = lt // SCAN_CHUNK
    n = bsz * pp * nc
    blk = pl.BlockSpec((bsz, pp, lt, LANES), lambda q, s: (0, q, s, 0))
    vec = pl.BlockSpec((1, pp * LANES), lambda q, s: (0, q))
    return pl.pallas_call(
        _scan_kernel,
        grid=(PAIRS // pp, seq // lt),
        in_specs=[blk] * 6 + [vec] * 3,
        out_specs=blk,
        out_shape=jax.ShapeDtypeStruct((bsz, PAIRS, seq, LANES), F32),
        scratch_shapes=[pltpu.VMEM((bsz * pp, LANES, LANES), F32),
                        pltpu.VMEM((n, SCAN_CHUNK, LANES), BF16),
                        pltpu.VMEM((n, SCAN_CHUNK, LANES), F32),
                        pltpu.VMEM((n, SCAN_CHUNK, 2 * LANES), BF16),
                        pltpu.VMEM((n, LANES, 2 * STACK), BF16),
                        pltpu.VMEM((n, STACK, LANES), BF16),
                        pltpu.VMEM((n, SCAN_CHUNK, LANES), F32),
                        pltpu.VMEM((n, LANES, LANES), F32)],
        compiler_params=pltpu.CompilerParams(
            dimension_semantics=("arbitrary", "arbitrary"), vmem_limit_bytes=VMEM_LIMIT),
        name="scan",
    )(r, lw, k, v, a, b, rk, gng, gnb)


def _post_kernel(x_ref, mod_ref, yb_ref, g_ref, gb_ref, ma_ref, wbb_ref, wout_ref, bout_ref,
                 ln1g_ref, ln1b_ref, w1_ref, b1_ref, w2_ref, b2_ref, ln2g_ref, ln2b_ref, o_ref):
    gt1 = mod_ref[0, 2:3, :]
    sh2 = mod_ref[0, 3:4, :]
    sc2 = mod_ref[0, 4:5, :]
    gt2 = mod_ref[0, 5:6, :]
    yb = jnp.concatenate([yb_ref[0, q] for q in range(PAIRS)], axis=1) * g_ref[0]
    pb = _dot(yb.astype(BF16), wbb_ref[...])
    merged = ma_ref[0] + gb_ref[0].astype(F32) * pb
    mix = _dot(merged.astype(BF16), wout_ref[...]) + bout_ref[...]
    h1 = _layer_norm(ALPHA * x_ref[0] + gt1 * mix, ln1g_ref[...], ln1b_ref[...], LN_EPS)
    h = (h1 * (1.0 + sc2) + sh2).astype(BF16)
    t = jnp.maximum(_dot(h, w1_ref[...]) + b1_ref[...], 0.0)
    ff = _dot((t * t).astype(BF16), w2_ref[...]) + b2_ref[...]
    o_ref[0] = _layer_norm(ALPHA * h1 + gt2 * ff, ln2g_ref[...], ln2b_ref[...], LN_EPS).astype(o_ref.dtype)


def _post(x, mod, yb, g, gb, ma, p, tl, out_dtype):
    bsz, seq, _ = x.shape
    row = lambda b, s: (b, s, 0)
    consts = [p["wbb"], p["wout"], p["bout"], p["ln1g"], p["ln1b"],
              p["w1"], p["b1"], p["w2"], p["b2"], p["ln2g"], p["ln2b"]]
    return pl.pallas_call(
        _post_kernel,
        grid=(bsz, seq // tl),
        in_specs=[pl.BlockSpec((1, tl, D_MODEL), row),
                  pl.BlockSpec((1, 6, D_MODEL), lambda b, s: (b, 0, 0)),
                  pl.BlockSpec((1, PAIRS, tl, LANES), lambda b, s: (b, 0, s, 0)),
                  pl.BlockSpec((1, tl, R_WIDTH), row),
                  pl.BlockSpec((1, tl, D_MODEL), row),
                  pl.BlockSpec((1, tl, D_MODEL), row)]
                 + [_const_spec(c.shape) for c in consts],
        out_specs=pl.BlockSpec((1, tl, D_MODEL), row),
        out_shape=jax.ShapeDtypeStruct((bsz, seq, D_MODEL), out_dtype),
        compiler_params=pltpu.CompilerParams(
            dimension_semantics=("arbitrary", "arbitrary"), vmem_limit_bytes=VMEM_LIMIT),
        name="post",
    )(x, mod, yb, g, gb, ma, *consts)


def _prepare_params(l, w_in, b_in, g_ln_v, b_ln_v, w_spatial, b_spatial, mu_shift, w0,
                    w_decay_up, a0, w_aaa_up, w_gate_up, k_k, k_a, r_k, gn_gain, gn_bias,
                    w_branch_a, w_branch_b, w_out, b_out, ln1_g, ln1_b, w_ff1, b_ff1, w_ff2,
                    b_ff2, ln2_g, ln2_b):
    g_end = 2 * G_WIDTH
    rkv_end = g_end + 3 * R_WIDTH
    r_end = rkv_end + LORA_COLS
    pad = LORA_PAD - LORA_COLS
    row2 = lambda t: t.reshape(1, -1)
    wi, bi = w_in[l], b_in[l]
    wr = jnp.concatenate([wi[:, g_end:r_end], jnp.zeros((D_MODEL, pad), F32)], axis=1)
    br = jnp.concatenate([bi[g_end:r_end], jnp.zeros((pad,), F32)])
    mu = jnp.concatenate([mu_shift[l], jnp.zeros((pad,), F32)])
    wup = jnp.zeros((LORA_PAD, 3 * R_WIDTH), F32)
    wup = wup.at[0:DECAY_LORA, 0:R_WIDTH].set(w_decay_up[l])
    wup = wup.at[DECAY_LORA:DECAY_LORA + AAA_LORA, R_WIDTH:2 * R_WIDTH].set(w_aaa_up[l])
    wup = wup.at[DECAY_LORA + AAA_LORA:LORA_COLS, 2 * R_WIDTH:].set(w_gate_up[l])
    ws = w_spatial[l].reshape(PAIRS, 2, CHUNK, CHUNK).transpose(0, 2, 1, 3).reshape(PAIRS, CHUNK, 2 * CHUNK)
    bs = jnp.repeat(b_spatial[l].T, G_WIDTH // G_GROUPS, axis=1)
    hid = jnp.arange(R_WIDTH) // R_HEAD
    bd = (hid[:, None] == hid[None, :]).astype(BF16)
    return dict(
        wgu=wi[:, :G_WIDTH].astype(BF16), bgu=row2(bi[:G_WIDTH]),
        wgv=wi[:, G_WIDTH:g_end].astype(BF16), bgv=row2(bi[G_WIDTH:g_end]),
        wr=wr.astype(BF16), br=row2(br),
        wga=wi[:, r_end:r_end + D_MODEL].astype(BF16), bga=row2(bi[r_end:r_end + D_MODEL]),
        wgb=wi[:, r_end + D_MODEL:].astype(BF16), bgb=row2(bi[r_end + D_MODEL:]),
        glnv=row2(g_ln_v[l]), blnv=row2(b_ln_v[l]), ws=ws, bs=bs, mu=row2(mu),
        w0=row2(w0[l]), a0=row2(a0[l]), wup=wup.astype(BF16), kk=row2(k_k[l]), ka=row2(k_a[l]),
        bd=bd, wba=w_branch_a[l].astype(BF16),
        rk=row2(r_k[l]), gng=row2(gn_gain[l]), gnb=row2(gn_bias[l]),
        wbb=w_branch_b[l].astype(BF16), wout=w_out[l].astype(BF16), bout=row2(b_out[l]),
        ln1g=row2(ln1_g[l]), ln1b=row2(ln1_b[l]),
        w1=w_ff1[l].astype(BF16), b1=row2(b_ff1[l]), w2=w_ff2[l].astype(BF16), b2=row2(b_ff2[l]),
        ln2g=row2(ln2_g[l]), ln2b=row2(ln2_b[l]),
    )


def _tile(seq, want):
    t = min(want, seq)
    while seq % t:
        t //= 2
    return t


def kernel(x, c, w_ada, b_ada, w_in, b_in, g_ln_v, b_ln_v, w_spatial, b_spatial, mu_shift, w0, w_decay_up, a0, w_aaa_up, w_gate_up, k_k, k_a, r_k, gn_gain, gn_bias, w_branch_a, w_branch_b, w_out, b_out, ln1_g, ln1_b, w_ff1, b_ff1, w_ff2, b_ff2, ln2_g, ln2_b):
    bsz, seq, _ = x.shape
    assert seq % CHUNK == 0 and x.shape[2] == D_MODEL
    out_dtype = x.dtype
    h_res = x.astype(F32)
    tl = _tile(seq, 256)
    for l in range(DEPTH):
        p = _prepare_params(l, w_in, b_in, g_ln_v, b_ln_v, w_spatial, b_spatial, mu_shift, w0,
                            w_decay_up, a0, w_aaa_up, w_gate_up, k_k, k_a, r_k, gn_gain, gn_bias,
                            w_branch_a, w_branch_b, w_out, b_out, ln1_g, ln1_b, w_ff1, b_ff1,
                            w_ff2, b_ff2, ln2_g, ln2_b)
        mod = _modulation(c.astype(F32), w_ada[l], b_ada[l]).reshape(bsz, 6, D_MODEL)
        ma, gb, r, lw, k, v, a, b, g = _inproj(h_res, mod, p, tl)
        yb = _scan(r, lw, k, v, a, b, p["rk"], p["gng"], p["gnb"], _tile(seq, SCAN_TILE), SCAN_PAIRS)
        h_res = _post(h_res, mod, yb, g, gb, ma, p, tl, F32)
    return h_res.astype(out_dtype)
```

```python
import math

import jax
import jax.numpy as jnp
from jax import lax
from jax.experimental import pallas as pl
from jax.experimental.pallas import tpu as pltpu

D_MODEL = 1024
G_GROUPS = 8
G_WIDTH = 512
CHUNK = 128
R_WIDTH = 512
R_HEAD = 64
R_HEADS = R_WIDTH // R_HEAD
DECAY_LORA = 32
AAA_LORA = 32
GATE_LORA = 96
LORA_COLS = DECAY_LORA + AAA_LORA + GATE_LORA
D_FF = 4 * D_MODEL
DEPTH = 1
ALPHA = (2.0 * DEPTH) ** 0.25
LN_EPS = 1e-5
GN_EPS = 64e-5
DECAY_SCALE = math.exp(-0.5)

LANES = 128
PAIRS = R_WIDTH // LANES
LORA_PAD = 2 * LANES
R_COLS = 3 * R_WIDTH + LORA_PAD
SCAN_CHUNK = 64
STACK = 2 * SCAN_CHUNK
SCAN_TILE = 128
SCAN_PAIRS = 2
POST_GROUPS = 2
IN_GROUPS = 2
VMEM_LIMIT = 56 * 1024 * 1024

F32 = jnp.float32
BF16 = jnp.bfloat16


def _dot(a, b):
    return jnp.dot(a, b, preferred_element_type=F32)


def _dot_nt(a, b):
    return lax.dot_general(a, b, (((1,), (1,)), ((), ())), preferred_element_type=F32)


def _sigmoid(x):
    return 1.0 / (1.0 + jnp.exp(-x))


def _gelu_tanh(x):
    c = 0.7978845608028654
    hx = 0.5 * x
    return hx + hx * jnp.tanh(x * (c + (c * 0.044715) * (x * x)))


def _layer_norm(x, g, b, eps):
    mu = jnp.mean(x, axis=-1, keepdims=True)
    d = x - mu
    var = jnp.mean(d * d, axis=-1, keepdims=True)
    return d * lax.rsqrt(var + eps) * g + b


def _const_spec(shape):
    n = len(shape)
    return pl.BlockSpec(shape, lambda *_: (0,) * n)


def _mod_kernel(c_ref, w_ref, b_ref, o_ref):
    c = c_ref[...]
    c_act = c * _sigmoid(c)
    o_ref[...] = jnp.dot(c_act, w_ref[...], preferred_element_type=F32,
                         precision=lax.Precision.HIGHEST) + b_ref[...]


def _modulation(c, w_ada, b_ada):
    bsz = c.shape[0]
    n = w_ada.shape[1]
    tn = D_MODEL
    return pl.pallas_call(
        _mod_kernel,
        grid=(n // tn,),
        in_specs=[pl.BlockSpec((bsz, D_MODEL), lambda j: (0, 0)),
                  pl.BlockSpec((D_MODEL, tn), lambda j: (0, j)),
                  pl.BlockSpec((1, tn), lambda j: (0, j))],
        out_specs=pl.BlockSpec((bsz, tn), lambda j: (0, j)),
        out_shape=jax.ShapeDtypeStruct((bsz, n), F32),
        name="mod",
    )(c, w_ada, b_ada.reshape(1, n))


def _inproj_kernel(x_ref, mod_ref, wgu_ref, bgu_ref, wgv_ref, bgv_ref, wr_ref, br_ref, wga_ref, bga_ref, wgb_ref, bgb_ref,
                   glnv_ref, blnv_ref, ws_ref, bs_ref, mu_ref, w0_ref, a0_ref, wup_ref,
                   kk_ref, ka_ref, bd_ref, wba_ref,
                   ma_ref, gb_ref, r_ref, lw_ref, k_ref, v_ref, a_ref, b_ref, g_ref,
                   zsh_ref):
    tl = x_ref.shape[1]

    @pl.when(pl.program_id(1) == 0)
    def _():
        zsh_ref[0:8, :] = jnp.zeros((8, R_COLS), F32)

    sh1 = mod_ref[0, 0:1, :]
    sc1 = mod_ref[0, 1:2, :]
    gsz = tl // IN_GROUPS
    groups = [slice(i * gsz, (i + 1) * gsz) for i in range(IN_GROUPS)]
    gi = range(IN_GROUPS)
    h = [(x_ref[0, rows, :] * (1.0 + sc1) + sh1).astype(BF16) for rows in groups]
    zr = [_dot(h[i], wr_ref[...]) + br_ref[...] for i in gi]
    for i, rows in enumerate(groups):
        zsh_ref[8 + rows.start:8 + rows.stop, :] = zr[i]
    zu = [_dot(h[i], wgu_ref[...]) + bgu_ref[...] for i in gi]
    zv = [_dot(h[i], wgv_ref[...]) + bgv_ref[...] for i in gi]

    z = []
    for i, rows in enumerate(groups):
        prev = zsh_ref[7 + rows.start:7 + rows.stop, :]
        z.append(zr[i] + (prev - zr[i]) * mu_ref[...])
    zsh_ref[7:8, :] = zr[-1][gsz - 1:gsz, :]
    llane = lax.broadcasted_iota(jnp.int32, (gsz, LORA_PAD), 1)
    lin = []
    for i in gi:
        xl = z[i][:, 3 * R_WIDTH:]
        lin.append(jnp.where(llane < DECAY_LORA, jnp.tanh(xl),
                             jnp.where(llane < DECAY_LORA + AAA_LORA, xl, _sigmoid(xl))).astype(BF16))
    zgb = [_dot(h[i], wgb_ref[...]) + bgb_ref[...] for i in gi]
    up = [_dot(lin[i], wup_ref[...]) for i in gi]

    u = [_gelu_tanh(zu[i]) for i in gi]
    v = [_layer_norm(_gelu_tanh(zv[i]), glnv_ref[...], blnv_ref[...], LN_EPS) for i in gi]

    kk, n2 = [], []
    for i, rows in enumerate(groups):
        lw = -DECAY_SCALE * _sigmoid(w0_ref[...] + up[i][:, 0:R_WIDTH])
        g_ref[0, rows, :] = up[i][:, 2 * R_WIDTH:]
        kk.append(z[i][:, R_WIDTH:2 * R_WIDTH] * kk_ref[...])
        n2.append(_dot((kk[i] * kk[i]).astype(BF16), bd_ref[...]))
        for q in range(PAIRS):
            sl = slice(q * LANES, (q + 1) * LANES)
            lw_ref[0, q, rows, :] = lw[:, sl]
            r_ref[0, q, rows, :] = z[i][:, sl]
            v_ref[0, q, rows, :] = z[i][:, 2 * R_WIDTH + q * LANES:2 * R_WIDTH + (q + 1) * LANES]

    lane = lax.broadcasted_iota(jnp.int32, (CHUNK, LANES), 1)
    first_head = lane < R_HEAD
    trow = lax.broadcasted_iota(jnp.int32, (CHUNK, 2 * CHUNK), 0)
    scol = lax.broadcasted_iota(jnp.int32, (CHUNK, 2 * CHUNK), 1) % CHUNK
    causal = trow >= scol
    ws = [jnp.where(causal, ws_ref[q], 0.0).astype(BF16) for q in range(PAIRS)]
    ya = []
    for i in gi:
        ya_rows = []
        for c in range(gsz // CHUNK):
            vc = v[i][c * CHUNK:(c + 1) * CHUNK, :]
            s_parts = []
            for q in range(PAIRS):
                v2 = vc[:, q * LANES:(q + 1) * LANES]
                vm = jnp.concatenate([jnp.where(first_head, v2, 0.0),
                                      jnp.where(first_head, 0.0, v2)], axis=0).astype(BF16)
                s_parts.append(_dot(ws[q], vm))
            s = jnp.concatenate(s_parts, axis=1) + bs_ref[...]
            ya_rows.append(u[i][c * CHUNK:(c + 1) * CHUNK, :] * s)
        ya.append(jnp.concatenate(ya_rows, axis=0).astype(BF16))
    zga = [_dot(h[i], wga_ref[...]) + bga_ref[...] for i in gi]
    pa = [_dot(ya[i], wba_ref[...]) for i in gi]

    for i, rows in enumerate(groups):
        gb_ref[0, rows, :] = _sigmoid(zgb[i]).astype(gb_ref.dtype)
        a = _sigmoid(a0_ref[...] + up[i][:, R_WIDTH:2 * R_WIDTH])
        kkn = kk[i] * lax.rsqrt(jnp.maximum(n2[i], 1e-24))
        k2 = z[i][:, R_WIDTH:2 * R_WIDTH] * (1.0 + (a - 1.0) * ka_ref[...])
        nkk = -kkn
        kka = kkn * a
        for q in range(PAIRS):
            sl = slice(q * LANES, (q + 1) * LANES)
            k_ref[0, q, rows, :] = k2[:, sl]
            a_ref[0, q, rows, :] = nkk[:, sl]
            b_ref[0, q, rows, :] = kka[:, sl]
        ma_ref[0, rows, :] = _sigmoid(zga[i]) * pa[i]


def _inproj(x, mod, p, tl):
    bsz, seq, _ = x.shape
    grid = (bsz, seq // tl)
    row = lambda b, s: (b, s, 0)
    pair = lambda b, s: (b, 0, s, 0)
    pair_shape = jax.ShapeDtypeStruct((bsz, PAIRS, seq, LANES), F32)
    pair_spec = pl.BlockSpec((1, PAIRS, tl, LANES), pair)
    consts = [p["wgu"], p["bgu"], p["wgv"], p["bgv"], p["wr"], p["br"], p["wga"], p["bga"], p["wgb"], p["bgb"],
              p["glnv"], p["blnv"],
              p["ws"], p["bs"], p["mu"], p["w0"], p["a0"], p["wup"], p["kk"], p["ka"], p["bd"],
              p["wba"]]
    return pl.pallas_call(
        _inproj_kernel,
        grid=grid,
        in_specs=[pl.BlockSpec((1, tl, D_MODEL), row),
                  pl.BlockSpec((1, 6, D_MODEL), lambda b, s: (b, 0, 0))]
                 + [_const_spec(c.shape) for c in consts],
        out_specs=[pl.BlockSpec((1, tl, D_MODEL), row),
                   pl.BlockSpec((1, tl, D_MODEL), row)]
                  + [pair_spec] * 6
                  + [pl.BlockSpec((1, tl, R_WIDTH), row)],
        out_shape=[jax.ShapeDtypeStruct((bsz, seq, D_MODEL), F32),
                   jax.ShapeDtypeStruct((bsz, seq, D_MODEL), BF16)]
                  + [pair_shape] * 6
                  + [jax.ShapeDtypeStruct((bsz, seq, R_WIDTH), F32)],
        scratch_shapes=[pltpu.VMEM((tl + 8, R_COLS), F32)],
        compiler_params=pltpu.CompilerParams(
            dimension_semantics=("arbitrary", "arbitrary"), vmem_limit_bytes=VMEM_LIMIT),
        name="inproj",
    )(x, mod, *consts)


def _scan_kernel(r_ref, lw_ref, k_ref, v_ref, a_ref, b_ref, rk_ref, gng_ref, gnb_ref,
                 y_ref,
                 st_ref, pm_ref, qm_ref, ra_ref, bkt_ref, vm_ref, y0_ref, wm_ref):
    bt, pp, lt, _ = r_ref.shape
    nc = lt // SCAN_CHUNK
    cs = SCAN_CHUNK

    @pl.when(pl.program_id(1) == 0)
    def _():
        st_ref[...] = jnp.zeros(st_ref.shape, F32)

    lane = lax.broadcasted_iota(jnp.int32, (cs, LANES), 1)
    first_head = lane < R_HEAD

    def stack(t):
        return jnp.concatenate([jnp.where(first_head, t, 0.0),
                                jnp.where(first_head, 0.0, t)], axis=0)

    ri = lax.broadcasted_iota(jnp.int32, (cs, cs), 0)
    ci = lax.broadcasted_iota(jnp.int32, (cs, cs), 1)
    tri_ones = jnp.where(ri >= ci, 1.0, 0.0).astype(BF16)
    si = lax.broadcasted_iota(jnp.int32, (cs, LANES), 0)
    sj = lax.broadcasted_iota(jnp.int32, (cs, LANES), 1) % cs
    strict = si > sj
    incl = si >= sj
    eye = jnp.where(si == sj, 1.0, 0.0).astype(F32)

    chains = [(b, q) for q in range(pp) for b in range(bt)]
    ids = range(len(chains))

    def prepare(c, carry):
        rows = pl.ds(pl.multiple_of(c * cs, cs), cs)
        lw = [lw_ref[b, q, rows, :] for b, q in chains]
        cum = []
        for j in ids:
            p1 = lw[j].astype(BF16)
            e1 = lw[j] - p1.astype(F32)
            p2 = e1.astype(BF16)
            p3 = (e1 - p2.astype(F32)).astype(BF16)
            cum.append(_dot(tri_ones, p1) + _dot(tri_ones, p2) + _dot(tri_ones, p3))

        at, rt_b, vm, bkt, wm, aa = [], [], [], [], [], []
        for j, (b, q) in enumerate(chains):
            last = cum[j][cs - 1:cs, :]
            e_pos = jnp.exp(cum[j])
            e_pos_ex = jnp.exp(cum[j] - lw[j])
            e_neg = jnp.exp(-cum[j])
            e_end = jnp.exp(last - cum[j])
            wc = jnp.exp(last)
            r = r_ref[b, q, rows, :]
            k = k_ref[b, q, rows, :]
            v = v_ref[b, q, rows, :]
            a = a_ref[b, q, rows, :]
            bb = b_ref[b, q, rows, :]
            at.append(a * e_pos_ex)
            rt_b.append((r * e_pos).astype(BF16))
            bk = jnp.concatenate([stack(bb * e_neg), stack(k * e_neg)], axis=0).astype(BF16)
            vm.append(stack(v).astype(BF16))
            bkt.append(jnp.concatenate([stack(bb * e_end).T, stack(k * e_end).T], axis=1).astype(BF16))
            wm.append(jnp.broadcast_to(wc, (LANES, LANES)).T)
            aa.append(_dot_nt(jnp.concatenate([at[j].astype(BF16), rt_b[j]], axis=0), bk))

        a_ab = [jnp.where(strict, aa[j][:cs, :LANES], 0.0) for j in ids]
        a_rb = [jnp.where(incl, aa[j][cs:, :LANES], 0.0).astype(BF16) for j in ids]
        akv_y0 = [_dot(jnp.concatenate([jnp.where(strict, aa[j][:cs, LANES:], 0.0),
                                        jnp.where(incl, aa[j][cs:, LANES:], 0.0)],
                                       axis=0).astype(BF16), vm[j]) for j in ids]

        xk = [_dot(a_ab[j].astype(BF16), stack(a_ab[j]).astype(BF16)) for j in ids]
        tk = [eye + a_ab[j] for j in ids]
        n_sq = (cs - 1).bit_length() - 1
        for step in range(n_sq):
            for j in ids:
                x_bd = stack(xk[j]).astype(BF16)
                if step < n_sq - 1:
                    xt = _dot(jnp.concatenate([xk[j], tk[j]], axis=0).astype(BF16), x_bd)
                    xk[j] = xt[:cs]
                    tk[j] = tk[j] + xt[cs:]
                else:
                    tk[j] = tk[j] + _dot(tk[j].astype(BF16), x_bd)

        pq = [_dot(tk[j].astype(BF16),
                   jnp.concatenate([stack(at[j]), stack(akv_y0[j][:cs])], axis=1).astype(BF16))
              for j in ids]
        for j in ids:
            i = j * nc + c
            pm_ref[i] = pq[j][:, :LANES].astype(BF16)
            qm_ref[i] = pq[j][:, LANES:]
            ra_ref[i] = jnp.concatenate([rt_b[j], a_rb[j]], axis=1)
            bkt_ref[i] = bkt[j]
            vm_ref[i] = vm[j]
            y0_ref[i] = akv_y0[j][cs:]
            wm_ref[i] = wm[j]
        return carry

    lax.fori_loop(0, nc, prepare, 0)

    def advance(c, carry):
        rows = pl.ds(pl.multiple_of(c * cs, cs), cs)
        idx = [j * nc + c for j in ids]
        st = [st_ref[j] for j in ids]
        st_b = [st[j].astype(BF16) for j in ids]
        u_b = [stack(_dot(pm_ref[idx[j]], st_b[j]) + qm_ref[idx[j]]).astype(BF16) for j in ids]
        y2, st_new = [], []
        for j in ids:
            i = idx[j]
            y2.append(_dot(ra_ref[i], jnp.concatenate([st_b[j], u_b[j]], axis=0)) + y0_ref[i])
            st_new.append(wm_ref[i] * st[j]
                          + _dot(bkt_ref[i], jnp.concatenate([u_b[j], vm_ref[i]], axis=0)))
        for j, (b, q) in enumerate(chains):
            st_ref[j] = st_new[j]
            y_ref[b, q, rows, :] = y2[j]
        return carry

    lax.fori_loop(0, nc, advance, 0)

    bsi = lax.broadcasted_iota(jnp.int32, (LANES, LANES), 0) // R_HEAD
    bsj = lax.broadcasted_iota(jnp.int32, (LANES, LANES), 1) // R_HEAD
    head_ones = jnp.where(bsi == bsj, 1.0, 0.0).astype(BF16)
    head_mean = jnp.where(bsi == bsj, 1.0 / R_HEAD, 0.0).astype(BF16)
    y = [y_ref[b, q] for b, q in chains]
    d = [y[j] - _dot(y[j].astype(BF16), head_mean) for j in ids]
    var = [_dot((d[j] * d[j]).astype(BF16), head_mean) for j in ids]
    for j, (b, q) in enumerate(chains):
        lanes = slice(q * LANES, (q + 1) * LANES)
        rkr = (r_ref[b, q] * k_ref[b, q] * rk_ref[:, lanes]).astype(BF16)
        bonus = _dot(rkr, head_ones) * v_ref[b, q]
        y_ref[b, q] = (d[j] * lax.rsqrt(var[j] + GN_EPS) * gng_ref[:, lanes] + gnb_ref[:, lanes]
                       + bonus)


def _scan(r, lw, k, v, a, b, rk, gng, gnb, lt, pp):
    bsz, _, seq, _ = r.shape
    nc = lt // SCAN_CHUNK
    n = bsz * pp * nc
    blk = pl.BlockSpec((bsz, pp, lt, LANES), lambda q, s: (0, q, s, 0))
    vec = pl.BlockSpec((1, pp * LANES), lambda q, s: (0, q))
    return pl.pallas_call(
        _scan_kernel,
        grid=(PAIRS // pp, seq // lt),
        in_specs=[blk] * 6 + [vec] * 3,
        out_specs=blk,
        out_shape=jax.ShapeDtypeStruct((bsz, PAIRS, seq, LANES), F32),
        scratch_shapes=[pltpu.VMEM((bsz * pp, LANES, LANES), F32),
                        pltpu.VMEM((n, SCAN_CHUNK, LANES), BF16),
                        pltpu.VMEM((n, SCAN_CHUNK, LANES), F32),
                        pltpu.VMEM((n, SCAN_CHUNK, 2 * LANES), BF16),
                        pltpu.VMEM((n, LANES, 2 * STACK), BF16),
                        pltpu.VMEM((n, STACK, LANES), BF16),
                        pltpu.VMEM((n, SCAN_CHUNK, LANES), F32),
                        pltpu.VMEM((n, LANES, LANES), F32)],
        compiler_params=pltpu.CompilerParams(
            dimension_semantics=("arbitrary", "arbitrary"), vmem_limit_bytes=VMEM_LIMIT),
        name="scan",
    )(r, lw, k, v, a, b, rk, gng, gnb)


def _post_kernel(x_ref, mod_ref, yb_ref, g_ref, gb_ref, ma_ref, wbb_ref, wout_ref, bout_ref,
                 ln1g_ref, ln1b_ref, w1_ref, b1_ref, w2_ref, b2_ref, ln2g_ref, ln2b_ref, o_ref):
    tl = x_ref.shape[1]
    gt1 = mod_ref[0, 2:3, :]
    sh2 = mod_ref[0, 3:4, :]
    sc2 = mod_ref[0, 4:5, :]
    gt2 = mod_ref[0, 5:6, :]
    groups = [slice(i * tl // POST_GROUPS, (i + 1) * tl // POST_GROUPS) for i in range(POST_GROUPS)]
    pb = []
    for rows in groups:
        yb = jnp.concatenate([yb_ref[0, q, rows, :] for q in range(PAIRS)], axis=1) * g_ref[0, rows, :]
        pb.append(_dot(yb.astype(BF16), wbb_ref[...]))
    h1 = []
    for i, rows in enumerate(groups):
        merged = ma_ref[0, rows, :] + gb_ref[0, rows, :].astype(F32) * pb[i]
        mix = _dot(merged.astype(BF16), wout_ref[...]) + bout_ref[...]
        h1.append(_layer_norm(ALPHA * x_ref[0, rows, :] + gt1 * mix,
                              ln1g_ref[...], ln1b_ref[...], LN_EPS))
    t = []
    for i in range(POST_GROUPS):
        h = (h1[i] * (1.0 + sc2) + sh2).astype(BF16)
        t.append(jnp.maximum(_dot(h, w1_ref[...]) + b1_ref[...], 0.0))
    for i, rows in enumerate(groups):
        ff = _dot((t[i] * t[i]).astype(BF16), w2_ref[...]) + b2_ref[...]
        o_ref[0, rows, :] = _layer_norm(ALPHA * h1[i] + gt2 * ff,
                                        ln2g_ref[...], ln2b_ref[...], LN_EPS).astype(o_ref.dtype)


def _post(x, mod, yb, g, gb, ma, p, tl, out_dtype):
    bsz, seq, _ = x.shape
    row = lambda b, s: (b, s, 0)
    consts = [p["wbb"], p["wout"], p["bout"], p["ln1g"], p["ln1b"],
              p["w1"], p["b1"], p["w2"], p["b2"], p["ln2g"], p["ln2b"]]
    return pl.pallas_call(
        _post_kernel,
        grid=(bsz, seq // tl),
        in_specs=[pl.BlockSpec((1, tl, D_MODEL), row),
                  pl.BlockSpec((1, 6, D_MODEL), lambda b, s: (b, 0, 0)),
                  pl.BlockSpec((1, PAIRS, tl, LANES), lambda b, s: (b, 0, s, 0)),
                  pl.BlockSpec((1, tl, R_WIDTH), row),
                  pl.BlockSpec((1, tl, D_MODEL), row),
                  pl.BlockSpec((1, tl, D_MODEL), row)]
                 + [_const_spec(c.shape) for c in consts],
        out_specs=pl.BlockSpec((1, tl, D_MODEL), row),
        out_shape=jax.ShapeDtypeStruct((bsz, seq, D_MODEL), out_dtype),
        compiler_params=pltpu.CompilerParams(
            dimension_semantics=("arbitrary", "arbitrary"), vmem_limit_bytes=VMEM_LIMIT),
        name="post",
    )(x, mod, yb, g, gb, ma, *consts)


def _prepare_params(l, w_in, b_in, g_ln_v, b_ln_v, w_spatial, b_spatial, mu_shift, w0,
                    w_decay_up, a0, w_aaa_up, w_gate_up, k_k, k_a, r_k, gn_gain, gn_bias,
                    w_branch_a, w_branch_b, w_out, b_out, ln1_g, ln1_b, w_ff1, b_ff1, w_ff2,
                    b_ff2, ln2_g, ln2_b):
    g_end = 2 * G_WIDTH
    rkv_end = g_end + 3 * R_WIDTH
    r_end = rkv_end + LORA_COLS
    pad = LORA_PAD - LORA_COLS
    row2 = lambda t: t.reshape(1, -1)
    wi, bi = w_in[l], b_in[l]
    wr = jnp.concatenate([wi[:, g_end:r_end], jnp.zeros((D_MODEL, pad), F32)], axis=1)
    br = jnp.concatenate([bi[g_end:r_end], jnp.zeros((pad,), F32)])
    mu = jnp.concatenate([mu_shift[l], jnp.zeros((pad,), F32)])
    wup = jnp.zeros((LORA_PAD, 3 * R_WIDTH), F32)
    wup = wup.at[0:DECAY_LORA, 0:R_WIDTH].set(w_decay_up[l])
    wup = wup.at[DECAY_LORA:DECAY_LORA + AAA_LORA, R_WIDTH:2 * R_WIDTH].set(w_aaa_up[l])
    wup = wup.at[DECAY_LORA + AAA_LORA:LORA_COLS, 2 * R_WIDTH:].set(w_gate_up[l])
    ws = w_spatial[l].reshape(PAIRS, 2, CHUNK, CHUNK).transpose(0, 2, 1, 3).reshape(PAIRS, CHUNK, 2 * CHUNK)
    bs = jnp.repeat(b_spatial[l].T, G_WIDTH // G_GROUPS, axis=1)
    hid = jnp.arange(R_WIDTH) // R_HEAD
    bd = (hid[:, None] == hid[None, :]).astype(BF16)
    return dict(
        wgu=wi[:, :G_WIDTH].astype(BF16), bgu=row2(bi[:G_WIDTH]),
        wgv=wi[:, G_WIDTH:g_end].astype(BF16), bgv=row2(bi[G_WIDTH:g_end]),
        wr=wr.astype(BF16), br=row2(br),
        wga=wi[:, r_end:r_end + D_MODEL].astype(BF16), bga=row2(bi[r_end:r_end + D_MODEL]),
        wgb=wi[:, r_end + D_MODEL:].astype(BF16), bgb=row2(bi[r_end + D_MODEL:]),
        glnv=row2(g_ln_v[l]), blnv=row2(b_ln_v[l]), ws=ws, bs=bs, mu=row2(mu),
        w0=row2(w0[l]), a0=row2(a0[l]), wup=wup.astype(BF16), kk=row2(k_k[l]), ka=row2(k_a[l]),
        bd=bd, wba=w_branch_a[l].astype(BF16),
        rk=row2(r_k[l]), gng=row2(gn_gain[l]), gnb=row2(gn_bias[l]),
        wbb=w_branch_b[l].astype(BF16), wout=w_out[l].astype(BF16), bout=row2(b_out[l]),
        ln1g=row2(ln1_g[l]), ln1b=row2(ln1_b[l]),
        w1=w_ff1[l].astype(BF16), b1=row2(b_ff1[l]), w2=w_ff2[l].astype(BF16), b2=row2(b_ff2[l]),
        ln2g=row2(ln2_g[l]), ln2b=row2(ln2_b[l]),
    )


def _tile(seq, want):
    t = min(want, seq)
    while seq % t:
        t //= 2
    return t


def kernel(x, c, w_ada, b_ada, w_in, b_in, g_ln_v, b_ln_v, w_spatial, b_spatial, mu_shift, w0, w_decay_up, a0, w_aaa_up, w_gate_up, k_k, k_a, r_k, gn_gain, gn_bias, w_branch_a, w_branch_b, w_out, b_out, ln1_g, ln1_b, w_ff1, b_ff1, w_ff2, b_ff2, ln2_g, ln2_b):
    bsz, seq, _ = x.shape
    assert seq % CHUNK == 0 and x.shape[2] == D_MODEL
    out_dtype = x.dtype
    h_res = x.astype(F32)
    tl = _tile(seq, 512)
    for l in range(DEPTH):
        p = _prepare_params(l, w_in, b_in, g_ln_v, b_ln_v, w_spatial, b_spatial, mu_shift, w0,
                            w_decay_up, a0, w_aaa_up, w_gate_up, k_k, k_a, r_k, gn_gain, gn_bias,
                            w_branch_a, w_branch_b, w_out, b_out, ln1_g, ln1_b, w_ff1, b_ff1,
                            w_ff2, b_ff2, ln2_g, ln2_b)
        mod = _modulation(c.astype(F32), w_ada[l], b_ada[l]).reshape(bsz, 6, D_MODEL)
        ma, gb, r, lw, k, v, a, b, g = _inproj(h_res, mod, p, tl)
        yb = _scan(r, lw, k, v, a, b, p["rk"], p["gng"], p["gnb"], _tile(seq, SCAN_TILE), SCAN_PAIRS)
        h_res = _post(h_res, mod, yb, g, gb, ma, p, tl, F32)
    return h_res.astype(out_dtype)
```

```python
import math

import jax
import jax.numpy as jnp
from jax import lax
from jax.experimental import pallas as pl
from jax.experimental.pallas import tpu as pltpu

D_MODEL = 1024
G_GROUPS = 8
G_WIDTH = 512
CHUNK = 128
R_WIDTH = 512
R_HEAD = 64
R_HEADS = R_WIDTH // R_HEAD
DECAY_LORA = 32
AAA_LORA = 32
GATE_LORA = 96
LORA_COLS = DECAY_LORA + AAA_LORA + GATE_LORA
D_FF = 4 * D_MODEL
DEPTH = 1
ALPHA = (2.0 * DEPTH) ** 0.25
LN_EPS = 1e-5
GN_EPS = 64e-5
DECAY_SCALE = math.exp(-0.5)

LANES = 128
PAIRS = R_WIDTH // LANES
LORA_PAD = 2 * LANES
R_COLS = 3 * R_WIDTH + LORA_PAD
SCAN_CHUNK = 64
STACK = 2 * SCAN_CHUNK
POST_GROUPS = 2
IN_GROUPS = 2
VMEM_LIMIT = 56 * 1024 * 1024

F32 = jnp.float32
BF16 = jnp.bfloat16


def _dot(a, b):
    return jnp.dot(a, b, preferred_element_type=F32)


def _dot_nt(a, b):
    return lax.dot_general(a, b, (((1,), (1,)), ((), ())), preferred_element_type=F32)


def _sigmoid(x):
    return 1.0 / (1.0 + jnp.exp(-x))


def _gelu_tanh(x):
    c = 0.7978845608028654
    hx = 0.5 * x
    return hx + hx * jnp.tanh(x * (c + (c * 0.044715) * (x * x)))


def _layer_norm(x, g, b, eps):
    mu = jnp.mean(x, axis=-1, keepdims=True)
    d = x - mu
    var = jnp.mean(d * d, axis=-1, keepdims=True)
    return d * lax.rsqrt(var + eps) * g + b


def _const_spec(shape):
    n = len(shape)
    return pl.BlockSpec(shape, lambda *_: (0,) * n)


def _mod_kernel(c_ref, w_ref, b_ref, o_ref):
    c = c_ref[...]
    c_act = c * _sigmoid(c)
    o_ref[...] = jnp.dot(c_act, w_ref[...], preferred_element_type=F32,
                         precision=lax.Precision.HIGHEST) + b_ref[...]


def _modulation(c, w_ada, b_ada):
    bsz = c.shape[0]
    n = w_ada.shape[1]
    tn = D_MODEL
    return pl.pallas_call(
        _mod_kernel,
        grid=(n // tn,),
        in_specs=[pl.BlockSpec((bsz, D_MODEL), lambda j: (0, 0)),
                  pl.BlockSpec((D_MODEL, tn), lambda j: (0, j)),
                  pl.BlockSpec((1, tn), lambda j: (0, j))],
        out_specs=pl.BlockSpec((bsz, tn), lambda j: (0, j)),
        out_shape=jax.ShapeDtypeStruct((bsz, n), F32),
        name="mod",
    )(c, w_ada, b_ada.reshape(1, n))


def _inproj_kernel(x_ref, mod_ref, wgu_ref, bgu_ref, wgv_ref, bgv_ref, wr_ref, br_ref, wga_ref, bga_ref, wgb_ref, bgb_ref,
                   glnv_ref, blnv_ref, ws_ref, bs_ref, mu_ref, w0_ref, a0_ref, wup_ref,
                   kk_ref, ka_ref, bd_ref, wba_ref,
                   ma_ref, gb_ref, r_ref, lw_ref, k_ref, v_ref, a_ref, b_ref, g_ref,
                   zsh_ref):
    tl = x_ref.shape[1]

    @pl.when(pl.program_id(1) == 0)
    def _():
        zsh_ref[0:8, :] = jnp.zeros((8, R_COLS), F32)

    sh1 = mod_ref[0, 0:1, :]
    sc1 = mod_ref[0, 1:2, :]
    gsz = tl // IN_GROUPS
    groups = [slice(i * gsz, (i + 1) * gsz) for i in range(IN_GROUPS)]
    gi = range(IN_GROUPS)
    h = [(x_ref[0, rows, :] * (1.0 + sc1) + sh1).astype(BF16) for rows in groups]
    zr = [_dot(h[i], wr_ref[...]) + br_ref[...] for i in gi]
    for i, rows in enumerate(groups):
        zsh_ref[8 + rows.start:8 + rows.stop, :] = zr[i]
    zu = [_dot(h[i], wgu_ref[...]) + bgu_ref[...] for i in gi]
    zv = [_dot(h[i], wgv_ref[...]) + bgv_ref[...] for i in gi]

    z = []
    for i, rows in enumerate(groups):
        prev = zsh_ref[7 + rows.start:7 + rows.stop, :]
        z.append(zr[i] + (prev - zr[i]) * mu_ref[...])
    zsh_ref[7:8, :] = zr[-1][gsz - 1:gsz, :]
    llane = lax.broadcasted_iota(jnp.int32, (gsz, LORA_PAD), 1)
    lin = []
    for i in gi:
        xl = z[i][:, 3 * R_WIDTH:]
        lin.append(jnp.where(llane < DECAY_LORA, jnp.tanh(xl),
                             jnp.where(llane < DECAY_LORA + AAA_LORA, xl, _sigmoid(xl))).astype(BF16))
    zgb = [_dot(h[i], wgb_ref[...]) + bgb_ref[...] for i in gi]
    up = [_dot(lin[i], wup_ref[...]) for i in gi]

    u = [_gelu_tanh(zu[i]) for i in gi]
    v = [_layer_norm(_gelu_tanh(zv[i]), glnv_ref[...], blnv_ref[...], LN_EPS) for i in gi]

    kk, n2 = [], []
    for i, rows in enumerate(groups):
        lw = -DECAY_SCALE * _sigmoid(w0_ref[...] + up[i][:, 0:R_WIDTH])
        g_ref[0, rows, :] = up[i][:, 2 * R_WIDTH:]
        kk.append(z[i][:, R_WIDTH:2 * R_WIDTH] * kk_ref[...])
        n2.append(_dot((kk[i] * kk[i]).astype(BF16), bd_ref[...]))
        for q in range(PAIRS):
            sl = slice(q * LANES, (q + 1) * LANES)
            lw_ref[0, q, rows, :] = lw[:, sl]
            r_ref[0, q, rows, :] = z[i][:, sl]
            v_ref[0, q, rows, :] = z[i][:, 2 * R_WIDTH + q * LANES:2 * R_WIDTH + (q + 1) * LANES]

    lane = lax.broadcasted_iota(jnp.int32, (CHUNK, LANES), 1)
    first_head = lane < R_HEAD
    trow = lax.broadcasted_iota(jnp.int32, (CHUNK, 2 * CHUNK), 0)
    scol = lax.broadcasted_iota(jnp.int32, (CHUNK, 2 * CHUNK), 1) % CHUNK
    causal = trow >= scol
    ws = [jnp.where(causal, ws_ref[q], 0.0).astype(BF16) for q in range(PAIRS)]
    ya = []
    for i in gi:
        ya_rows = []
        for c in range(gsz // CHUNK):
            vc = v[i][c * CHUNK:(c + 1) * CHUNK, :]
            s_parts = []
            for q in range(PAIRS):
                v2 = vc[:, q * LANES:(q + 1) * LANES]
                vm = jnp.concatenate([jnp.where(first_head, v2, 0.0),
                                      jnp.where(first_head, 0.0, v2)], axis=0).astype(BF16)
                s_parts.append(_dot(ws[q], vm))
            s = jnp.concatenate(s_parts, axis=1) + bs_ref[...]
            ya_rows.append(u[i][c * CHUNK:(c + 1) * CHUNK, :] * s)
        ya.append(jnp.concatenate(ya_rows, axis=0).astype(BF16))
    zga = [_dot(h[i], wga_ref[...]) + bga_ref[...] for i in gi]
    pa = [_dot(ya[i], wba_ref[...]) for i in gi]

    for i, rows in enumerate(groups):
        gb_ref[0, rows, :] = _sigmoid(zgb[i]).astype(gb_ref.dtype)
        a = _sigmoid(a0_ref[...] + up[i][:, R_WIDTH:2 * R_WIDTH])
        kkn = kk[i] * lax.rsqrt(jnp.maximum(n2[i], 1e-24))
        k2 = z[i][:, R_WIDTH:2 * R_WIDTH] * (1.0 + (a - 1.0) * ka_ref[...])
        nkk = -kkn
        kka = kkn * a
        for q in range(PAIRS):
            sl = slice(q * LANES, (q + 1) * LANES)
            k_ref[0, q, rows, :] = k2[:, sl]
            a_ref[0, q, rows, :] = nkk[:, sl]
            b_ref[0, q, rows, :] = kka[:, sl]
        ma_ref[0, rows, :] = _sigmoid(zga[i]) * pa[i]


def _inproj(x, mod, p, tl):
    bsz, seq, _ = x.shape
    grid = (bsz, seq // tl)
    row = lambda b, s: (b, s, 0)
    pair = lambda b, s: (b, 0, s, 0)
    pair_shape = jax.ShapeDtypeStruct((bsz, PAIRS, seq, LANES), F32)
    pair_spec = pl.BlockSpec((1, PAIRS, tl, LANES), pair)
    consts = [p["wgu"], p["bgu"], p["wgv"], p["bgv"], p["wr"], p["br"], p["wga"], p["bga"], p["wgb"], p["bgb"],
              p["glnv"], p["blnv"],
              p["ws"], p["bs"], p["mu"], p["w0"], p["a0"], p["wup"], p["kk"], p["ka"], p["bd"],
              p["wba"]]
    return pl.pallas_call(
        _inproj_kernel,
        grid=grid,
        in_specs=[pl.BlockSpec((1, tl, D_MODEL), row),
                  pl.BlockSpec((1, 6, D_MODEL), lambda b, s: (b, 0, 0))]
                 + [_const_spec(c.shape) for c in consts],
        out_specs=[pl.BlockSpec((1, tl, D_MODEL), row),
                   pl.BlockSpec((1, tl, D_MODEL), row)]
                  + [pair_spec] * 6
                  + [pl.BlockSpec((1, tl, R_WIDTH), row)],
        out_shape=[jax.ShapeDtypeStruct((bsz, seq, D_MODEL), F32),
                   jax.ShapeDtypeStruct((bsz, seq, D_MODEL), BF16)]
                  + [pair_shape] * 6
                  + [jax.ShapeDtypeStruct((bsz, seq, R_WIDTH), F32)],
        scratch_shapes=[pltpu.VMEM((tl + 8, R_COLS), F32)],
        compiler_params=pltpu.CompilerParams(
            dimension_semantics=("arbitrary", "arbitrary"), vmem_limit_bytes=VMEM_LIMIT),
        name="inproj",
    )(x, mod, *consts)


def _scan_kernel(r_ref, lw_ref, k_ref, v_ref, a_ref, b_ref, rk_ref, gng_ref, gnb_ref,
                 y_ref, st_ref):
    bt, pp, cs, _ = r_ref.shape

    @pl.when(pl.program_id(0) == 0)
    def _():
        st_ref[...] = jnp.zeros(st_ref.shape, F32)

    lane = lax.broadcasted_iota(jnp.int32, (cs, LANES), 1)
    first_head = lane < R_HEAD

    def stack(t):
        return jnp.concatenate([jnp.where(first_head, t, 0.0),
                                jnp.where(first_head, 0.0, t)], axis=0)

    def fold(t):
        return jnp.where(first_head, t[:cs], t[cs:])

    ri = lax.broadcasted_iota(jnp.int32, (cs, cs), 0)
    ci = lax.broadcasted_iota(jnp.int32, (cs, cs), 1)
    tri_ones = jnp.where(ri >= ci, 1.0, 0.0).astype(BF16)
    si = lax.broadcasted_iota(jnp.int32, (cs, LANES), 0)
    sj = lane % cs
    strict = si > sj
    incl = si >= sj
    eye = jnp.where(si == sj, 1.0, 0.0).astype(F32)
    bsi = lax.broadcasted_iota(jnp.int32, (LANES, LANES), 0) // R_HEAD
    bsj = lax.broadcasted_iota(jnp.int32, (LANES, LANES), 1) // R_HEAD
    head_ones = jnp.where(bsi == bsj, 1.0, 0.0).astype(BF16)
    head_mean = jnp.where(bsi == bsj, 1.0 / R_HEAD, 0.0).astype(BF16)

    chains = [(b, q) for q in range(pp) for b in range(bt)]
    ids = range(len(chains))

    lw = [lw_ref[b, q] for b, q in chains]
    cum = []
    for j in ids:
        p1 = lw[j].astype(BF16)
        e1 = lw[j] - p1.astype(F32)
        p2 = e1.astype(BF16)
        p3 = (e1 - p2.astype(F32)).astype(BF16)
        cum.append(_dot(tri_ones, p1) + _dot(tri_ones, p2) + _dot(tri_ones, p3))

    at, rt_b, vm, bkt, wm, aa, bonus = [], [], [], [], [], [], []
    for j, (b, q) in enumerate(chains):
        last = cum[j][cs - 1:cs, :]
        e_pos = jnp.exp(cum[j])
        e_pos_ex = jnp.exp(cum[j] - lw[j])
        e_neg = jnp.exp(-cum[j])
        e_end = jnp.exp(last - cum[j])
        wc = jnp.exp(last)
        r = r_ref[b, q]
        k = k_ref[b, q]
        v = v_ref[b, q]
        a = a_ref[b, q]
        bb = b_ref[b, q]
        at.append(a * e_pos_ex)
        rt_b.append((r * e_pos).astype(BF16))
        bk = jnp.concatenate([stack(bb * e_neg), stack(k * e_neg)], axis=0).astype(BF16)
        vm.append(stack(v).astype(BF16))
        bkt.append(jnp.concatenate([fold(stack(bb * e_end).T), fold(stack(k * e_end).T)],
                                   axis=1).astype(BF16))
        wm.append(fold(jnp.broadcast_to(wc, (LANES, LANES)).T))
        aa.append(_dot_nt(jnp.concatenate([at[j].astype(BF16), rt_b[j]], axis=0), bk))
        lanes = slice(q * LANES, (q + 1) * LANES)
        bonus.append((r * k * rk_ref[:, lanes]).astype(BF16))
    rk_sum = _dot(jnp.concatenate(bonus, axis=0), head_ones)

    a_ab = [jnp.where(strict, aa[j][:cs, :LANES], 0.0) for j in ids]
    a_rb = [jnp.where(incl, aa[j][cs:, :LANES], 0.0).astype(BF16) for j in ids]
    akv_y0 = [_dot(jnp.concatenate([jnp.where(strict, aa[j][:cs, LANES:], 0.0),
                                    jnp.where(incl, aa[j][cs:, LANES:], 0.0)],
                                   axis=0).astype(BF16), vm[j]) for j in ids]

    xk = [_dot(a_ab[j].astype(BF16), stack(a_ab[j]).astype(BF16)) for j in ids]
    tk = [eye + a_ab[j] for j in ids]
    n_sq = (cs - 1).bit_length() - 1
    for step in range(n_sq):
        for j in ids:
            x_bd = stack(xk[j]).astype(BF16)
            if step < n_sq - 1:
                xt = _dot(jnp.concatenate([xk[j], tk[j]], axis=0).astype(BF16), x_bd)
                xk[j] = xt[:cs]
                tk[j] = tk[j] + xt[cs:]
            else:
                tk[j] = tk[j] + _dot(tk[j].astype(BF16), x_bd)

    pq = [_dot(tk[j].astype(BF16),
               jnp.concatenate([stack(at[j]), stack(akv_y0[j][:cs])], axis=1).astype(BF16))
          for j in ids]

    st = [st_ref[j] for j in ids]
    st_b = [stack(st[j]).astype(BF16) for j in ids]
    u_b = [stack(_dot(pq[j][:, :LANES].astype(BF16), st_b[j]) + pq[j][:, LANES:]).astype(BF16)
           for j in ids]
    y = []
    for j in ids:
        ra = jnp.concatenate([rt_b[j], a_rb[j]], axis=1)
        y.append(_dot(ra, jnp.concatenate([st_b[j], u_b[j]], axis=0)) + akv_y0[j][cs:])
        st_ref[j] = wm[j] * st[j] + _dot(bkt[j], jnp.concatenate([u_b[j], vm[j]], axis=0))

    y_all = jnp.concatenate(y, axis=0)
    d_all = y_all - _dot(y_all.astype(BF16), head_mean)
    var_all = _dot((d_all * d_all).astype(BF16), head_mean)
    yn_all = d_all * lax.rsqrt(var_all + GN_EPS)
    for j, (b, q) in enumerate(chains):
        lanes = slice(q * LANES, (q + 1) * LANES)
        rows = slice(j * cs, (j + 1) * cs)
        y_ref[b, q] = (yn_all[rows] * gng_ref[:, lanes] + gnb_ref[:, lanes]
                       + rk_sum[rows] * v_ref[b, q])


def _scan(r, lw, k, v, a, b, rk, gng, gnb):
    bsz, pp, seq, _ = r.shape
    blk = pl.BlockSpec((bsz, pp, SCAN_CHUNK, LANES), lambda s: (0, 0, s, 0))
    vec = pl.BlockSpec((1, pp * LANES), lambda s: (0, 0))
    return pl.pallas_call(
        _scan_kernel,
        grid=(seq // SCAN_CHUNK,),
        in_specs=[blk] * 6 + [vec] * 3,
        out_specs=blk,
        out_shape=jax.ShapeDtypeStruct((bsz, pp, seq, LANES), F32),
        scratch_shapes=[pltpu.VMEM((bsz * pp, SCAN_CHUNK, LANES), F32)],
        compiler_params=pltpu.CompilerParams(
            dimension_semantics=("arbitrary",), vmem_limit_bytes=VMEM_LIMIT),
        name="scan",
    )(r, lw, k, v, a, b, rk, gng, gnb)


def _post_kernel(x_ref, mod_ref, yb_ref, g_ref, gb_ref, ma_ref, wbb_ref, wout_ref, bout_ref,
                 ln1g_ref, ln1b_ref, w1_ref, b1_ref, w2_ref, b2_ref, ln2g_ref, ln2b_ref, o_ref):
    tl = x_ref.shape[1]
    gt1 = mod_ref[0, 2:3, :]
    sh2 = mod_ref[0, 3:4, :]
    sc2 = mod_ref[0, 4:5, :]
    gt2 = mod_ref[0, 5:6, :]
    groups = [slice(i * tl // POST_GROUPS, (i + 1) * tl // POST_GROUPS) for i in range(POST_GROUPS)]
    pb = []
    for rows in groups:
        yb = jnp.concatenate([yb_ref[0, q, rows, :] for q in range(PAIRS)], axis=1) * g_ref[0, rows, :]
        pb.append(_dot(yb.astype(BF16), wbb_ref[...]))
    h1 = []
    for i, rows in enumerate(groups):
        merged = ma_ref[0, rows, :] + gb_ref[0, rows, :].astype(F32) * pb[i]
        mix = _dot(merged.astype(BF16), wout_ref[...]) + bout_ref[...]
        h1.append(_layer_norm(ALPHA * x_ref[0, rows, :] + gt1 * mix,
                              ln1g_ref[...], ln1b_ref[...], LN_EPS))
    t = []
    for i in range(POST_GROUPS):
        h = (h1[i] * (1.0 + sc2) + sh2).astype(BF16)
        t.append(jnp.maximum(_dot(h, w1_ref[...]) + b1_ref[...], 0.0))
    for i, rows in enumerate(groups):
        ff = _dot((t[i] * t[i]).astype(BF16), w2_ref[...]) + b2_ref[...]
        o_ref[0, rows, :] = _layer_norm(ALPHA * h1[i] + gt2 * ff,
                                        ln2g_ref[...], ln2b_ref[...], LN_EPS).astype(o_ref.dtype)


def _post(x, mod, yb, g, gb, ma, p, tl, out_dtype):
    bsz, seq, _ = x.shape
    row = lambda b, s: (b, s, 0)
    consts = [p["wbb"], p["wout"], p["bout"], p["ln1g"], p["ln1b"],
              p["w1"], p["b1"], p["w2"], p["b2"], p["ln2g"], p["ln2b"]]
    return pl.pallas_call(
        _post_kernel,
        grid=(bsz, seq // tl),
        in_specs=[pl.BlockSpec((1, tl, D_MODEL), row),
                  pl.BlockSpec((1, 6, D_MODEL), lambda b, s: (b, 0, 0)),
                  pl.BlockSpec((1, PAIRS, tl, LANES), lambda b, s: (b, 0, s, 0)),
                  pl.BlockSpec((1, tl, R_WIDTH), row),
                  pl.BlockSpec((1, tl, D_MODEL), row),
                  pl.BlockSpec((1, tl, D_MODEL), row)]
                 + [_const_spec(c.shape) for c in consts],
        out_specs=pl.BlockSpec((1, tl, D_MODEL), row),
        out_shape=jax.ShapeDtypeStruct((bsz, seq, D_MODEL), out_dtype),
        compiler_params=pltpu.CompilerParams(
            dimension_semantics=("arbitrary", "arbitrary"), vmem_limit_bytes=VMEM_LIMIT),
        name="post",
    )(x, mod, yb, g, gb, ma, *consts)


def _prepare_params(l, w_in, b_in, g_ln_v, b_ln_v, w_spatial, b_spatial, mu_shift, w0,
                    w_decay_up, a0, w_aaa_up, w_gate_up, k_k, k_a, r_k, gn_gain, gn_bias,
                    w_branch_a, w_branch_b, w_out, b_out, ln1_g, ln1_b, w_ff1, b_ff1, w_ff2,
                    b_ff2, ln2_g, ln2_b):
    g_end = 2 * G_WIDTH
    rkv_end = g_end + 3 * R_WIDTH
    r_end = rkv_end + LORA_COLS
    pad = LORA_PAD - LORA_COLS
    row2 = lambda t: t.reshape(1, -1)
    wi, bi = w_in[l], b_in[l]
    wr = jnp.concatenate([wi[:, g_end:r_end], jnp.zeros((D_MODEL, pad), F32)], axis=1)
    br = jnp.concatenate([bi[g_end:r_end], jnp.zeros((pad,), F32)])
    mu = jnp.concatenate([mu_shift[l], jnp.zeros((pad,), F32)])
    wup = jnp.zeros((LORA_PAD, 3 * R_WIDTH), F32)
    wup = wup.at[0:DECAY_LORA, 0:R_WIDTH].set(w_decay_up[l])
    wup = wup.at[DECAY_LORA:DECAY_LORA + AAA_LORA, R_WIDTH:2 * R_WIDTH].set(w_aaa_up[l])
    wup = wup.at[DECAY_LORA + AAA_LORA:LORA_COLS, 2 * R_WIDTH:].set(w_gate_up[l])
    ws = w_spatial[l].reshape(PAIRS, 2, CHUNK, CHUNK).transpose(0, 2, 1, 3).reshape(PAIRS, CHUNK, 2 * CHUNK)
    bs = jnp.repeat(b_spatial[l].T, G_WIDTH // G_GROUPS, axis=1)
    hid = jnp.arange(R_WIDTH) // R_HEAD
    bd = (hid[:, None] == hid[None, :]).astype(BF16)
    return dict(
        wgu=wi[:, :G_WIDTH].astype(BF16), bgu=row2(bi[:G_WIDTH]),
        wgv=wi[:, G_WIDTH:g_end].astype(BF16), bgv=row2(bi[G_WIDTH:g_end]),
        wr=wr.astype(BF16), br=row2(br),
        wga=wi[:, r_end:r_end + D_MODEL].astype(BF16), bga=row2(bi[r_end:r_end + D_MODEL]),
        wgb=wi[:, r_end + D_MODEL:].astype(BF16), bgb=row2(bi[r_end + D_MODEL:]),
        glnv=row2(g_ln_v[l]), blnv=row2(b_ln_v[l]), ws=ws, bs=bs, mu=row2(mu),
        w0=row2(w0[l]), a0=row2(a0[l]), wup=wup.astype(BF16), kk=row2(k_k[l]), ka=row2(k_a[l]),
        bd=bd, wba=w_branch_a[l].astype(BF16),
        rk=row2(r_k[l]), gng=row2(gn_gain[l]), gnb=row2(gn_bias[l]),
        wbb=w_branch_b[l].astype(BF16), wout=w_out[l].astype(BF16), bout=row2(b_out[l]),
        ln1g=row2(ln1_g[l]), ln1b=row2(ln1_b[l]),
        w1=w_ff1[l].astype(BF16), b1=row2(b_ff1[l]), w2=w_ff2[l].astype(BF16), b2=row2(b_ff2[l]),
        ln2g=row2(ln2_g[l]), ln2b=row2(ln2_b[l]),
    )


def _tile(seq, want):
    t = min(want, seq)
    while seq % t:
        t //= 2
    return t


def kernel(x, c, w_ada, b_ada, w_in, b_in, g_ln_v, b_ln_v, w_spatial, b_spatial, mu_shift, w0, w_decay_up, a0, w_aaa_up, w_gate_up, k_k, k_a, r_k, gn_gain, gn_bias, w_branch_a, w_branch_b, w_out, b_out, ln1_g, ln1_b, w_ff1, b_ff1, w_ff2, b_ff2, ln2_g, ln2_b):
    bsz, seq, _ = x.shape
    assert seq % CHUNK == 0 and x.shape[2] == D_MODEL
    out_dtype = x.dtype
    h_res = x.astype(F32)
    tl = _tile(seq, 512)
    for l in range(DEPTH):
        p = _prepare_params(l, w_in, b_in, g_ln_v, b_ln_v, w_spatial, b_spatial, mu_shift, w0,
                            w_decay_up, a0, w_aaa_up, w_gate_up, k_k, k_a, r_k, gn_gain, gn_bias,
                            w_branch_a, w_branch_b, w_out, b_out, ln1_g, ln1_b, w_ff1, b_ff1,
                            w_ff2, b_ff2, ln2_g, ln2_b)
        mod = _modulation(c.astype(F32), w_ada[l], b_ada[l]).reshape(bsz, 6, D_MODEL)
        ma, gb, r, lw, k, v, a, b, g = _inproj(h_res, mod, p, tl)
        yb = _scan(r, lw, k, v, a, b, p["rk"], p["gng"], p["gnb"])
        h_res = _post(h_res, mod, yb, g, gb, ma, p, tl, F32)
    return h_res.astype(out_dtype)
```

```python
import math

import jax
import jax.numpy as jnp
from jax import lax
from jax.experimental import pallas as pl
from jax.experimental.pallas import tpu as pltpu

D_MODEL = 1024
G_GROUPS = 8
G_WIDTH = 512
CHUNK = 128
R_WIDTH = 512
R_HEAD = 64
R_HEADS = R_WIDTH // R_HEAD
DECAY_LORA = 32
AAA_LORA = 32
GATE_LORA = 96
LORA_COLS = DECAY_LORA + AAA_LORA + GATE_LORA
D_FF = 4 * D_MODEL
DEPTH = 1
ALPHA = (2.0 * DEPTH) ** 0.25
LN_EPS = 1e-5
GN_EPS = 64e-5
DECAY_SCALE = math.exp(-0.5)

LANES = 128
PAIRS = R_WIDTH // LANES
LORA_PAD = 2 * LANES
R_COLS = 3 * R_WIDTH + LORA_PAD
COLS_U = slice(0, G_WIDTH)
COLS_V = slice(G_WIDTH, 2 * G_WIDTH)
COLS_R = slice(2 * G_WIDTH, 2 * G_WIDTH + R_COLS)
COLS_GA = slice(COLS_R.stop, COLS_R.stop + D_MODEL)
COLS_GB = slice(COLS_GA.stop, COLS_GA.stop + D_MODEL)
SCAN_CHUNK = 64
STACK = 2 * SCAN_CHUNK
POST_GROUPS = 2
IN_GROUPS = 2
VMEM_LIMIT = 56 * 1024 * 1024

F32 = jnp.float32
BF16 = jnp.bfloat16


def _dot(a, b):
    return jnp.dot(a, b, preferred_element_type=F32)


def _dot_nt(a, b):
    return lax.dot_general(a, b, (((1,), (1,)), ((), ())), preferred_element_type=F32)


def _sigmoid(x):
    return 1.0 / (1.0 + jnp.exp(-x))


def _gelu_tanh(x):
    c = 0.7978845608028654
    hx = 0.5 * x
    return hx + hx * jnp.tanh(x * (c + (c * 0.044715) * (x * x)))


def _layer_norm(x, g, b, eps):
    mu = jnp.mean(x, axis=-1, keepdims=True)
    d = x - mu
    var = jnp.mean(d * d, axis=-1, keepdims=True)
    return d * lax.rsqrt(var + eps) * g + b


def _const_spec(shape):
    n = len(shape)
    return pl.BlockSpec(shape, lambda *_: (0,) * n)


def _mod_kernel(c_ref, w_ref, b_ref, o_ref):
    c = c_ref[...]
    c_act = c * _sigmoid(c)
    o_ref[...] = jnp.dot(c_act, w_ref[...], preferred_element_type=F32,
                         precision=lax.Precision.HIGHEST) + b_ref[...]


def _modulation(c, w_ada, b_ada, l):
    bsz = c.shape[0]
    n = w_ada.shape[2]
    tn = D_MODEL
    return pl.pallas_call(
        _mod_kernel,
        grid=(n // tn,),
        in_specs=[pl.BlockSpec((bsz, D_MODEL), lambda j: (0, 0)),
                  pl.BlockSpec((None, D_MODEL, tn), lambda j: (l, 0, j)),
                  pl.BlockSpec((None, 1, tn), lambda j: (l, 0, j))],
        out_specs=pl.BlockSpec((bsz, tn), lambda j: (0, j)),
        out_shape=jax.ShapeDtypeStruct((bsz, n), F32),
        name="mod",
    )(c, w_ada, b_ada.reshape(b_ada.shape[0], 1, n))


def _inproj_kernel(x_ref, mod_ref, win_ref, bin_ref,
                   glnv_ref, blnv_ref, ws_ref, bs_ref, mu_ref, w0_ref, a0_ref, wup_ref,
                   kk_ref, ka_ref, bd_ref, wba_ref,
                   ma_ref, gb_ref, r_ref, lw_ref, k_ref, v_ref, a_ref, b_ref, g_ref,
                   zsh_ref):
    tl = x_ref.shape[1]

    @pl.when(pl.program_id(1) == 0)
    def _():
        zsh_ref[0:8, :] = jnp.zeros((8, R_COLS), F32)

    sh1 = mod_ref[0, 0:1, :]
    sc1 = mod_ref[0, 1:2, :]
    gsz = tl // IN_GROUPS
    groups = [slice(i * gsz, (i + 1) * gsz) for i in range(IN_GROUPS)]
    gi = range(IN_GROUPS)
    h = [(x_ref[0, rows, :] * (1.0 + sc1) + sh1).astype(BF16) for rows in groups]
    def proj(i, cols):
        return _dot(h[i], win_ref[:, cols]) + bin_ref[:, cols]

    zr = [proj(i, COLS_R) for i in gi]
    for i, rows in enumerate(groups):
        zsh_ref[8 + rows.start:8 + rows.stop, :] = zr[i]
    zu = [proj(i, COLS_U) for i in gi]
    zv = [proj(i, COLS_V) for i in gi]

    z = []
    for i, rows in enumerate(groups):
        prev = zsh_ref[7 + rows.start:7 + rows.stop, :]
        z.append(zr[i] + (prev - zr[i]) * mu_ref[...])
    zsh_ref[7:8, :] = zr[-1][gsz - 1:gsz, :]
    llane = lax.broadcasted_iota(jnp.int32, (gsz, LORA_PAD), 1)
    lin = []
    for i in gi:
        xl = z[i][:, 3 * R_WIDTH:]
        lin.append(jnp.where(llane < DECAY_LORA, jnp.tanh(xl),
                             jnp.where(llane < DECAY_LORA + AAA_LORA, xl, _sigmoid(xl))).astype(BF16))
    zgb = [proj(i, COLS_GB) for i in gi]
    up = [_dot(lin[i], wup_ref[...]) for i in gi]

    u = [_gelu_tanh(zu[i]) for i in gi]
    v = [_layer_norm(_gelu_tanh(zv[i]), glnv_ref[...], blnv_ref[...], LN_EPS) for i in gi]

    kk, n2 = [], []
    for i, rows in enumerate(groups):
        lw = -DECAY_SCALE * _sigmoid(w0_ref[...] + up[i][:, 0:R_WIDTH])
        g_ref[0, rows, :] = up[i][:, 2 * R_WIDTH:]
        kk.append(z[i][:, R_WIDTH:2 * R_WIDTH] * kk_ref[...])
        n2.append(_dot((kk[i] * kk[i]).astype(BF16), bd_ref[...]))
        for q in range(PAIRS):
            sl = slice(q * LANES, (q + 1) * LANES)
            lw_ref[0, q, rows, :] = lw[:, sl]
            r_ref[0, q, rows, :] = z[i][:, sl]
            v_ref[0, q, rows, :] = z[i][:, 2 * R_WIDTH + q * LANES:2 * R_WIDTH + (q + 1) * LANES]

    lane = lax.broadcasted_iota(jnp.int32, (CHUNK, LANES), 1)
    first_head = lane < R_HEAD
    trow = lax.broadcasted_iota(jnp.int32, (CHUNK, 2 * CHUNK), 0)
    scol = lax.broadcasted_iota(jnp.int32, (CHUNK, 2 * CHUNK), 1) % CHUNK
    causal = trow >= scol
    ws = [jnp.where(causal, ws_ref[q], 0.0).astype(BF16) for q in range(PAIRS)]
    ya = []
    for i in gi:
        ya_rows = []
        for c in range(gsz // CHUNK):
            vc = v[i][c * CHUNK:(c + 1) * CHUNK, :]
            s_parts = []
            for q in range(PAIRS):
                v2 = vc[:, q * LANES:(q + 1) * LANES]
                vm = jnp.concatenate([jnp.where(first_head, v2, 0.0),
                                      jnp.where(first_head, 0.0, v2)], axis=0).astype(BF16)
                s_parts.append(_dot(ws[q], vm))
            s = jnp.concatenate(s_parts, axis=1) + bs_ref[...]
            ya_rows.append(u[i][c * CHUNK:(c + 1) * CHUNK, :] * s)
        ya.append(jnp.concatenate(ya_rows, axis=0).astype(BF16))
    zga = [proj(i, COLS_GA) for i in gi]
    pa = [_dot(ya[i], wba_ref[...]) for i in gi]

    for i, rows in enumerate(groups):
        gb_ref[0, rows, :] = _sigmoid(zgb[i]).astype(gb_ref.dtype)
        a = _sigmoid(a0_ref[...] + up[i][:, R_WIDTH:2 * R_WIDTH])
        kkn = kk[i] * lax.rsqrt(jnp.maximum(n2[i], 1e-24))
        k2 = z[i][:, R_WIDTH:2 * R_WIDTH] * (1.0 + (a - 1.0) * ka_ref[...])
        nkk = -kkn
        kka = kkn * a
        for q in range(PAIRS):
            sl = slice(q * LANES, (q + 1) * LANES)
            k_ref[0, q, rows, :] = k2[:, sl]
            a_ref[0, q, rows, :] = nkk[:, sl]
            b_ref[0, q, rows, :] = kka[:, sl]
        ma_ref[0, rows, :] = _sigmoid(zga[i]) * pa[i]


def _inproj(x, mod, p, tl):
    bsz, seq, _ = x.shape
    grid = (bsz, seq // tl)
    row = lambda b, s: (b, s, 0)
    pair = lambda b, s: (b, 0, s, 0)
    pair_shape = jax.ShapeDtypeStruct((bsz, PAIRS, seq, LANES), F32)
    pair_spec = pl.BlockSpec((1, PAIRS, tl, LANES), pair)
    consts = [p["win"], p["bin"], p["glnv"], p["blnv"],
              p["ws"], p["bs"], p["mu"], p["w0"], p["a0"], p["wup"], p["kk"], p["ka"], p["bd"],
              p["wba"]]
    return pl.pallas_call(
        _inproj_kernel,
        grid=grid,
        in_specs=[pl.BlockSpec((1, tl, D_MODEL), row),
                  pl.BlockSpec((1, 6, D_MODEL), lambda b, s: (b, 0, 0))]
                 + [_const_spec(c.shape) for c in consts],
        out_specs=[pl.BlockSpec((1, tl, D_MODEL), row),
                   pl.BlockSpec((1, tl, D_MODEL), row)]
                  + [pair_spec] * 6
                  + [pl.BlockSpec((1, tl, R_WIDTH), row)],
        out_shape=[jax.ShapeDtypeStruct((bsz, seq, D_MODEL), F32),
                   jax.ShapeDtypeStruct((bsz, seq, D_MODEL), BF16)]
                  + [pair_shape] * 6
                  + [jax.ShapeDtypeStruct((bsz, seq, R_WIDTH), F32)],
        scratch_shapes=[pltpu.VMEM((tl + 8, R_COLS), F32)],
        compiler_params=pltpu.CompilerParams(
            dimension_semantics=("arbitrary", "arbitrary"), vmem_limit_bytes=VMEM_LIMIT),
        name="inproj",
    )(x, mod, *consts)


def _scan_kernel(r_ref, lw_ref, k_ref, v_ref, a_ref, b_ref, rk_ref, gng_ref, gnb_ref,
                 y_ref, st_ref):
    bt, pp, cs, _ = r_ref.shape

    @pl.when(pl.program_id(0) == 0)
    def _():
        st_ref[...] = jnp.zeros(st_ref.shape, F32)

    lane = lax.broadcasted_iota(jnp.int32, (cs, LANES), 1)
    first_head = lane < R_HEAD

    def stack(t):
        return jnp.concatenate([jnp.where(first_head, t, 0.0),
                                jnp.where(first_head, 0.0, t)], axis=0)

    def fold(t):
        return jnp.where(first_head, t[:cs], t[cs:])

    ri = lax.broadcasted_iota(jnp.int32, (cs, cs), 0)
    ci = lax.broadcasted_iota(jnp.int32, (cs, cs), 1)
    tri_ones = jnp.where(ri >= ci, 1.0, 0.0).astype(BF16)
    si = lax.broadcasted_iota(jnp.int32, (cs, LANES), 0)
    sj = lane % cs
    strict = si > sj
    incl = si >= sj
    eye = jnp.where(si == sj, 1.0, 0.0).astype(F32)
    bsi = lax.broadcasted_iota(jnp.int32, (LANES, LANES), 0) // R_HEAD
    bsj = lax.broadcasted_iota(jnp.int32, (LANES, LANES), 1) // R_HEAD
    head_ones = jnp.where(bsi == bsj, 1.0, 0.0).astype(BF16)
    head_mean = jnp.where(bsi == bsj, 1.0 / R_HEAD, 0.0).astype(BF16)

    chains = [(b, q) for q in range(pp) for b in range(bt)]
    ids = range(len(chains))

    lw = [lw_ref[b, q] for b, q in chains]
    cum = []
    for j in ids:
        p1 = lw[j].astype(BF16)
        e1 = lw[j] - p1.astype(F32)
        p2 = e1.astype(BF16)
        p3 = (e1 - p2.astype(F32)).astype(BF16)
        cum.append(_dot(tri_ones, p1) + _dot(tri_ones, p2) + _dot(tri_ones, p3))

    at, rt_b, vm, bkt, wm, aa, bonus = [], [], [], [], [], [], []
    for j, (b, q) in enumerate(chains):
        last = cum[j][cs - 1:cs, :]
        e_pos = jnp.exp(cum[j])
        e_pos_ex = jnp.exp(cum[j] - lw[j])
        e_neg = jnp.exp(-cum[j])
        e_end = jnp.exp(last - cum[j])
        wc = jnp.exp(last)
        r = r_ref[b, q]
        k = k_ref[b, q]
        v = v_ref[b, q]
        a = a_ref[b, q]
        bb = b_ref[b, q]
        at.append(a * e_pos_ex)
        rt_b.append((r * e_pos).astype(BF16))
        bk = jnp.concatenate([stack(bb * e_neg), stack(k * e_neg)], axis=0).astype(BF16)
        vm.append(stack(v).astype(BF16))
        bkt.append(jnp.concatenate([fold(stack(bb * e_end).T), fold(stack(k * e_end).T)],
                                   axis=1).astype(BF16))
        wm.append(fold(jnp.broadcast_to(wc, (LANES, LANES)).T))
        aa.append(_dot_nt(jnp.concatenate([at[j].astype(BF16), rt_b[j]], axis=0), bk))
        lanes = slice(q * LANES, (q + 1) * LANES)
        bonus.append((r * k * rk_ref[:, lanes]).astype(BF16))
    rk_sum = _dot(jnp.concatenate(bonus, axis=0), head_ones)

    a_ab = [jnp.where(strict, aa[j][:cs, :LANES], 0.0) for j in ids]
    a_rb = [jnp.where(incl, aa[j][cs:, :LANES], 0.0).astype(BF16) for j in ids]
    akv_y0 = [_dot(jnp.concatenate([jnp.where(strict, aa[j][:cs, LANES:], 0.0),
                                    jnp.where(incl, aa[j][cs:, LANES:], 0.0)],
                                   axis=0).astype(BF16), vm[j]) for j in ids]

    xk = [_dot(a_ab[j].astype(BF16), stack(a_ab[j]).astype(BF16)) for j in ids]
    tk = [eye + a_ab[j] for j in ids]
    n_sq = (cs - 1).bit_length() - 1
    for step in range(n_sq):
        for j in ids:
            x_bd = stack(xk[j]).astype(BF16)
            if step < n_sq - 1:
                xt = _dot(jnp.concatenate([xk[j], tk[j]], axis=0).astype(BF16), x_bd)
                xk[j] = xt[:cs]
                tk[j] = tk[j] + xt[cs:]
            else:
                tk[j] = tk[j] + _dot(tk[j].astype(BF16), x_bd)

    pq = [_dot(tk[j].astype(BF16),
               jnp.concatenate([stack(at[j]), stack(akv_y0[j][:cs])], axis=1).astype(BF16))
          for j in ids]

    st = [st_ref[j] for j in ids]
    st_b = [stack(st[j]).astype(BF16) for j in ids]
    u_b = [stack(_dot(pq[j][:, :LANES].astype(BF16), st_b[j]) + pq[j][:, LANES:]).astype(BF16)
           for j in ids]
    y = []
    for j in ids:
        ra = jnp.concatenate([rt_b[j], a_rb[j]], axis=1)
        y.append(_dot(ra, jnp.concatenate([st_b[j], u_b[j]], axis=0)) + akv_y0[j][cs:])
        st_ref[j] = wm[j] * st[j] + _dot(bkt[j], jnp.concatenate([u_b[j], vm[j]], axis=0))

    y_all = jnp.concatenate(y, axis=0)
    d_all = y_all - _dot(y_all.astype(BF16), head_mean)
    var_all = _dot((d_all * d_all).astype(BF16), head_mean)
    yn_all = d_all * lax.rsqrt(var_all + GN_EPS)
    for j, (b, q) in enumerate(chains):
        lanes = slice(q * LANES, (q + 1) * LANES)
        rows = slice(j * cs, (j + 1) * cs)
        y_ref[b, q] = (yn_all[rows] * gng_ref[:, lanes] + gnb_ref[:, lanes]
                       + rk_sum[rows] * v_ref[b, q])


def _scan(r, lw, k, v, a, b, rk, gng, gnb):
    bsz, pp, seq, _ = r.shape
    blk = pl.BlockSpec((bsz, pp, SCAN_CHUNK, LANES), lambda s: (0, 0, s, 0))
    vec = pl.BlockSpec((1, pp * LANES), lambda s: (0, 0))
    return pl.pallas_call(
        _scan_kernel,
        grid=(seq // SCAN_CHUNK,),
        in_specs=[blk] * 6 + [vec] * 3,
        out_specs=blk,
        out_shape=jax.ShapeDtypeStruct((bsz, pp, seq, LANES), F32),
        scratch_shapes=[pltpu.VMEM((bsz * pp, SCAN_CHUNK, LANES), F32)],
        compiler_params=pltpu.CompilerParams(
            dimension_semantics=("arbitrary",), vmem_limit_bytes=VMEM_LIMIT),
        name="scan",
    )(r, lw, k, v, a, b, rk, gng, gnb)


def _post_kernel(x_ref, mod_ref, yb_ref, g_ref, gb_ref, ma_ref, wbb_ref, wout_ref, bout_ref,
                 ln1g_ref, ln1b_ref, w1_ref, b1_ref, w2_ref, b2_ref, ln2g_ref, ln2b_ref, o_ref):
    tl = x_ref.shape[1]
    gt1 = mod_ref[0, 2:3, :]
    sh2 = mod_ref[0, 3:4, :]
    sc2 = mod_ref[0, 4:5, :]
    gt2 = mod_ref[0, 5:6, :]
    groups = [slice(i * tl // POST_GROUPS, (i + 1) * tl // POST_GROUPS) for i in range(POST_GROUPS)]
    pb = []
    for rows in groups:
        yb = jnp.concatenate([yb_ref[0, q, rows, :] for q in range(PAIRS)], axis=1) * g_ref[0, rows, :]
        pb.append(_dot(yb.astype(BF16), wbb_ref[...]))
    h1 = []
    for i, rows in enumerate(groups):
        merged = ma_ref[0, rows, :] + gb_ref[0, rows, :].astype(F32) * pb[i]
        mix = _dot(merged.astype(BF16), wout_ref[...]) + bout_ref[...]
        h1.append(_layer_norm(ALPHA * x_ref[0, rows, :] + gt1 * mix,
                              ln1g_ref[...], ln1b_ref[...], LN_EPS))
    t = []
    for i in range(POST_GROUPS):
        h = (h1[i] * (1.0 + sc2) + sh2).astype(BF16)
        t.append(jnp.maximum(_dot(h, w1_ref[...]) + b1_ref[...], 0.0))
    for i, rows in enumerate(groups):
        ff = _dot((t[i] * t[i]).astype(BF16), w2_ref[...]) + b2_ref[...]
        o_ref[0, rows, :] = _layer_norm(ALPHA * h1[i] + gt2 * ff,
                                        ln2g_ref[...], ln2b_ref[...], LN_EPS).astype(o_ref.dtype)


def _post(x, mod, yb, g, gb, ma, p, tl, out_dtype):
    bsz, seq, _ = x.shape
    row = lambda b, s: (b, s, 0)
    consts = [p["wbb"], p["wout"], p["bout"], p["ln1g"], p["ln1b"],
              p["w1"], p["b1"], p["w2"], p["b2"], p["ln2g"], p["ln2b"]]
    return pl.pallas_call(
        _post_kernel,
        grid=(bsz, seq // tl),
        in_specs=[pl.BlockSpec((1, tl, D_MODEL), row),
                  pl.BlockSpec((1, 6, D_MODEL), lambda b, s: (b, 0, 0)),
                  pl.BlockSpec((1, PAIRS, tl, LANES), lambda b, s: (b, 0, s, 0)),
                  pl.BlockSpec((1, tl, R_WIDTH), row),
                  pl.BlockSpec((1, tl, D_MODEL), row),
                  pl.BlockSpec((1, tl, D_MODEL), row)]
                 + [_const_spec(c.shape) for c in consts],
        out_specs=pl.BlockSpec((1, tl, D_MODEL), row),
        out_shape=jax.ShapeDtypeStruct((bsz, seq, D_MODEL), out_dtype),
        compiler_params=pltpu.CompilerParams(
            dimension_semantics=("arbitrary", "arbitrary"), vmem_limit_bytes=VMEM_LIMIT),
        name="post",
    )(x, mod, yb, g, gb, ma, *consts)


def _prepare_params(l, w_in, b_in, g_ln_v, b_ln_v, w_spatial, b_spatial, mu_shift, w0,
                    w_decay_up, a0, w_aaa_up, w_gate_up, k_k, k_a, r_k, gn_gain, gn_bias,
                    w_branch_a, w_branch_b, w_out, b_out, ln1_g, ln1_b, w_ff1, b_ff1, w_ff2,
                    b_ff2, ln2_g, ln2_b):
    g_end = 2 * G_WIDTH
    rkv_end = g_end + 3 * R_WIDTH
    r_end = rkv_end + LORA_COLS
    pad = LORA_PAD - LORA_COLS
    row2 = lambda t: t.reshape(1, -1)
    wi, bi = w_in[l], b_in[l]
    win = jnp.concatenate([wi[:, :r_end], jnp.zeros((D_MODEL, pad), F32), wi[:, r_end:]], axis=1)
    b_all = jnp.concatenate([bi[:r_end], jnp.zeros((pad,), F32), bi[r_end:]])
    mu = jnp.concatenate([mu_shift[l], jnp.zeros((pad,), F32)])
    wup = jnp.zeros((LORA_PAD, 3 * R_WIDTH), F32)
    wup = wup.at[0:DECAY_LORA, 0:R_WIDTH].set(w_decay_up[l])
    wup = wup.at[DECAY_LORA:DECAY_LORA + AAA_LORA, R_WIDTH:2 * R_WIDTH].set(w_aaa_up[l])
    wup = wup.at[DECAY_LORA + AAA_LORA:LORA_COLS, 2 * R_WIDTH:].set(w_gate_up[l])
    ws = w_spatial[l].reshape(PAIRS, 2, CHUNK, CHUNK).transpose(0, 2, 1, 3).reshape(PAIRS, CHUNK, 2 * CHUNK)
    bs = jnp.repeat(b_spatial[l].T, G_WIDTH // G_GROUPS, axis=1)
    hid = jnp.arange(R_WIDTH) // R_HEAD
    bd = (hid[:, None] == hid[None, :]).astype(BF16)
    return dict(
        win=win.astype(BF16), bin=row2(b_all),
        glnv=row2(g_ln_v[l]), blnv=row2(b_ln_v[l]), ws=ws, bs=bs, mu=row2(mu),
        w0=row2(w0[l]), a0=row2(a0[l]), wup=wup.astype(BF16), kk=row2(k_k[l]), ka=row2(k_a[l]),
        bd=bd, wba=w_branch_a[l].astype(BF16),
        rk=row2(r_k[l]), gng=row2(gn_gain[l]), gnb=row2(gn_bias[l]),
        wbb=w_branch_b[l].astype(BF16), wout=w_out[l].astype(BF16), bout=row2(b_out[l]),
        ln1g=row2(ln1_g[l]), ln1b=row2(ln1_b[l]),
        w1=w_ff1[l].astype(BF16), b1=row2(b_ff1[l]), w2=w_ff2[l].astype(BF16), b2=row2(b_ff2[l]),
        ln2g=row2(ln2_g[l]), ln2b=row2(ln2_b[l]),
    )


def _tile(seq, want):
    t = min(want, seq)
    while seq % t:
        t //= 2
    return t


def kernel(x, c, w_ada, b_ada, w_in, b_in, g_ln_v, b_ln_v, w_spatial, b_spatial, mu_shift, w0, w_decay_up, a0, w_aaa_up, w_gate_up, k_k, k_a, r_k, gn_gain, gn_bias, w_branch_a, w_branch_b, w_out, b_out, ln1_g, ln1_b, w_ff1, b_ff1, w_ff2, b_ff2, ln2_g, ln2_b):
    bsz, seq, _ = x.shape
    assert seq % CHUNK == 0 and x.shape[2] == D_MODEL
    out_dtype = x.dtype
    h_res = x.astype(F32)
    tl = _tile(seq, 512)
    for l in range(DEPTH):
        p = _prepare_params(l, w_in, b_in, g_ln_v, b_ln_v, w_spatial, b_spatial, mu_shift, w0,
                            w_decay_up, a0, w_aaa_up, w_gate_up, k_k, k_a, r_k, gn_gain, gn_bias,
                            w_branch_a, w_branch_b, w_out, b_out, ln1_g, ln1_b, w_ff1, b_ff1,
                            w_ff2, b_ff2, ln2_g, ln2_b)
        mod = _modulation(c.astype(F32), w_ada, b_ada, l).reshape(bsz, 6, D_MODEL)
        ma, gb, r, lw, k, v, a, b, g = _inproj(h_res, mod, p, tl)
        yb = _scan(r, lw, k, v, a, b, p["rk"], p["gng"], p["gnb"])
        h_res = _post(h_res, mod, yb, g, gb, ma, p, tl, F32)
    return h_res.astype(out_dtype)
```

```python
import math

import jax
import jax.numpy as jnp
from jax import lax
from jax.experimental import pallas as pl
from jax.experimental.pallas import tpu as pltpu

D_MODEL = 1024
G_GROUPS = 8
G_WIDTH = 512
CHUNK = 128
R_WIDTH = 512
R_HEAD = 64
R_HEADS = R_WIDTH // R_HEAD
DECAY_LORA = 32
AAA_LORA = 32
GATE_LORA = 96
LORA_COLS = DECAY_LORA + AAA_LORA + GATE_LORA
D_FF = 4 * D_MODEL
DEPTH = 1
ALPHA = (2.0 * DEPTH) ** 0.25
LN_EPS = 1e-5
GN_EPS = 64e-5
DECAY_SCALE = math.exp(-0.5)

LANES = 128
PAIRS = R_WIDTH // LANES
LORA_PAD = 2 * LANES
R_COLS = 3 * R_WIDTH + LORA_PAD
COLS_U = slice(0, G_WIDTH)
COLS_V = slice(G_WIDTH, 2 * G_WIDTH)
COLS_R = slice(2 * G_WIDTH, 2 * G_WIDTH + R_COLS)
COLS_GA = slice(COLS_R.stop, COLS_R.stop + D_MODEL)
COLS_GB = slice(COLS_GA.stop, COLS_GA.stop + D_MODEL)
SCAN_CHUNK = 64
STACK = 2 * SCAN_CHUNK
POST_GROUPS = 2
IN_GROUPS = 2
SCAN_GROUPS = 1
SCAN_LAG = 3
VMEM_LIMIT = 56 * 1024 * 1024

F32 = jnp.float32
BF16 = jnp.bfloat16


def _dot(a, b):
    return jnp.dot(a, b, preferred_element_type=F32)


def _sigmoid(x):
    return 1.0 / (1.0 + jnp.exp(-x))


def _gelu_tanh(x):
    c = 0.7978845608028654
    hx = 0.5 * x
    return hx + hx * jnp.tanh(x * (c + (c * 0.044715) * (x * x)))


def _layer_norm(x, g, b, eps):
    mu = jnp.mean(x, axis=-1, keepdims=True)
    d = x - mu
    var = jnp.mean(d * d, axis=-1, keepdims=True)
    return d * lax.rsqrt(var + eps) * g + b


def _const_spec(shape):
    n = len(shape)
    return pl.BlockSpec(shape, lambda *_: (0,) * n)


def _mod_kernel(c_ref, w_ref, b_ref, o_ref):
    c = c_ref[...]
    c_act = c * _sigmoid(c)
    o_ref[...] = jnp.dot(c_act, w_ref[...], preferred_element_type=F32,
                         precision=lax.Precision.HIGHEST) + b_ref[...]


def _modulation(c, w_ada, b_ada, l):
    bsz = c.shape[0]
    n = w_ada.shape[2]
    tn = D_MODEL
    return pl.pallas_call(
        _mod_kernel,
        grid=(n // tn,),
        in_specs=[pl.BlockSpec((bsz, D_MODEL), lambda j: (0, 0)),
                  pl.BlockSpec((None, D_MODEL, tn), lambda j: (l, 0, j)),
                  pl.BlockSpec((None, 1, tn), lambda j: (l, 0, j))],
        out_specs=pl.BlockSpec((bsz, tn), lambda j: (0, j)),
        out_shape=jax.ShapeDtypeStruct((bsz, n), F32),
        name="mod",
    )(c, w_ada, b_ada.reshape(b_ada.shape[0], 1, n))


def _inproj_kernel(x_ref, mod_ref, win_ref, bin_ref,
                   glnv_ref, blnv_ref, ws_ref, bs_ref, mu_ref, w0_ref, a0_ref, wup_ref,
                   kk_ref, ka_ref, bd_ref, wba_ref,
                   ma_ref, gb_ref, r_ref, lw_ref, k_ref, v_ref, a_ref, b_ref, g_ref,
                   zsh_ref):
    tl = x_ref.shape[1]

    @pl.when(pl.program_id(1) == 0)
    def _():
        zsh_ref[0:8, :] = jnp.zeros((8, R_COLS), F32)

    sh1 = mod_ref[0, 0:1, :]
    sc1 = mod_ref[0, 1:2, :]
    gsz = tl // IN_GROUPS
    groups = [slice(i * gsz, (i + 1) * gsz) for i in range(IN_GROUPS)]
    gi = range(IN_GROUPS)
    h = [(x_ref[0, rows, :] * (1.0 + sc1) + sh1).astype(BF16) for rows in groups]
    def proj(i, cols):
        return _dot(h[i], win_ref[:, cols]) + bin_ref[:, cols]

    zr = [proj(i, COLS_R) for i in gi]
    for i, rows in enumerate(groups):
        zsh_ref[8 + rows.start:8 + rows.stop, :] = zr[i]
    zu = [proj(i, COLS_U) for i in gi]
    zv = [proj(i, COLS_V) for i in gi]

    z = []
    for i, rows in enumerate(groups):
        prev = zsh_ref[7 + rows.start:7 + rows.stop, :]
        z.append(zr[i] + (prev - zr[i]) * mu_ref[...])
    zsh_ref[7:8, :] = zr[-1][gsz - 1:gsz, :]
    llane = lax.broadcasted_iota(jnp.int32, (gsz, LORA_PAD), 1)
    lin = []
    for i in gi:
        xl = z[i][:, 3 * R_WIDTH:]
        lin.append(jnp.where(llane < DECAY_LORA, jnp.tanh(xl),
                             jnp.where(llane < DECAY_LORA + AAA_LORA, xl, _sigmoid(xl))).astype(BF16))
    zgb = [proj(i, COLS_GB) for i in gi]
    up = [_dot(lin[i], wup_ref[...]) for i in gi]

    u = [_gelu_tanh(zu[i]) for i in gi]
    v = [_layer_norm(_gelu_tanh(zv[i]), glnv_ref[...], blnv_ref[...], LN_EPS) for i in gi]

    kk, n2 = [], []
    for i, rows in enumerate(groups):
        lw = -DECAY_SCALE * _sigmoid(w0_ref[...] + up[i][:, 0:R_WIDTH])
        g_ref[0, rows, :] = up[i][:, 2 * R_WIDTH:].astype(g_ref.dtype)
        kk.append(z[i][:, R_WIDTH:2 * R_WIDTH] * kk_ref[...])
        n2.append(_dot((kk[i] * kk[i]).astype(BF16), bd_ref[...]))
        for q in range(PAIRS):
            sl = slice(q * LANES, (q + 1) * LANES)
            lw_ref[0, q, rows, :] = lw[:, sl]
            r_ref[0, q, rows, :] = z[i][:, sl]
            v_ref[0, q, rows, :] = z[i][:, 2 * R_WIDTH + q * LANES:2 * R_WIDTH + (q + 1) * LANES]

    lane = lax.broadcasted_iota(jnp.int32, (CHUNK, LANES), 1)
    first_head = lane < R_HEAD
    trow = lax.broadcasted_iota(jnp.int32, (CHUNK, 2 * CHUNK), 0)
    scol = lax.broadcasted_iota(jnp.int32, (CHUNK, 2 * CHUNK), 1) % CHUNK
    causal = trow >= scol
    ws = [jnp.where(causal, ws_ref[q], 0.0).astype(BF16) for q in range(PAIRS)]
    ya = []
    for i in gi:
        ya_rows = []
        for c in range(gsz // CHUNK):
            vc = v[i][c * CHUNK:(c + 1) * CHUNK, :]
            s_parts = []
            for q in range(PAIRS):
                v2 = vc[:, q * LANES:(q + 1) * LANES]
                vm = jnp.concatenate([jnp.where(first_head, v2, 0.0),
                                      jnp.where(first_head, 0.0, v2)], axis=0).astype(BF16)
                s_parts.append(_dot(ws[q], vm))
            s = jnp.concatenate(s_parts, axis=1) + bs_ref[...]
            ya_rows.append(u[i][c * CHUNK:(c + 1) * CHUNK, :] * s)
        ya.append(jnp.concatenate(ya_rows, axis=0).astype(BF16))
    zga = [proj(i, COLS_GA) for i in gi]
    pa = [_dot(ya[i], wba_ref[...]) for i in gi]

    for i, rows in enumerate(groups):
        gb_ref[0, rows, :] = _sigmoid(zgb[i]).astype(gb_ref.dtype)
        a = _sigmoid(a0_ref[...] + up[i][:, R_WIDTH:2 * R_WIDTH])
        kkn = kk[i] * lax.rsqrt(jnp.maximum(n2[i], 1e-24))
        k2 = z[i][:, R_WIDTH:2 * R_WIDTH] * (1.0 + (a - 1.0) * ka_ref[...])
        nkk = -kkn
        kka = kkn * a
        for q in range(PAIRS):
            sl = slice(q * LANES, (q + 1) * LANES)
            k_ref[0, q, rows, :] = k2[:, sl]
            a_ref[0, q, rows, :] = nkk[:, sl]
            b_ref[0, q, rows, :] = kka[:, sl]
        ma_ref[0, rows, :] = (_sigmoid(zga[i]) * pa[i]).astype(ma_ref.dtype)


def _inproj(x, mod, p, tl):
    bsz, seq, _ = x.shape
    grid = (bsz, seq // tl)
    row = lambda b, s: (b, s, 0)
    pair = lambda b, s: (b, 0, s, 0)
    pair_shape = jax.ShapeDtypeStruct((bsz, PAIRS, seq, LANES), F32)
    pair_spec = pl.BlockSpec((1, PAIRS, tl, LANES), pair)
    consts = [p["win"], p["bin"], p["glnv"], p["blnv"],
              p["ws"], p["bs"], p["mu"], p["w0"], p["a0"], p["wup"], p["kk"], p["ka"], p["bd"],
              p["wba"]]
    return pl.pallas_call(
        _inproj_kernel,
        grid=grid,
        in_specs=[pl.BlockSpec((1, tl, D_MODEL), row),
                  pl.BlockSpec((1, 6, D_MODEL), lambda b, s: (b, 0, 0))]
                 + [_const_spec(c.shape) for c in consts],
        out_specs=[pl.BlockSpec((1, tl, D_MODEL), row),
                   pl.BlockSpec((1, tl, D_MODEL), row)]
                  + [pair_spec] * 6
                  + [pl.BlockSpec((1, tl, R_WIDTH), row)],
        out_shape=[jax.ShapeDtypeStruct((bsz, seq, D_MODEL), BF16),
                   jax.ShapeDtypeStruct((bsz, seq, D_MODEL), BF16)]
                  + [pair_shape] * 6
                  + [jax.ShapeDtypeStruct((bsz, seq, R_WIDTH), BF16)],
        scratch_shapes=[pltpu.VMEM((tl + 8, R_COLS), F32)],
        compiler_params=pltpu.CompilerParams(
            dimension_semantics=("arbitrary", "arbitrary"), vmem_limit_bytes=VMEM_LIMIT),
        name="inproj",
    )(x, mod, *consts)


def _scan_kernel(r_ref, lw_ref, k_ref, v_ref, a_ref, b_ref, rk_ref, gng_ref, gnb_ref,
                 y_ref, st_ref):
    bt, pp, cs, _ = r_ref.shape

    @pl.when(pl.program_id(0) == 0)
    def _():
        st_ref[...] = jnp.zeros(st_ref.shape, F32)

    lane = lax.broadcasted_iota(jnp.int32, (cs, LANES), 1)
    first_head = lane < R_HEAD

    def stack(t):
        return jnp.concatenate([jnp.where(first_head, t, 0.0),
                                jnp.where(first_head, 0.0, t)], axis=0)

    def fold(t):
        return jnp.where(first_head, t[:cs], t[cs:])

    ri = lax.broadcasted_iota(jnp.int32, (cs, cs), 0)
    ci = lax.broadcasted_iota(jnp.int32, (cs, cs), 1)
    tri_ones = jnp.where(ri >= ci, 1.0, 0.0).astype(BF16)
    si = lax.broadcasted_iota(jnp.int32, (cs, LANES), 0)
    sj = lane % cs
    strict = si > sj
    incl = si >= sj
    eye = jnp.where(si == sj, 1.0, 0.0).astype(F32)
    bsi = lax.broadcasted_iota(jnp.int32, (LANES, LANES), 0) // R_HEAD
    bsj = lax.broadcasted_iota(jnp.int32, (LANES, LANES), 1) // R_HEAD
    head_ones = jnp.where(bsi == bsj, 1.0, 0.0).astype(BF16)
    head_mean = jnp.where(bsi == bsj, 1.0 / R_HEAD, 0.0).astype(BF16)

    chains = [(b, q) for q in range(pp) for b in range(bt)]
    n_sq = (cs - 1).bit_length() - 1
    cstate = [dict(j=j, b=b, q=q) for j, (b, q) in enumerate(chains)]

    def s_cumsum(c):
        lw = lw_ref[c["b"], c["q"]]
        p1 = lw.astype(BF16)
        e1 = lw - p1.astype(F32)
        p2 = e1.astype(BF16)
        p3 = (e1 - p2.astype(F32)).astype(BF16)
        c["lw"] = lw
        c["cum"] = _dot(tri_ones, p1) + _dot(tri_ones, p2) + _dot(tri_ones, p3)

    def s_operands(c):
        b, q, cum, lw = c["b"], c["q"], c["cum"], c["lw"]
        last = cum[cs - 1:cs, :]
        e_pos = jnp.exp(cum)
        e_pos_ex = jnp.exp(cum - lw)
        e_neg = jnp.exp(-cum)
        wcol = jnp.broadcast_to(jnp.exp(last), (LANES, LANES)).T
        r = r_ref[b, q]
        k = k_ref[b, q]
        bb = b_ref[b, q]
        c["at"] = a_ref[b, q] * e_pos_ex
        c["rt_b"] = (r * e_pos).astype(BF16)
        tb = stack(bb * e_neg).T
        tk = stack(k * e_neg).T
        c["vm"] = stack(v_ref[b, q]).astype(BF16)
        c["bt"] = fold(tb * wcol).astype(BF16)
        c["kt"] = fold(tk * wcol)
        c["wm"] = fold(wcol)
        c["aa"] = _dot(jnp.concatenate([c["at"].astype(BF16), c["rt_b"]], axis=0),
                       jnp.concatenate([tb, tk], axis=1).astype(BF16))
        lanes = slice(q * LANES, (q + 1) * LANES)
        c["rkr"] = (r * k * rk_ref[:, lanes]).astype(BF16)

    def s_intra(c):
        aa = c.pop("aa")
        a_ab = jnp.where(strict, aa[:cs, :LANES], 0.0)
        c["a_rb"] = jnp.where(incl, aa[cs:, :LANES], 0.0).astype(BF16)
        c["akv_y0"] = _dot(jnp.concatenate([jnp.where(strict, aa[:cs, LANES:], 0.0),
                                            jnp.where(incl, aa[cs:, LANES:], 0.0),
                                            c.pop("kt")], axis=0).astype(BF16), c.pop("vm"))
        c["xk"] = _dot(a_ab.astype(BF16), stack(a_ab).astype(BF16))
        c["tk"] = eye + a_ab

    def s_square(c):
        x_bd = stack(c["xk"]).astype(BF16)
        xt = _dot(jnp.concatenate([c["xk"], c["tk"]], axis=0).astype(BF16), x_bd)
        c["xk"] = xt[:cs]
        c["tk"] = c["tk"] + xt[cs:]

    def s_solve(c):
        x_bd = stack(c.pop("xk")).astype(BF16)
        tk = c["tk"] + _dot(c["tk"].astype(BF16), x_bd)
        c["pq"] = _dot(tk.astype(BF16),
                       jnp.concatenate([stack(c.pop("at")), stack(c["akv_y0"][:cs])],
                                       axis=1).astype(BF16))
        del c["tk"]

    def s_u(c):
        c["st"] = st_ref[c["j"]]
        pq = c.pop("pq")
        ps = _dot(jnp.concatenate([pq[:, :LANES].astype(BF16), c.pop("rt_b")], axis=0),
                  stack(c["st"]).astype(BF16))
        c["u_b"] = stack(ps[:cs] + pq[:, LANES:]).astype(BF16)
        c["rs"] = ps[cs:]

    def s_y(c):
        au = _dot(jnp.concatenate([c.pop("a_rb"), c.pop("bt")], axis=0), c.pop("u_b"))
        akv_y0 = c.pop("akv_y0")
        c["y"] = c.pop("rs") + au[:cs] + akv_y0[cs:2 * cs]
        st_ref[c["j"]] = c.pop("wm") * c.pop("st") + au[cs:] + akv_y0[2 * cs:]

    stages = [s_cumsum, s_operands, s_intra] + [s_square] * (n_sq - 1) + [s_solve, s_u, s_y]
    gsz = len(cstate) // SCAN_GROUPS
    groups = [cstate[g * gsz:(g + 1) * gsz] for g in range(SCAN_GROUPS)]
    for t in range(len(stages) + SCAN_LAG * (SCAN_GROUPS - 1)):
        for g, grp in enumerate(groups):
            k = t - g * SCAN_LAG
            if 0 <= k < len(stages):
                for c in grp:
                    stages[k](c)

    rk_sum = _dot(jnp.concatenate([c["rkr"] for c in cstate], axis=0), head_ones)
    y_all = jnp.concatenate([c["y"] for c in cstate], axis=0)
    d_all = y_all - _dot(y_all.astype(BF16), head_mean)
    var_all = _dot((d_all * d_all).astype(BF16), head_mean)
    yn_all = d_all * lax.rsqrt(var_all + GN_EPS)
    for c in cstate:
        b, q, j = c["b"], c["q"], c["j"]
        lanes = slice(q * LANES, (q + 1) * LANES)
        rows = slice(j * cs, (j + 1) * cs)
        y_ref[b, q] = (yn_all[rows] * gng_ref[:, lanes] + gnb_ref[:, lanes]
                       + rk_sum[rows] * v_ref[b, q]).astype(y_ref.dtype)


def _scan(r, lw, k, v, a, b, rk, gng, gnb):
    bsz, pp, seq, _ = r.shape
    blk = pl.BlockSpec((bsz, pp, SCAN_CHUNK, LANES), lambda s: (0, 0, s, 0))
    vec = pl.BlockSpec((1, pp * LANES), lambda s: (0, 0))
    return pl.pallas_call(
        _scan_kernel,
        grid=(seq // SCAN_CHUNK,),
        in_specs=[blk] * 6 + [vec] * 3,
        out_specs=blk,
        out_shape=jax.ShapeDtypeStruct((bsz, pp, seq, LANES), BF16),
        scratch_shapes=[pltpu.VMEM((bsz * pp, SCAN_CHUNK, LANES), F32)],
        compiler_params=pltpu.CompilerParams(
            dimension_semantics=("arbitrary",), vmem_limit_bytes=VMEM_LIMIT),
        name="scan",
    )(r, lw, k, v, a, b, rk, gng, gnb)


def _post_kernel(x_ref, mod_ref, yb_ref, g_ref, gb_ref, ma_ref, wbb_ref, wout_ref, bout_ref,
                 ln1g_ref, ln1b_ref, w1_ref, b1_ref, w2_ref, b2_ref, ln2g_ref, ln2b_ref, o_ref):
    tl = x_ref.shape[1]
    gt1 = mod_ref[0, 2:3, :]
    sh2 = mod_ref[0, 3:4, :]
    sc2 = mod_ref[0, 4:5, :]
    gt2 = mod_ref[0, 5:6, :]
    groups = [slice(i * tl // POST_GROUPS, (i + 1) * tl // POST_GROUPS) for i in range(POST_GROUPS)]
    pb = []
    for rows in groups:
        yb = (jnp.concatenate([yb_ref[0, q, rows, :] for q in range(PAIRS)], axis=1).astype(F32)
              * g_ref[0, rows, :].astype(F32))
        pb.append(_dot(yb.astype(BF16), wbb_ref[...]))
    h1 = []
    for i, rows in enumerate(groups):
        merged = ma_ref[0, rows, :].astype(F32) + gb_ref[0, rows, :].astype(F32) * pb[i]
        mix = _dot(merged.astype(BF16), wout_ref[...]) + bout_ref[...]
        h1.append(_layer_norm(ALPHA * x_ref[0, rows, :] + gt1 * mix,
                              ln1g_ref[...], ln1b_ref[...], LN_EPS))
    t = []
    for i in range(POST_GROUPS):
        h = (h1[i] * (1.0 + sc2) + sh2).astype(BF16)
        t.append(jnp.maximum(_dot(h, w1_ref[...]) + b1_ref[...], 0.0))
    for i, rows in enumerate(groups):
        ff = _dot((t[i] * t[i]).astype(BF16), w2_ref[...]) + b2_ref[...]
        o_ref[0, rows, :] = _layer_norm(ALPHA * h1[i] + gt2 * ff,
                                        ln2g_ref[...], ln2b_ref[...], LN_EPS).astype(o_ref.dtype)


def _post(x, mod, yb, g, gb, ma, p, tl, out_dtype):
    bsz, seq, _ = x.shape
    row = lambda b, s: (b, s, 0)
    consts = [p["wbb"], p["wout"], p["bout"], p["ln1g"], p["ln1b"],
              p["w1"], p["b1"], p["w2"], p["b2"], p["ln2g"], p["ln2b"]]
    return pl.pallas_call(
        _post_kernel,
        grid=(bsz, seq // tl),
        in_specs=[pl.BlockSpec((1, tl, D_MODEL), row),
                  pl.BlockSpec((1, 6, D_MODEL), lambda b, s: (b, 0, 0)),
                  pl.BlockSpec((1, PAIRS, tl, LANES), lambda b, s: (b, 0, s, 0)),
                  pl.BlockSpec((1, tl, R_WIDTH), row),
                  pl.BlockSpec((1, tl, D_MODEL), row),
                  pl.BlockSpec((1, tl, D_MODEL), row)]
                 + [_const_spec(c.shape) for c in consts],
        out_specs=pl.BlockSpec((1, tl, D_MODEL), row),
        out_shape=jax.ShapeDtypeStruct((bsz, seq, D_MODEL), out_dtype),
        compiler_params=pltpu.CompilerParams(
            dimension_semantics=("arbitrary", "arbitrary"), vmem_limit_bytes=VMEM_LIMIT),
        name="post",
    )(x, mod, yb, g, gb, ma, *consts)


def _relayout_w_in_kernel(w_ref, o_ref):
    split = 2 * G_WIDTH + 3 * R_WIDTH + LORA_COLS
    rows = w_ref.shape[0]
    o_ref[:, :split] = w_ref[:, :split].astype(BF16)
    o_ref[:, split:COLS_R.stop] = jnp.zeros((rows, COLS_R.stop - split), BF16)
    o_ref[:, COLS_R.stop:] = w_ref[:, split:].astype(BF16)


def _relayout_w_in(w_in, l):
    n_in = w_in.shape[2]
    tr = D_MODEL // 8
    return pl.pallas_call(
        _relayout_w_in_kernel,
        grid=(D_MODEL // tr,),
        in_specs=[pl.BlockSpec((None, tr, n_in), lambda i: (l, i, 0))],
        out_specs=pl.BlockSpec((tr, COLS_GB.stop), lambda i: (i, 0)),
        out_shape=jax.ShapeDtypeStruct((D_MODEL, COLS_GB.stop), BF16),
        name="wprep",
    )(w_in)


def _prepare_params(l, w_in, b_in, g_ln_v, b_ln_v, w_spatial, b_spatial, mu_shift, w0,
                    w_decay_up, a0, w_aaa_up, w_gate_up, k_k, k_a, r_k, gn_gain, gn_bias,
                    w_branch_a, w_branch_b, w_out, b_out, ln1_g, ln1_b, w_ff1, b_ff1, w_ff2,
                    b_ff2, ln2_g, ln2_b):
    g_end = 2 * G_WIDTH
    rkv_end = g_end + 3 * R_WIDTH
    r_end = rkv_end + LORA_COLS
    pad = LORA_PAD - LORA_COLS
    row2 = lambda t: t.reshape(1, -1)
    wi, bi = w_in[l], b_in[l]
    b_all = jnp.concatenate([bi[:r_end], jnp.zeros((pad,), F32), bi[r_end:]])
    mu = jnp.concatenate([mu_shift[l], jnp.zeros((pad,), F32)])
    wup = jnp.zeros((LORA_PAD, 3 * R_WIDTH), F32)
    wup = wup.at[0:DECAY_LORA, 0:R_WIDTH].set(w_decay_up[l])
    wup = wup.at[DECAY_LORA:DECAY_LORA + AAA_LORA, R_WIDTH:2 * R_WIDTH].set(w_aaa_up[l])
    wup = wup.at[DECAY_LORA + AAA_LORA:LORA_COLS, 2 * R_WIDTH:].set(w_gate_up[l])
    ws = w_spatial[l].reshape(PAIRS, 2, CHUNK, CHUNK).transpose(0, 2, 1, 3).reshape(PAIRS, CHUNK, 2 * CHUNK)
    bs = jnp.repeat(b_spatial[l].T, G_WIDTH // G_GROUPS, axis=1)
    hid = jnp.arange(R_WIDTH) // R_HEAD
    bd = (hid[:, None] == hid[None, :]).astype(BF16)
    return dict(
        win=_relayout_w_in(w_in, l), bin=row2(b_all),
        glnv=row2(g_ln_v[l]), blnv=row2(b_ln_v[l]), ws=ws, bs=bs, mu=row2(mu),
        w0=row2(w0[l]), a0=row2(a0[l]), wup=wup.astype(BF16), kk=row2(k_k[l]), ka=row2(k_a[l]),
        bd=bd, wba=w_branch_a[l].astype(BF16),
        rk=row2(r_k[l]), gng=row2(gn_gain[l]), gnb=row2(gn_bias[l]),
        wbb=w_branch_b[l].astype(BF16), wout=w_out[l].astype(BF16), bout=row2(b_out[l]),
        ln1g=row2(ln1_g[l]), ln1b=row2(ln1_b[l]),
        w1=w_ff1[l].astype(BF16), b1=row2(b_ff1[l]), w2=w_ff2[l].astype(BF16), b2=row2(b_ff2[l]),
        ln2g=row2(ln2_g[l]), ln2b=row2(ln2_b[l]),
    )


def _tile(seq, want):
    t = min(want, seq)
    while seq % t:
        t //= 2
    return t


def kernel(x, c, w_ada, b_ada, w_in, b_in, g_ln_v, b_ln_v, w_spatial, b_spatial, mu_shift, w0, w_decay_up, a0, w_aaa_up, w_gate_up, k_k, k_a, r_k, gn_gain, gn_bias, w_branch_a, w_branch_b, w_out, b_out, ln1_g, ln1_b, w_ff1, b_ff1, w_ff2, b_ff2, ln2_g, ln2_b):
    bsz, seq, _ = x.shape
    assert seq % CHUNK == 0 and x.shape[2] == D_MODEL
    out_dtype = x.dtype
    h_res = x.astype(F32)
    tl = _tile(seq, 512)
    for l in range(DEPTH):
        p = _prepare_params(l, w_in, b_in, g_ln_v, b_ln_v, w_spatial, b_spatial, mu_shift, w0,
                            w_decay_up, a0, w_aaa_up, w_gate_up, k_k, k_a, r_k, gn_gain, gn_bias,
                            w_branch_a, w_branch_b, w_out, b_out, ln1_g, ln1_b, w_ff1, b_ff1,
                            w_ff2, b_ff2, ln2_g, ln2_b)
        mod = _modulation(c.astype(F32), w_ada, b_ada, l).reshape(bsz, 6, D_MODEL)
        ma, gb, r, lw, k, v, a, b, g = _inproj(h_res, mod, p, tl)
        yb = _scan(r, lw, k, v, a, b, p["rk"], p["gng"], p["gnb"])
        h_res = _post(h_res, mod, yb, g, gb, ma, p, tl, F32)
    return h_res.astype(out_dtype)
```

```python
import math

import jax
import jax.numpy as jnp
from jax import lax
from jax.experimental import pallas as pl
from jax.experimental.pallas import tpu as pltpu

D_MODEL = 1024
G_GROUPS = 8
G_WIDTH = 512
CHUNK = 128
R_WIDTH = 512
R_HEAD = 64
R_HEADS = R_WIDTH // R_HEAD
DECAY_LORA = 32
AAA_LORA = 32
GATE_LORA = 96
LORA_COLS = DECAY_LORA + AAA_LORA + GATE_LORA
D_FF = 4 * D_MODEL
DEPTH = 1
ALPHA = (2.0 * DEPTH) ** 0.25
LN_EPS = 1e-5
GN_EPS = 64e-5
DECAY_SCALE = math.exp(-0.5)

LANES = 128
PAIRS = R_WIDTH // LANES
LORA_PAD = 2 * LANES
R_COLS = 3 * R_WIDTH + LORA_PAD
COLS_U = slice(0, G_WIDTH)
COLS_V = slice(G_WIDTH, 2 * G_WIDTH)
COLS_R = slice(2 * G_WIDTH, 2 * G_WIDTH + R_COLS)
COLS_GA = slice(COLS_R.stop, COLS_R.stop + D_MODEL)
COLS_GB = slice(COLS_GA.stop, COLS_GA.stop + D_MODEL)
SCAN_CHUNK = 64
STACK = 2 * SCAN_CHUNK
POST_GROUPS = 2
IN_GROUPS = 2
I_R, I_LW, I_K, I_V, I_A, I_B = range(6)
SCAN_GROUPS = 1
SCAN_LAG = 3
VMEM_LIMIT = 56 * 1024 * 1024

F32 = jnp.float32
BF16 = jnp.bfloat16


def _dot(a, b):
    return jnp.dot(a, b, preferred_element_type=F32)


def _sigmoid(x):
    return 1.0 / (1.0 + jnp.exp(-x))


def _gelu_tanh(x):
    c = 0.7978845608028654
    hx = 0.5 * x
    return hx + hx * jnp.tanh(x * (c + (c * 0.044715) * (x * x)))


def _layer_norm(x, g, b, eps):
    mu = jnp.mean(x, axis=-1, keepdims=True)
    d = x - mu
    var = jnp.mean(d * d, axis=-1, keepdims=True)
    return d * lax.rsqrt(var + eps) * g + b


def _const_spec(shape):
    n = len(shape)
    return pl.BlockSpec(shape, lambda *_: (0,) * n)


def _mod_kernel(c_ref, w_ref, b_ref, o_ref):
    c = c_ref[...]
    c_act = c * _sigmoid(c)
    o_ref[...] = jnp.dot(c_act, w_ref[...], preferred_element_type=F32,
                         precision=lax.Precision.HIGHEST) + b_ref[...]


def _modulation(c, w_ada, b_ada, l):
    bsz = c.shape[0]
    n = w_ada.shape[2]
    tn = D_MODEL
    return pl.pallas_call(
        _mod_kernel,
        grid=(n // tn,),
        in_specs=[pl.BlockSpec((bsz, D_MODEL), lambda j: (0, 0)),
                  pl.BlockSpec((None, D_MODEL, tn), lambda j: (l, 0, j)),
                  pl.BlockSpec((None, 1, tn), lambda j: (l, 0, j))],
        out_specs=pl.BlockSpec((bsz, tn), lambda j: (0, j)),
        out_shape=jax.ShapeDtypeStruct((bsz, n), F32),
        name="mod",
    )(c, w_ada, b_ada.reshape(b_ada.shape[0], 1, n))


def _inproj_kernel(x_ref, mod_ref, win_ref, bin_ref,
                   glnv_ref, blnv_ref, ws_ref, bs_ref, mu_ref, w0_ref, a0_ref, wup_ref,
                   kk_ref, ka_ref, bd_ref, wba_ref,
                   ma_ref, gb_ref, sc_ref, g_ref,
                   zsh_ref):
    tl = x_ref.shape[1]

    @pl.when(pl.program_id(1) == 0)
    def _():
        zsh_ref[0:8, :] = jnp.zeros((8, R_COLS), F32)

    sh1 = mod_ref[0, 0:1, :]
    sc1 = mod_ref[0, 1:2, :]
    gsz = tl // IN_GROUPS
    groups = [slice(i * gsz, (i + 1) * gsz) for i in range(IN_GROUPS)]
    gi = range(IN_GROUPS)
    h = [(x_ref[0, rows, :] * (1.0 + sc1) + sh1).astype(BF16) for rows in groups]
    def proj(i, cols):
        return _dot(h[i], win_ref[:, cols]) + bin_ref[:, cols]

    zr = [proj(i, COLS_R) for i in gi]
    for i, rows in enumerate(groups):
        zsh_ref[8 + rows.start:8 + rows.stop, :] = zr[i]
    zu = [proj(i, COLS_U) for i in gi]
    zv = [proj(i, COLS_V) for i in gi]

    z = []
    for i, rows in enumerate(groups):
        prev = zsh_ref[7 + rows.start:7 + rows.stop, :]
        z.append(zr[i] + (prev - zr[i]) * mu_ref[...])
    zsh_ref[7:8, :] = zr[-1][gsz - 1:gsz, :]
    llane = lax.broadcasted_iota(jnp.int32, (gsz, LORA_PAD), 1)
    lin = []
    for i in gi:
        xl = z[i][:, 3 * R_WIDTH:]
        lin.append(jnp.where(llane < DECAY_LORA, jnp.tanh(xl),
                             jnp.where(llane < DECAY_LORA + AAA_LORA, xl, _sigmoid(xl))).astype(BF16))
    zgb = [proj(i, COLS_GB) for i in gi]
    up = [_dot(lin[i], wup_ref[...]) for i in gi]

    u = [_gelu_tanh(zu[i]) for i in gi]
    v = [_layer_norm(_gelu_tanh(zv[i]), glnv_ref[...], blnv_ref[...], LN_EPS) for i in gi]

    kk, n2 = [], []
    for i, rows in enumerate(groups):
        lw = -DECAY_SCALE * _sigmoid(w0_ref[...] + up[i][:, 0:R_WIDTH])
        g_ref[0, rows, :] = up[i][:, 2 * R_WIDTH:].astype(g_ref.dtype)
        kk.append(z[i][:, R_WIDTH:2 * R_WIDTH] * kk_ref[...])
        n2.append(_dot((kk[i] * kk[i]).astype(BF16), bd_ref[...]))
        for q in range(PAIRS):
            sl = slice(q * LANES, (q + 1) * LANES)
            sc_ref[0, I_LW, q, rows, :] = lw[:, sl]
            sc_ref[0, I_R, q, rows, :] = z[i][:, sl]
            sc_ref[0, I_V, q, rows, :] = z[i][:, 2 * R_WIDTH + q * LANES:2 * R_WIDTH + (q + 1) * LANES]

    lane = lax.broadcasted_iota(jnp.int32, (CHUNK, LANES), 1)
    first_head = lane < R_HEAD
    trow = lax.broadcasted_iota(jnp.int32, (CHUNK, 2 * CHUNK), 0)
    scol = lax.broadcasted_iota(jnp.int32, (CHUNK, 2 * CHUNK), 1) % CHUNK
    causal = trow >= scol
    ws = [jnp.where(causal, ws_ref[q], 0.0).astype(BF16) for q in range(PAIRS)]
    ya = []
    for i in gi:
        ya_rows = []
        for c in range(gsz // CHUNK):
            vc = v[i][c * CHUNK:(c + 1) * CHUNK, :]
            s_parts = []
            for q in range(PAIRS):
                v2 = vc[:, q * LANES:(q + 1) * LANES]
                vm = jnp.concatenate([jnp.where(first_head, v2, 0.0),
                                      jnp.where(first_head, 0.0, v2)], axis=0).astype(BF16)
                s_parts.append(_dot(ws[q], vm))
            s = jnp.concatenate(s_parts, axis=1) + bs_ref[...]
            ya_rows.append(u[i][c * CHUNK:(c + 1) * CHUNK, :] * s)
        ya.append(jnp.concatenate(ya_rows, axis=0).astype(BF16))
    zga = [proj(i, COLS_GA) for i in gi]
    pa = [_dot(ya[i], wba_ref[...]) for i in gi]

    for i, rows in enumerate(groups):
        gb_ref[0, rows, :] = _sigmoid(zgb[i]).astype(gb_ref.dtype)
        a = _sigmoid(a0_ref[...] + up[i][:, R_WIDTH:2 * R_WIDTH])
        kkn = kk[i] * lax.rsqrt(jnp.maximum(n2[i], 1e-24))
        k2 = z[i][:, R_WIDTH:2 * R_WIDTH] * (1.0 + (a - 1.0) * ka_ref[...])
        nkk = -kkn
        kka = kkn * a
        for q in range(PAIRS):
            sl = slice(q * LANES, (q + 1) * LANES)
            sc_ref[0, I_K, q, rows, :] = k2[:, sl]
            sc_ref[0, I_A, q, rows, :] = nkk[:, sl]
            sc_ref[0, I_B, q, rows, :] = kka[:, sl]
        ma_ref[0, rows, :] = (_sigmoid(zga[i]) * pa[i]).astype(ma_ref.dtype)


def _inproj(x, mod, p, tl):
    bsz, seq, _ = x.shape
    grid = (bsz, seq // tl)
    row = lambda b, s: (b, s, 0)
    sc_shape = jax.ShapeDtypeStruct((bsz, 6, PAIRS, seq, LANES), F32)
    sc_spec = pl.BlockSpec((1, 6, PAIRS, tl, LANES), lambda b, s: (b, 0, 0, s, 0))
    consts = [p["win"], p["bin"], p["glnv"], p["blnv"],
              p["ws"], p["bs"], p["mu"], p["w0"], p["a0"], p["wup"], p["kk"], p["ka"], p["bd"],
              p["wba"]]
    return pl.pallas_call(
        _inproj_kernel,
        grid=grid,
        in_specs=[pl.BlockSpec((1, tl, D_MODEL), row),
                  pl.BlockSpec((1, 6, D_MODEL), lambda b, s: (b, 0, 0))]
                 + [_const_spec(c.shape) for c in consts],
        out_specs=[pl.BlockSpec((1, tl, D_MODEL), row),
                   pl.BlockSpec((1, tl, D_MODEL), row)]
                  + [sc_spec]
                  + [pl.BlockSpec((1, tl, R_WIDTH), row)],
        out_shape=[jax.ShapeDtypeStruct((bsz, seq, D_MODEL), BF16),
                   jax.ShapeDtypeStruct((bsz, seq, D_MODEL), BF16)]
                  + [sc_shape]
                  + [jax.ShapeDtypeStruct((bsz, seq, R_WIDTH), BF16)],
        scratch_shapes=[pltpu.VMEM((tl + 8, R_COLS), F32)],
        compiler_params=pltpu.CompilerParams(
            dimension_semantics=("arbitrary", "arbitrary"), vmem_limit_bytes=VMEM_LIMIT),
        name="inproj",
    )(x, mod, *consts)


def _scan_kernel(sc_ref, rk_ref, gng_ref, gnb_ref, y_ref, st_ref):
    bt, _, pp, cs, _ = sc_ref.shape

    @pl.when(pl.program_id(0) == 0)
    def _():
        st_ref[...] = jnp.zeros(st_ref.shape, F32)

    lane = lax.broadcasted_iota(jnp.int32, (cs, LANES), 1)
    first_head = lane < R_HEAD

    def stack(t):
        return jnp.concatenate([jnp.where(first_head, t, 0.0),
                                jnp.where(first_head, 0.0, t)], axis=0)

    def fold(t):
        return jnp.where(first_head, t[:cs], t[cs:])

    ri = lax.broadcasted_iota(jnp.int32, (cs, cs), 0)
    ci = lax.broadcasted_iota(jnp.int32, (cs, cs), 1)
    tri_ones = jnp.where(ri >= ci, 1.0, 0.0).astype(BF16)
    si = lax.broadcasted_iota(jnp.int32, (cs, LANES), 0)
    sj = lane % cs
    strict = si > sj
    incl = si >= sj
    eye = jnp.where(si == sj, 1.0, 0.0).astype(F32)
    bsi = lax.broadcasted_iota(jnp.int32, (LANES, LANES), 0) // R_HEAD
    bsj = lax.broadcasted_iota(jnp.int32, (LANES, LANES), 1) // R_HEAD
    head_ones = jnp.where(bsi == bsj, 1.0, 0.0).astype(BF16)
    head_mean = jnp.where(bsi == bsj, 1.0 / R_HEAD, 0.0).astype(BF16)

    chains = [(b, q) for q in range(pp) for b in range(bt)]
    n_sq = (cs - 1).bit_length() - 1
    cstate = [dict(j=j, b=b, q=q) for j, (b, q) in enumerate(chains)]

    def s_cumsum(c):
        lw = sc_ref[c["b"], I_LW, c["q"]]
        p1 = lw.astype(BF16)
        e1 = lw - p1.astype(F32)
        p2 = e1.astype(BF16)
        p3 = (e1 - p2.astype(F32)).astype(BF16)
        c["lw"] = lw
        c["cum"] = _dot(tri_ones, p1) + _dot(tri_ones, p2) + _dot(tri_ones, p3)

    def s_operands(c):
        b, q, cum, lw = c["b"], c["q"], c["cum"], c["lw"]
        last = cum[cs - 1:cs, :]
        e_pos = jnp.exp(cum)
        e_pos_ex = jnp.exp(cum - lw)
        e_neg = jnp.exp(-cum)
        wcol = jnp.broadcast_to(jnp.exp(last), (LANES, LANES)).T
        r = sc_ref[b, I_R, q]
        k = sc_ref[b, I_K, q]
        bb = sc_ref[b, I_B, q]
        c["at"] = sc_ref[b, I_A, q] * e_pos_ex
        c["rt_b"] = (r * e_pos).astype(BF16)
        tb = stack(bb * e_neg).T
        tk = stack(k * e_neg).T
        c["vm"] = stack(sc_ref[b, I_V, q]).astype(BF16)
        c["bt"] = fold(tb * wcol).astype(BF16)
        c["kt"] = fold(tk * wcol)
        c["wm"] = fold(wcol)
        c["aa"] = _dot(jnp.concatenate([c["at"].astype(BF16), c["rt_b"]], axis=0),
                       jnp.concatenate([tb, tk], axis=1).astype(BF16))
        lanes = slice(q * LANES, (q + 1) * LANES)
        c["rkr"] = (r * k * rk_ref[:, lanes]).astype(BF16)

    def s_intra(c):
        aa = c.pop("aa")
        a_ab = jnp.where(strict, aa[:cs, :LANES], 0.0)
        c["a_rb"] = jnp.where(incl, aa[cs:, :LANES], 0.0).astype(BF16)
        c["akv_y0"] = _dot(jnp.concatenate([jnp.where(strict, aa[:cs, LANES:], 0.0),
                                            jnp.where(incl, aa[cs:, LANES:], 0.0),
                                            c.pop("kt")], axis=0).astype(BF16), c.pop("vm"))
        c["xk"] = _dot(a_ab.astype(BF16), stack(a_ab).astype(BF16))
        c["tk"] = eye + a_ab

    def s_square(c):
        x_bd = stack(c["xk"]).astype(BF16)
        xt = _dot(jnp.concatenate([c["xk"], c["tk"]], axis=0).astype(BF16), x_bd)
        c["xk"] = xt[:cs]
        c["tk"] = c["tk"] + xt[cs:]

    def s_solve(c):
        x_bd = stack(c.pop("xk")).astype(BF16)
        tk = c["tk"] + _dot(c["tk"].astype(BF16), x_bd)
        c["pq"] = _dot(tk.astype(BF16),
                       jnp.concatenate([stack(c.pop("at")), stack(c["akv_y0"][:cs])],
                                       axis=1).astype(BF16))
        del c["tk"]

    def s_u(c):
        c["st"] = st_ref[c["j"]]
        pq = c.pop("pq")
        ps = _dot(jnp.concatenate([pq[:, :LANES].astype(BF16), c.pop("rt_b")], axis=0),
                  stack(c["st"]).astype(BF16))
        c["u_b"] = stack(ps[:cs] + pq[:, LANES:]).astype(BF16)
        c["rs"] = ps[cs:]

    def s_y(c):
        au = _dot(jnp.concatenate([c.pop("a_rb"), c.pop("bt")], axis=0), c.pop("u_b"))
        akv_y0 = c.pop("akv_y0")
        c["y"] = c.pop("rs") + au[:cs] + akv_y0[cs:2 * cs]
        st_ref[c["j"]] = c.pop("wm") * c.pop("st") + au[cs:] + akv_y0[2 * cs:]

    stages = [s_cumsum, s_operands, s_intra] + [s_square] * (n_sq - 1) + [s_solve, s_u, s_y]
    gsz = len(cstate) // SCAN_GROUPS
    groups = [cstate[g * gsz:(g + 1) * gsz] for g in range(SCAN_GROUPS)]
    for t in range(len(stages) + SCAN_LAG * (SCAN_GROUPS - 1)):
        for g, grp in enumerate(groups):
            k = t - g * SCAN_LAG
            if 0 <= k < len(stages):
                for c in grp:
                    stages[k](c)

    rk_sum = _dot(jnp.concatenate([c["rkr"] for c in cstate], axis=0), head_ones)
    y_all = jnp.concatenate([c["y"] for c in cstate], axis=0)
    d_all = y_all - _dot(y_all.astype(BF16), head_mean)
    var_all = _dot((d_all * d_all).astype(BF16), head_mean)
    yn_all = d_all * lax.rsqrt(var_all + GN_EPS)
    for c in cstate:
        b, q, j = c["b"], c["q"], c["j"]
        lanes = slice(q * LANES, (q + 1) * LANES)
        rows = slice(j * cs, (j + 1) * cs)
        y_ref[b, q] = (yn_all[rows] * gng_ref[:, lanes] + gnb_ref[:, lanes]
                       + rk_sum[rows] * sc_ref[b, I_V, q]).astype(y_ref.dtype)


def _scan(sc, rk, gng, gnb):
    bsz, _, pp, seq, _ = sc.shape
    blk = pl.BlockSpec((bsz, pp, SCAN_CHUNK, LANES), lambda s: (0, 0, s, 0))
    vec = pl.BlockSpec((1, pp * LANES), lambda s: (0, 0))
    return pl.pallas_call(
        _scan_kernel,
        grid=(seq // SCAN_CHUNK,),
        in_specs=[pl.BlockSpec((bsz, 6, pp, SCAN_CHUNK, LANES), lambda s: (0, 0, 0, s, 0))] + [vec] * 3,
        out_specs=blk,
        out_shape=jax.ShapeDtypeStruct((bsz, pp, seq, LANES), BF16),
        scratch_shapes=[pltpu.VMEM((bsz * pp, SCAN_CHUNK, LANES), F32)],
        compiler_params=pltpu.CompilerParams(
            dimension_semantics=("arbitrary",), vmem_limit_bytes=VMEM_LIMIT),
        name="scan",
    )(sc, rk, gng, gnb)


def _post_kernel(x_ref, mod_ref, yb_ref, g_ref, gb_ref, ma_ref, wbb_ref, wout_ref, bout_ref,
                 ln1g_ref, ln1b_ref, w1_ref, b1_ref, w2_ref, b2_ref, ln2g_ref, ln2b_ref, o_ref):
    tl = x_ref.shape[1]
    gt1 = mod_ref[0, 2:3, :]
    sh2 = mod_ref[0, 3:4, :]
    sc2 = mod_ref[0, 4:5, :]
    gt2 = mod_ref[0, 5:6, :]
    groups = [slice(i * tl // POST_GROUPS, (i + 1) * tl // POST_GROUPS) for i in range(POST_GROUPS)]
    pb = []
    for rows in groups:
        yb = (jnp.concatenate([yb_ref[0, q, rows, :] for q in range(PAIRS)], axis=1).astype(F32)
              * g_ref[0, rows, :].astype(F32))
        pb.append(_dot(yb.astype(BF16), wbb_ref[...]))
    h1 = []
    for i, rows in enumerate(groups):
        merged = ma_ref[0, rows, :].astype(F32) + gb_ref[0, rows, :].astype(F32) * pb[i]
        mix = _dot(merged.astype(BF16), wout_ref[...]) + bout_ref[...]
        h1.append(_layer_norm(ALPHA * x_ref[0, rows, :] + gt1 * mix,
                              ln1g_ref[...], ln1b_ref[...], LN_EPS))
    t = []
    for i in range(POST_GROUPS):
        h = (h1[i] * (1.0 + sc2) + sh2).astype(BF16)
        t.append(jnp.maximum(_dot(h, w1_ref[...]) + b1_ref[...], 0.0))
    for i, rows in enumerate(groups):
        ff = _dot((t[i] * t[i]).astype(BF16), w2_ref[...]) + b2_ref[...]
        o_ref[0, rows, :] = _layer_norm(ALPHA * h1[i] + gt2 * ff,
                                        ln2g_ref[...], ln2b_ref[...], LN_EPS).astype(o_ref.dtype)


def _post(x, mod, yb, g, gb, ma, p, tl, out_dtype):
    bsz, seq, _ = x.shape
    row = lambda b, s: (b, s, 0)
    consts = [p["wbb"], p["wout"], p["bout"], p["ln1g"], p["ln1b"],
              p["w1"], p["b1"], p["w2"], p["b2"], p["ln2g"], p["ln2b"]]
    return pl.pallas_call(
        _post_kernel,
        grid=(bsz, seq // tl),
        in_specs=[pl.BlockSpec((1, tl, D_MODEL), row),
                  pl.BlockSpec((1, 6, D_MODEL), lambda b, s: (b, 0, 0)),
                  pl.BlockSpec((1, PAIRS, tl, LANES), lambda b, s: (b, 0, s, 0)),
                  pl.BlockSpec((1, tl, R_WIDTH), row),
                  pl.BlockSpec((1, tl, D_MODEL), row),
                  pl.BlockSpec((1, tl, D_MODEL), row)]
                 + [_const_spec(c.shape) for c in consts],
        out_specs=pl.BlockSpec((1, tl, D_MODEL), row),
        out_shape=jax.ShapeDtypeStruct((bsz, seq, D_MODEL), out_dtype),
        compiler_params=pltpu.CompilerParams(
            dimension_semantics=("arbitrary", "arbitrary"), vmem_limit_bytes=VMEM_LIMIT),
        name="post",
    )(x, mod, yb, g, gb, ma, *consts)


def _relayout_w_in_kernel(w_ref, o_ref):
    split = 2 * G_WIDTH + 3 * R_WIDTH + LORA_COLS
    rows = w_ref.shape[0]
    o_ref[:, :split] = w_ref[:, :split].astype(BF16)
    o_ref[:, split:COLS_R.stop] = jnp.zeros((rows, COLS_R.stop - split), BF16)
    o_ref[:, COLS_R.stop:] = w_ref[:, split:].astype(BF16)


def _relayout_w_in(w_in, l):
    n_in = w_in.shape[2]
    tr = D_MODEL // 8
    return pl.pallas_call(
        _relayout_w_in_kernel,
        grid=(D_MODEL // tr,),
        in_specs=[pl.BlockSpec((None, tr, n_in), lambda i: (l, i, 0))],
        out_specs=pl.BlockSpec((tr, COLS_GB.stop), lambda i: (i, 0)),
        out_shape=jax.ShapeDtypeStruct((D_MODEL, COLS_GB.stop), BF16),
        name="wprep",
    )(w_in)


def _prepare_params(l, w_in, b_in, g_ln_v, b_ln_v, w_spatial, b_spatial, mu_shift, w0,
                    w_decay_up, a0, w_aaa_up, w_gate_up, k_k, k_a, r_k, gn_gain, gn_bias,
                    w_branch_a, w_branch_b, w_out, b_out, ln1_g, ln1_b, w_ff1, b_ff1, w_ff2,
                    b_ff2, ln2_g, ln2_b):
    g_end = 2 * G_WIDTH
    rkv_end = g_end + 3 * R_WIDTH
    r_end = rkv_end + LORA_COLS
    pad = LORA_PAD - LORA_COLS
    row2 = lambda t: t.reshape(1, -1)
    wi, bi = w_in[l], b_in[l]
    b_all = jnp.concatenate([bi[:r_end], jnp.zeros((pad,), F32), bi[r_end:]])
    mu = jnp.concatenate([mu_shift[l], jnp.zeros((pad,), F32)])
    wup = jnp.zeros((LORA_PAD, 3 * R_WIDTH), F32)
    wup = wup.at[0:DECAY_LORA, 0:R_WIDTH].set(w_decay_up[l])
    wup = wup.at[DECAY_LORA:DECAY_LORA + AAA_LORA, R_WIDTH:2 * R_WIDTH].set(w_aaa_up[l])
    wup = wup.at[DECAY_LORA + AAA_LORA:LORA_COLS, 2 * R_WIDTH:].set(w_gate_up[l])
    ws = w_spatial[l].reshape(PAIRS, 2, CHUNK, CHUNK).transpose(0, 2, 1, 3).reshape(PAIRS, CHUNK, 2 * CHUNK)
    bs = jnp.repeat(b_spatial[l].T, G_WIDTH // G_GROUPS, axis=1)
    hid = jnp.arange(R_WIDTH) // R_HEAD
    bd = (hid[:, None] == hid[None, :]).astype(BF16)
    return dict(
        win=_relayout_w_in(w_in, l), bin=row2(b_all),
        glnv=row2(g_ln_v[l]), blnv=row2(b_ln_v[l]), ws=ws, bs=bs, mu=row2(mu),
        w0=row2(w0[l]), a0=row2(a0[l]), wup=wup.astype(BF16), kk=row2(k_k[l]), ka=row2(k_a[l]),
        bd=bd, wba=w_branch_a[l].astype(BF16),
        rk=row2(r_k[l]), gng=row2(gn_gain[l]), gnb=row2(gn_bias[l]),
        wbb=w_branch_b[l].astype(BF16), wout=w_out[l].astype(BF16), bout=row2(b_out[l]),
        ln1g=row2(ln1_g[l]), ln1b=row2(ln1_b[l]),
        w1=w_ff1[l].astype(BF16), b1=row2(b_ff1[l]), w2=w_ff2[l].astype(BF16), b2=row2(b_ff2[l]),
        ln2g=row2(ln2_g[l]), ln2b=row2(ln2_b[l]),
    )


def _tile(seq, want):
    t = min(want, seq)
    while seq % t:
        t //= 2
    return t


def kernel(x, c, w_ada, b_ada, w_in, b_in, g_ln_v, b_ln_v, w_spatial, b_spatial, mu_shift, w0, w_decay_up, a0, w_aaa_up, w_gate_up, k_k, k_a, r_k, gn_gain, gn_bias, w_branch_a, w_branch_b, w_out, b_out, ln1_g, ln1_b, w_ff1, b_ff1, w_ff2, b_ff2, ln2_g, ln2_b):
    bsz, seq, _ = x.shape
    assert seq % CHUNK == 0 and x.shape[2] == D_MODEL
    out_dtype = x.dtype
    h_res = x.astype(F32)
    tl = _tile(seq, 512)
    for l in range(DEPTH):
        p = _prepare_params(l, w_in, b_in, g_ln_v, b_ln_v, w_spatial, b_spatial, mu_shift, w0,
                            w_decay_up, a0, w_aaa_up, w_gate_up, k_k, k_a, r_k, gn_gain, gn_bias,
                            w_branch_a, w_branch_b, w_out, b_out, ln1_g, ln1_b, w_ff1, b_ff1,
                            w_ff2, b_ff2, ln2_g, ln2_b)
        mod = _modulation(c.astype(F32), w_ada, b_ada, l).reshape(bsz, 6, D_MODEL)
        ma, gb, sc, g = _inproj(h_res, mod, p, tl)
        yb = _scan(sc, p["rk"], p["gng"], p["gnb"])
        h_res = _post(h_res, mod, yb, g, gb, ma, p, tl, F32)
    return h_res.astype(out_dtype)
```

```python
import math

import jax
import jax.numpy as jnp
from jax import lax
from jax.experimental import pallas as pl
from jax.experimental.pallas import tpu as pltpu

D_MODEL = 1024
G_GROUPS = 8
G_WIDTH = 512
CHUNK = 128
R_WIDTH = 512
R_HEAD = 64
R_HEADS = R_WIDTH // R_HEAD
DECAY_LORA = 32
AAA_LORA = 32
GATE_LORA = 96
LORA_COLS = DECAY_LORA + AAA_LORA + GATE_LORA
D_FF = 4 * D_MODEL
DEPTH = 1
ALPHA = (2.0 * DEPTH) ** 0.25
LN_EPS = 1e-5
GN_EPS = 64e-5
DECAY_SCALE = math.exp(-0.5)

LANES = 128
PAIRS = R_WIDTH // LANES
LORA_PAD = 2 * LANES
R_COLS = 3 * R_WIDTH + LORA_PAD
COLS_U = slice(0, G_WIDTH)
COLS_V = slice(G_WIDTH, 2 * G_WIDTH)
COLS_R = slice(2 * G_WIDTH, 2 * G_WIDTH + R_COLS)
COLS_GA = slice(COLS_R.stop, COLS_R.stop + D_MODEL)
COLS_GB = slice(COLS_GA.stop, COLS_GA.stop + D_MODEL)
SCAN_CHUNK = 64
STACK = 2 * SCAN_CHUNK
POST_GROUPS = 2
IN_GROUPS = 2
I_R, I_LW, I_K, I_V, I_A, I_B = range(6)
GN_BATCHES = 4
VMEM_LIMIT = 56 * 1024 * 1024

F32 = jnp.float32
BF16 = jnp.bfloat16


def _dot(a, b):
    return jnp.dot(a, b, preferred_element_type=F32)


def _sigmoid(x):
    return 1.0 / (1.0 + jnp.exp(-x))


def _gelu_tanh(x):
    c = 0.7978845608028654
    hx = 0.5 * x
    return hx + hx * jnp.tanh(x * (c + (c * 0.044715) * (x * x)))


def _layer_norm(x, g, b, eps):
    mu = jnp.mean(x, axis=-1, keepdims=True)
    d = x - mu
    var = jnp.mean(d * d, axis=-1, keepdims=True)
    return d * lax.rsqrt(var + eps) * g + b


def _const_spec(shape):
    n = len(shape)
    return pl.BlockSpec(shape, lambda *_: (0,) * n)


def _mod_kernel(c_ref, w_ref, b_ref, o_ref):
    c = c_ref[...]
    c_act = c * _sigmoid(c)
    o_ref[...] = jnp.dot(c_act, w_ref[...], preferred_element_type=F32,
                         precision=lax.Precision.HIGHEST) + b_ref[...]


def _modulation(c, w_ada, b_ada, l):
    bsz = c.shape[0]
    n = w_ada.shape[2]
    tn = D_MODEL
    return pl.pallas_call(
        _mod_kernel,
        grid=(n // tn,),
        in_specs=[pl.BlockSpec((bsz, D_MODEL), lambda j: (0, 0)),
                  pl.BlockSpec((None, D_MODEL, tn), lambda j: (l, 0, j)),
                  pl.BlockSpec((None, 1, tn), lambda j: (l, 0, j))],
        out_specs=pl.BlockSpec((bsz, tn), lambda j: (0, j)),
        out_shape=jax.ShapeDtypeStruct((bsz, n), F32),
        name="mod",
    )(c, w_ada, b_ada.reshape(b_ada.shape[0], 1, n))


def _inproj_kernel(x_ref, mod_ref, win_ref, bin_ref,
                   glnv_ref, blnv_ref, ws_ref, bs_ref, mu_ref, w0_ref, a0_ref, wup_ref,
                   kk_ref, ka_ref, bd_ref, wba_ref,
                   ma_ref, gb_ref, sc_ref, g_ref,
                   zsh_ref):
    tl = x_ref.shape[1]

    @pl.when(pl.program_id(1) == 0)
    def _():
        zsh_ref[0:8, :] = jnp.zeros((8, R_COLS), F32)

    sh1 = mod_ref[0, 0:1, :]
    sc1 = mod_ref[0, 1:2, :]
    gsz = tl // IN_GROUPS
    groups = [slice(i * gsz, (i + 1) * gsz) for i in range(IN_GROUPS)]
    gi = range(IN_GROUPS)
    h = [(x_ref[0, rows, :] * (1.0 + sc1) + sh1).astype(BF16) for rows in groups]
    def proj(i, cols):
        return _dot(h[i], win_ref[:, cols]) + bin_ref[:, cols]

    zr = [proj(i, COLS_R) for i in gi]
    for i, rows in enumerate(groups):
        zsh_ref[8 + rows.start:8 + rows.stop, :] = zr[i]
    zu = [proj(i, COLS_U) for i in gi]
    zv = [proj(i, COLS_V) for i in gi]

    z = []
    for i, rows in enumerate(groups):
        prev = zsh_ref[7 + rows.start:7 + rows.stop, :]
        z.append(zr[i] + (prev - zr[i]) * mu_ref[...])
    zsh_ref[7:8, :] = zr[-1][gsz - 1:gsz, :]
    llane = lax.broadcasted_iota(jnp.int32, (gsz, LORA_PAD), 1)
    lin = []
    for i in gi:
        xl = z[i][:, 3 * R_WIDTH:]
        lin.append(jnp.where(llane < DECAY_LORA, jnp.tanh(xl),
                             jnp.where(llane < DECAY_LORA + AAA_LORA, xl, _sigmoid(xl))).astype(BF16))
    zgb = [proj(i, COLS_GB) for i in gi]
    up = [_dot(lin[i], wup_ref[...]) for i in gi]

    u = [_gelu_tanh(zu[i]) for i in gi]
    v = [_layer_norm(_gelu_tanh(zv[i]), glnv_ref[...], blnv_ref[...], LN_EPS) for i in gi]

    kk, n2 = [], []
    for i, rows in enumerate(groups):
        lw = -DECAY_SCALE * _sigmoid(w0_ref[...] + up[i][:, 0:R_WIDTH])
        g_ref[0, rows, :] = up[i][:, 2 * R_WIDTH:].astype(g_ref.dtype)
        kk.append(z[i][:, R_WIDTH:2 * R_WIDTH] * kk_ref[...])
        n2.append(_dot((kk[i] * kk[i]).astype(BF16), bd_ref[...]))
        for q in range(PAIRS):
            sl = slice(q * LANES, (q + 1) * LANES)
            sc_ref[0, I_LW, q, rows, :] = lw[:, sl]
            sc_ref[0, I_R, q, rows, :] = z[i][:, sl]
            sc_ref[0, I_V, q, rows, :] = z[i][:, 2 * R_WIDTH + q * LANES:2 * R_WIDTH + (q + 1) * LANES]

    lane = lax.broadcasted_iota(jnp.int32, (CHUNK, LANES), 1)
    first_head = lane < R_HEAD
    trow = lax.broadcasted_iota(jnp.int32, (CHUNK, 2 * CHUNK), 0)
    scol = lax.broadcasted_iota(jnp.int32, (CHUNK, 2 * CHUNK), 1) % CHUNK
    causal = trow >= scol
    ws = [jnp.where(causal, ws_ref[q], 0.0).astype(BF16) for q in range(PAIRS)]
    ya = []
    for i in gi:
        ya_rows = []
        for c in range(gsz // CHUNK):
            vc = v[i][c * CHUNK:(c + 1) * CHUNK, :]
            s_parts = []
            for q in range(PAIRS):
                v2 = vc[:, q * LANES:(q + 1) * LANES]
                vm = jnp.concatenate([jnp.where(first_head, v2, 0.0),
                                      jnp.where(first_head, 0.0, v2)], axis=0).astype(BF16)
                s_parts.append(_dot(ws[q], vm))
            s = jnp.concatenate(s_parts, axis=1) + bs_ref[...]
            ya_rows.append(u[i][c * CHUNK:(c + 1) * CHUNK, :] * s)
        ya.append(jnp.concatenate(ya_rows, axis=0).astype(BF16))
    zga = [proj(i, COLS_GA) for i in gi]
    pa = [_dot(ya[i], wba_ref[...]) for i in gi]

    for i, rows in enumerate(groups):
        gb_ref[0, rows, :] = _sigmoid(zgb[i]).astype(gb_ref.dtype)
        a = _sigmoid(a0_ref[...] + up[i][:, R_WIDTH:2 * R_WIDTH])
        kkn = kk[i] * lax.rsqrt(jnp.maximum(n2[i], 1e-24))
        k2 = z[i][:, R_WIDTH:2 * R_WIDTH] * (1.0 + (a - 1.0) * ka_ref[...])
        nkk = -kkn
        kka = kkn * a
        for q in range(PAIRS):
            sl = slice(q * LANES, (q + 1) * LANES)
            sc_ref[0, I_K, q, rows, :] = k2[:, sl]
            sc_ref[0, I_A, q, rows, :] = nkk[:, sl]
            sc_ref[0, I_B, q, rows, :] = kka[:, sl]
        ma_ref[0, rows, :] = (_sigmoid(zga[i]) * pa[i]).astype(ma_ref.dtype)


def _inproj(x, mod, p, tl):
    bsz, seq, _ = x.shape
    grid = (bsz, seq // tl)
    row = lambda b, s: (b, s, 0)
    sc_shape = jax.ShapeDtypeStruct((bsz, 6, PAIRS, seq, LANES), F32)
    sc_spec = pl.BlockSpec((1, 6, PAIRS, tl, LANES), lambda b, s: (b, 0, 0, s, 0))
    consts = [p["win"], p["bin"], p["glnv"], p["blnv"],
              p["ws"], p["bs"], p["mu"], p["w0"], p["a0"], p["wup"], p["kk"], p["ka"], p["bd"],
              p["wba"]]
    return pl.pallas_call(
        _inproj_kernel,
        grid=grid,
        in_specs=[pl.BlockSpec((1, tl, D_MODEL), row),
                  pl.BlockSpec((1, 6, D_MODEL), lambda b, s: (b, 0, 0))]
                 + [_const_spec(c.shape) for c in consts],
        out_specs=[pl.BlockSpec((1, tl, D_MODEL), row),
                   pl.BlockSpec((1, tl, D_MODEL), row)]
                  + [sc_spec]
                  + [pl.BlockSpec((1, tl, R_WIDTH), row)],
        out_shape=[jax.ShapeDtypeStruct((bsz, seq, D_MODEL), BF16),
                   jax.ShapeDtypeStruct((bsz, seq, D_MODEL), BF16)]
                  + [sc_shape]
                  + [jax.ShapeDtypeStruct((bsz, seq, R_WIDTH), BF16)],
        scratch_shapes=[pltpu.VMEM((tl + 8, R_COLS), F32)],
        compiler_params=pltpu.CompilerParams(
            dimension_semantics=("arbitrary", "arbitrary"), vmem_limit_bytes=VMEM_LIMIT),
        name="inproj",
    )(x, mod, *consts)


def _scan_kernel(sc_ref, rk_ref, gng_ref, gnb_ref, y_ref, st_ref):
    bt, _, pp, cs, _ = sc_ref.shape

    @pl.when(pl.program_id(0) == 0)
    def _():
        st_ref[...] = jnp.zeros(st_ref.shape, F32)

    lane = lax.broadcasted_iota(jnp.int32, (cs, LANES), 1)
    first_head = lane < R_HEAD

    def stack(t):
        return jnp.concatenate([jnp.where(first_head, t, 0.0),
                                jnp.where(first_head, 0.0, t)], axis=0)

    def fold(t):
        return jnp.where(first_head, t[:cs], t[cs:])

    ri = lax.broadcasted_iota(jnp.int32, (cs, cs), 0)
    ci = lax.broadcasted_iota(jnp.int32, (cs, cs), 1)
    tri_ones = jnp.where(ri >= ci, 1.0, 0.0).astype(BF16)
    si = lax.broadcasted_iota(jnp.int32, (cs, LANES), 0)
    sj = lane % cs
    strict = si > sj
    incl = si >= sj
    eye = jnp.where(si == sj, 1.0, 0.0).astype(F32)
    bsi = lax.broadcasted_iota(jnp.int32, (LANES, LANES), 0) // R_HEAD
    bsj = lax.broadcasted_iota(jnp.int32, (LANES, LANES), 1) // R_HEAD
    head_ones = jnp.where(bsi == bsj, 1.0, 0.0).astype(BF16)
    head_mean = jnp.where(bsi == bsj, 1.0 / R_HEAD, 0.0).astype(BF16)

    chains = [(b, q) for q in range(pp) for b in range(bt)]
    n_sq = (cs - 1).bit_length() - 1
    cstate = [dict(j=j, b=b, q=q) for j, (b, q) in enumerate(chains)]

    def s_cumsum(c):
        lw = sc_ref[c["b"], I_LW, c["q"]]
        p1 = lw.astype(BF16)
        p2 = (lw - p1.astype(F32)).astype(BF16)
        c["lw"] = lw
        c["cum"] = _dot(tri_ones, p1) + _dot(tri_ones, p2)

    def s_operands(c):
        b, q, cum, lw = c["b"], c["q"], c["cum"], c["lw"]
        last = cum[cs - 1:cs, :]
        e_pos = jnp.exp(cum)
        e_pos_ex = jnp.exp(cum - lw)
        e_neg = 1.0 / e_pos
        wcol = jnp.broadcast_to(jnp.exp(last), (LANES, LANES)).T
        r = sc_ref[b, I_R, q]
        k = sc_ref[b, I_K, q]
        bb = sc_ref[b, I_B, q]
        c["at"] = sc_ref[b, I_A, q] * e_pos_ex
        c["rt_b"] = (r * e_pos).astype(BF16)
        tb = stack(bb * e_neg).T
        tk = stack(k * e_neg).T
        c["vm"] = stack(sc_ref[b, I_V, q]).astype(BF16)
        c["bt"] = fold(tb).astype(BF16)
        c["kt"] = fold(tk)
        c["wm"] = fold(wcol)
        c["aa"] = _dot(jnp.concatenate([c["at"].astype(BF16), c["rt_b"]], axis=0),
                       jnp.concatenate([tb, tk], axis=1).astype(BF16))
        lanes = slice(q * LANES, (q + 1) * LANES)
        c["rkr"] = (r * k * rk_ref[:, lanes]).astype(BF16)

    def s_intra(c):
        aa = c.pop("aa")
        a_ab = jnp.where(strict, aa[:cs, :LANES], 0.0)
        c["a_rb"] = jnp.where(incl, aa[cs:, :LANES], 0.0).astype(BF16)
        c["akv_y0"] = _dot(jnp.concatenate([jnp.where(strict, aa[:cs, LANES:], 0.0),
                                            jnp.where(incl, aa[cs:, LANES:], 0.0),
                                            c.pop("kt")], axis=0).astype(BF16), c.pop("vm"))
        c["xk"] = _dot(a_ab.astype(BF16), stack(a_ab).astype(BF16))
        c["tk"] = eye + a_ab

    def s_square(c):
        x_bd = stack(c["xk"]).astype(BF16)
        xt = _dot(jnp.concatenate([c["xk"], c["tk"]], axis=0).astype(BF16), x_bd)
        c["xk"] = xt[:cs]
        c["tk"] = c["tk"] + xt[cs:]

    def s_solve(c):
        x_bd = stack(c.pop("xk")).astype(BF16)
        tk = c["tk"] + _dot(c["tk"].astype(BF16), x_bd)
        c["pq"] = _dot(tk.astype(BF16),
                       jnp.concatenate([stack(c.pop("at")), stack(c["akv_y0"][:cs])],
                                       axis=1).astype(BF16))
        del c["tk"]

    def s_u(c):
        c["st"] = st_ref[c["j"]]
        pq = c.pop("pq")
        ps = _dot(jnp.concatenate([pq[:, :LANES].astype(BF16), c.pop("rt_b")], axis=0),
                  stack(c["st"]).astype(BF16))
        c["u_b"] = stack(ps[:cs] + pq[:, LANES:]).astype(BF16)
        c["rs"] = ps[cs:]

    def s_y(c):
        au = _dot(jnp.concatenate([c.pop("a_rb"), c.pop("bt")], axis=0), c.pop("u_b"))
        akv_y0 = c.pop("akv_y0")
        c["y"] = c.pop("rs") + au[:cs] + akv_y0[cs:2 * cs]
        st_ref[c["j"]] = c.pop("wm") * (c.pop("st") + au[cs:] + akv_y0[2 * cs:])

    for stage in [s_cumsum, s_operands, s_intra]:
        for c in cstate:
            stage(c)
    rk_sum = _dot(jnp.concatenate([c.pop("rkr") for c in cstate], axis=0), head_ones)
    for stage in [s_square] * (n_sq - 1) + [s_solve, s_u]:
        for c in cstate:
            stage(c)

    gsz = max(1, len(cstate) // GN_BATCHES)
    batches = [cstate[i:i + gsz] for i in range(0, len(cstate), gsz)]
    d = {}
    for t in range(len(batches) + 2):
        if t < len(batches):
            for c in batches[t]:
                s_y(c)
        if 0 <= t - 1 < len(batches):
            y_b = jnp.concatenate([c.pop("y") for c in batches[t - 1]], axis=0)
            d[t - 1] = y_b - _dot(y_b.astype(BF16), head_mean)
        if 0 <= t - 2 < len(batches):
            d_b = d.pop(t - 2)
            yn = d_b * lax.rsqrt(_dot((d_b * d_b).astype(BF16), head_mean) + GN_EPS)
            for i, c in enumerate(batches[t - 2]):
                b, q, j = c["b"], c["q"], c["j"]
                lanes = slice(q * LANES, (q + 1) * LANES)
                y_ref[b, q] = (yn[i * cs:(i + 1) * cs] * gng_ref[:, lanes] + gnb_ref[:, lanes]
                               + rk_sum[j * cs:(j + 1) * cs] * sc_ref[b, I_V, q]).astype(y_ref.dtype)


def _scan(sc, rk, gng, gnb):
    bsz, _, pp, seq, _ = sc.shape
    blk = pl.BlockSpec((bsz, pp, SCAN_CHUNK, LANES), lambda s: (0, 0, s, 0))
    vec = pl.BlockSpec((1, pp * LANES), lambda s: (0, 0))
    return pl.pallas_call(
        _scan_kernel,
        grid=(seq // SCAN_CHUNK,),
        in_specs=[pl.BlockSpec((bsz, 6, pp, SCAN_CHUNK, LANES), lambda s: (0, 0, 0, s, 0))] + [vec] * 3,
        out_specs=blk,
        out_shape=jax.ShapeDtypeStruct((bsz, pp, seq, LANES), BF16),
        scratch_shapes=[pltpu.VMEM((bsz * pp, SCAN_CHUNK, LANES), F32)],
        compiler_params=pltpu.CompilerParams(
            dimension_semantics=("arbitrary",), vmem_limit_bytes=VMEM_LIMIT),
        name="scan",
    )(sc, rk, gng, gnb)


def _post_kernel(x_ref, mod_ref, yb_ref, g_ref, gb_ref, ma_ref, wbb_ref, wout_ref, bout_ref,
                 ln1g_ref, ln1b_ref, w1_ref, b1_ref, w2_ref, b2_ref, ln2g_ref, ln2b_ref, o_ref):
    tl = x_ref.shape[1]
    gt1 = mod_ref[0, 2:3, :]
    sh2 = mod_ref[0, 3:4, :]
    sc2 = mod_ref[0, 4:5, :]
    gt2 = mod_ref[0, 5:6, :]
    groups = [slice(i * tl // POST_GROUPS, (i + 1) * tl // POST_GROUPS) for i in range(POST_GROUPS)]
    pb = []
    for rows in groups:
        yb = (jnp.concatenate([yb_ref[0, q, rows, :] for q in range(PAIRS)], axis=1).astype(F32)
              * g_ref[0, rows, :].astype(F32))
        pb.append(_dot(yb.astype(BF16), wbb_ref[...]))
    h1 = []
    for i, rows in enumerate(groups):
        merged = ma_ref[0, rows, :].astype(F32) + gb_ref[0, rows, :].astype(F32) * pb[i]
        mix = _dot(merged.astype(BF16), wout_ref[...]) + bout_ref[...]
        h1.append(_layer_norm(ALPHA * x_ref[0, rows, :] + gt1 * mix,
                              ln1g_ref[...], ln1b_ref[...], LN_EPS))
    t = []
    for i in range(POST_GROUPS):
        h = (h1[i] * (1.0 + sc2) + sh2).astype(BF16)
        t.append(jnp.maximum(_dot(h, w1_ref[...]) + b1_ref[...], 0.0))
    for i, rows in enumerate(groups):
        ff = _dot((t[i] * t[i]).astype(BF16), w2_ref[...]) + b2_ref[...]
        o_ref[0, rows, :] = _layer_norm(ALPHA * h1[i] + gt2 * ff,
                                        ln2g_ref[...], ln2b_ref[...], LN_EPS).astype(o_ref.dtype)


def _post(x, mod, yb, g, gb, ma, p, tl, out_dtype):
    bsz, seq, _ = x.shape
    row = lambda b, s: (b, s, 0)
    consts = [p["wbb"], p["wout"], p["bout"], p["ln1g"], p["ln1b"],
              p["w1"], p["b1"], p["w2"], p["b2"], p["ln2g"], p["ln2b"]]
    return pl.pallas_call(
        _post_kernel,
        grid=(bsz, seq // tl),
        in_specs=[pl.BlockSpec((1, tl, D_MODEL), row),
                  pl.BlockSpec((1, 6, D_MODEL), lambda b, s: (b, 0, 0)),
                  pl.BlockSpec((1, PAIRS, tl, LANES), lambda b, s: (b, 0, s, 0)),
                  pl.BlockSpec((1, tl, R_WIDTH), row),
                  pl.BlockSpec((1, tl, D_MODEL), row),
                  pl.BlockSpec((1, tl, D_MODEL), row)]
                 + [_const_spec(c.shape) for c in consts],
        out_specs=pl.BlockSpec((1, tl, D_MODEL), row),
        out_shape=jax.ShapeDtypeStruct((bsz, seq, D_MODEL), out_dtype),
        compiler_params=pltpu.CompilerParams(
            dimension_semantics=("arbitrary", "arbitrary"), vmem_limit_bytes=VMEM_LIMIT),
        name="post",
    )(x, mod, yb, g, gb, ma, *consts)


def _relayout_w_in_kernel(w_ref, o_ref):
    split = 2 * G_WIDTH + 3 * R_WIDTH + LORA_COLS
    rows = w_ref.shape[0]
    o_ref[:, :split] = w_ref[:, :split].astype(BF16)
    o_ref[:, split:COLS_R.stop] = jnp.zeros((rows, COLS_R.stop - split), BF16)
    o_ref[:, COLS_R.stop:] = w_ref[:, split:].astype(BF16)


def _relayout_w_in(w_in, l):
    n_in = w_in.shape[2]
    tr = D_MODEL // 8
    return pl.pallas_call(
        _relayout_w_in_kernel,
        grid=(D_MODEL // tr,),
        in_specs=[pl.BlockSpec((None, tr, n_in), lambda i: (l, i, 0))],
        out_specs=pl.BlockSpec((tr, COLS_GB.stop), lambda i: (i, 0)),
        out_shape=jax.ShapeDtypeStruct((D_MODEL, COLS_GB.stop), BF16),
        name="wprep",
    )(w_in)


def _prepare_params(l, w_in, b_in, g_ln_v, b_ln_v, w_spatial, b_spatial, mu_shift, w0,
                    w_decay_up, a0, w_aaa_up, w_gate_up, k_k, k_a, r_k, gn_gain, gn_bias,
                    w_branch_a, w_branch_b, w_out, b_out, ln1_g, ln1_b, w_ff1, b_ff1, w_ff2,
                    b_ff2, ln2_g, ln2_b):
    g_end = 2 * G_WIDTH
    rkv_end = g_end + 3 * R_WIDTH
    r_end = rkv_end + LORA_COLS
    pad = LORA_PAD - LORA_COLS
    row2 = lambda t: t.reshape(1, -1)
    wi, bi = w_in[l], b_in[l]
    b_all = jnp.concatenate([bi[:r_end], jnp.zeros((pad,), F32), bi[r_end:]])
    mu = jnp.concatenate([mu_shift[l], jnp.zeros((pad,), F32)])
    wup = jnp.zeros((LORA_PAD, 3 * R_WIDTH), F32)
    wup = wup.at[0:DECAY_LORA, 0:R_WIDTH].set(w_decay_up[l])
    wup = wup.at[DECAY_LORA:DECAY_LORA + AAA_LORA, R_WIDTH:2 * R_WIDTH].set(w_aaa_up[l])
    wup = wup.at[DECAY_LORA + AAA_LORA:LORA_COLS, 2 * R_WIDTH:].set(w_gate_up[l])
    ws = w_spatial[l].reshape(PAIRS, 2, CHUNK, CHUNK).transpose(0, 2, 1, 3).reshape(PAIRS, CHUNK, 2 * CHUNK)
    bs = jnp.repeat(b_spatial[l].T, G_WIDTH // G_GROUPS, axis=1)
    hid = jnp.arange(R_WIDTH) // R_HEAD
    bd = (hid[:, None] == hid[None, :]).astype(BF16)
    return dict(
        win=_relayout_w_in(w_in, l), bin=row2(b_all),
        glnv=row2(g_ln_v[l]), blnv=row2(b_ln_v[l]), ws=ws, bs=bs, mu=row2(mu),
        w0=row2(w0[l]), a0=row2(a0[l]), wup=wup.astype(BF16), kk=row2(k_k[l]), ka=row2(k_a[l]),
        bd=bd, wba=w_branch_a[l].astype(BF16),
        rk=row2(r_k[l]), gng=row2(gn_gain[l]), gnb=row2(gn_bias[l]),
        wbb=w_branch_b[l].astype(BF16), wout=w_out[l].astype(BF16), bout=row2(b_out[l]),
        ln1g=row2(ln1_g[l]), ln1b=row2(ln1_b[l]),
        w1=w_ff1[l].astype(BF16), b1=row2(b_ff1[l]), w2=w_ff2[l].astype(BF16), b2=row2(b_ff2[l]),
        ln2g=row2(ln2_g[l]), ln2b=row2(ln2_b[l]),
    )


def _tile(seq, want):
    t = min(want, seq)
    while seq % t:
        t //= 2
    return t


def kernel(x, c, w_ada, b_ada, w_in, b_in, g_ln_v, b_ln_v, w_spatial, b_spatial, mu_shift, w0, w_decay_up, a0, w_aaa_up, w_gate_up, k_k, k_a, r_k, gn_gain, gn_bias, w_branch_a, w_branch_b, w_out, b_out, ln1_g, ln1_b, w_ff1, b_ff1, w_ff2, b_ff2, ln2_g, ln2_b):
    bsz, seq, _ = x.shape
    assert seq % CHUNK == 0 and x.shape[2] == D_MODEL
    out_dtype = x.dtype
    h_res = x.astype(F32)
    tl = _tile(seq, 512)
    for l in range(DEPTH):
        p = _prepare_params(l, w_in, b_in, g_ln_v, b_ln_v, w_spatial, b_spatial, mu_shift, w0,
                            w_decay_up, a0, w_aaa_up, w_gate_up, k_k, k_a, r_k, gn_gain, gn_bias,
                            w_branch_a, w_branch_b, w_out, b_out, ln1_g, ln1_b, w_ff1, b_ff1,
                            w_ff2, b_ff2, ln2_g, ln2_b)
        mod = _modulation(c.astype(F32), w_ada, b_ada, l).reshape(bsz, 6, D_MODEL)
        ma, gb, sc, g = _inproj(h_res, mod, p, tl)
        yb = _scan(sc, p["rk"], p["gng"], p["gnb"])
        h_res = _post(h_res, mod, yb, g, gb, ma, p, tl, F32)
    return h_res.astype(out_dtype)
```

```python
import math

import jax
import jax.numpy as jnp
from jax import lax
from jax.experimental import pallas as pl
from jax.experimental.pallas import tpu as pltpu

D_MODEL = 1024
G_GROUPS = 8
G_WIDTH = 512
CHUNK = 128
R_WIDTH = 512
R_HEAD = 64
R_HEADS = R_WIDTH // R_HEAD
DECAY_LORA = 32
AAA_LORA = 32
GATE_LORA = 96
LORA_COLS = DECAY_LORA + AAA_LORA + GATE_LORA
D_FF = 4 * D_MODEL
DEPTH = 1
ALPHA = (2.0 * DEPTH) ** 0.25
LN_EPS = 1e-5
GN_EPS = 64e-5
DECAY_SCALE = math.exp(-0.5)

LANES = 128
PAIRS = R_WIDTH // LANES
LORA_PAD = 2 * LANES
R_COLS = 3 * R_WIDTH + LORA_PAD
COLS_U = slice(0, G_WIDTH)
COLS_V = slice(G_WIDTH, 2 * G_WIDTH)
COLS_R = slice(2 * G_WIDTH, 2 * G_WIDTH + R_COLS)
COLS_GA = slice(COLS_R.stop, COLS_R.stop + D_MODEL)
COLS_GB = slice(COLS_GA.stop, COLS_GA.stop + D_MODEL)
SCAN_CHUNK = 64
STACK = 2 * SCAN_CHUNK
POST_GROUPS = 2
IN_GROUPS = 2
I_R, I_LW, I_K, I_V, I_A, I_B = range(6)
GN_BATCHES = 4
VMEM_LIMIT = 56 * 1024 * 1024

F32 = jnp.float32
BF16 = jnp.bfloat16


def _dot(a, b):
    return jnp.dot(a, b, preferred_element_type=F32)


def _sigmoid(x):
    return 1.0 / (1.0 + jnp.exp(-x))


def _gelu_tanh(x):
    c = 0.7978845608028654
    hx = 0.5 * x
    return hx + hx * jnp.tanh(x * (c + (c * 0.044715) * (x * x)))


def _layer_norm(x, g, b, eps):
    mu = jnp.mean(x, axis=-1, keepdims=True)
    d = x - mu
    var = jnp.mean(d * d, axis=-1, keepdims=True)
    return d * lax.rsqrt(var + eps) * g + b


def _const_spec(shape):
    n = len(shape)
    return pl.BlockSpec(shape, lambda *_: (0,) * n)


def _mod_kernel(c_ref, w_ref, b_ref, o_ref):
    c = c_ref[...]
    c_act = c * _sigmoid(c)
    o_ref[...] = jnp.dot(c_act, w_ref[...], preferred_element_type=F32,
                         precision=lax.Precision.HIGHEST) + b_ref[...]


def _modulation(c, w_ada, b_ada, l):
    bsz = c.shape[0]
    n = w_ada.shape[2]
    tn = D_MODEL
    return pl.pallas_call(
        _mod_kernel,
        grid=(n // tn,),
        in_specs=[pl.BlockSpec((bsz, D_MODEL), lambda j: (0, 0)),
                  pl.BlockSpec((None, D_MODEL, tn), lambda j: (l, 0, j)),
                  pl.BlockSpec((None, 1, tn), lambda j: (l, 0, j))],
        out_specs=pl.BlockSpec((bsz, tn), lambda j: (0, j)),
        out_shape=jax.ShapeDtypeStruct((bsz, n), F32),
        name="mod",
    )(c, w_ada, b_ada.reshape(b_ada.shape[0], 1, n))


def _inproj_kernel(x_ref, mod_ref, win_ref, bin_ref,
                   glnv_ref, blnv_ref, ws_ref, bs_ref, mu_ref, w0_ref, a0_ref, wup_ref,
                   kk_ref, ka_ref, bd_ref, wba_ref,
                   ma_ref, gb_ref, sc_ref, g_ref,
                   zsh_ref):
    tl = x_ref.shape[1]

    @pl.when(pl.program_id(1) == 0)
    def _():
        zsh_ref[0:8, :] = jnp.zeros((8, R_COLS), F32)

    sh1 = mod_ref[0, 0:1, :]
    sc1 = mod_ref[0, 1:2, :]
    gsz = tl // IN_GROUPS
    groups = [slice(i * gsz, (i + 1) * gsz) for i in range(IN_GROUPS)]
    gi = range(IN_GROUPS)
    h = [(x_ref[0, rows, :] * (1.0 + sc1) + sh1).astype(BF16) for rows in groups]
    def proj(i, cols):
        return _dot(h[i], win_ref[:, cols]) + bin_ref[:, cols]

    zr = [proj(i, COLS_R) for i in gi]
    for i, rows in enumerate(groups):
        zsh_ref[8 + rows.start:8 + rows.stop, :] = zr[i]
    zu = [proj(i, COLS_U) for i in gi]
    zv = [proj(i, COLS_V) for i in gi]

    z = []
    for i, rows in enumerate(groups):
        prev = zsh_ref[7 + rows.start:7 + rows.stop, :]
        z.append(zr[i] + (prev - zr[i]) * mu_ref[...])
    zsh_ref[7:8, :] = zr[-1][gsz - 1:gsz, :]
    llane = lax.broadcasted_iota(jnp.int32, (gsz, LORA_PAD), 1)
    lin = []
    for i in gi:
        xl = z[i][:, 3 * R_WIDTH:]
        lin.append(jnp.where(llane < DECAY_LORA, jnp.tanh(xl),
                             jnp.where(llane < DECAY_LORA + AAA_LORA, xl, _sigmoid(xl))).astype(BF16))
    zgb = [proj(i, COLS_GB) for i in gi]
    up = [_dot(lin[i], wup_ref[...]) for i in gi]

    u = [_gelu_tanh(zu[i]) for i in gi]
    v = [_layer_norm(_gelu_tanh(zv[i]), glnv_ref[...], blnv_ref[...], LN_EPS) for i in gi]

    kk, n2 = [], []
    for i, rows in enumerate(groups):
        lw = -DECAY_SCALE * _sigmoid(w0_ref[...] + up[i][:, 0:R_WIDTH])
        g_ref[0, rows, :] = up[i][:, 2 * R_WIDTH:].astype(g_ref.dtype)
        kk.append(z[i][:, R_WIDTH:2 * R_WIDTH] * kk_ref[...])
        n2.append(_dot((kk[i] * kk[i]).astype(BF16), bd_ref[...]))
        for q in range(PAIRS):
            sl = slice(q * LANES, (q + 1) * LANES)
            sc_ref[0, I_LW, q, rows, :] = lw[:, sl]
            sc_ref[0, I_R, q, rows, :] = z[i][:, sl]
            sc_ref[0, I_V, q, rows, :] = z[i][:, 2 * R_WIDTH + q * LANES:2 * R_WIDTH + (q + 1) * LANES]

    lane = lax.broadcasted_iota(jnp.int32, (CHUNK, LANES), 1)
    first_head = lane < R_HEAD
    trow = lax.broadcasted_iota(jnp.int32, (CHUNK, 2 * CHUNK), 0)
    scol = lax.broadcasted_iota(jnp.int32, (CHUNK, 2 * CHUNK), 1) % CHUNK
    causal = trow >= scol
    ws = [jnp.where(causal, ws_ref[q], 0.0).astype(BF16) for q in range(PAIRS)]
    ya = []
    for i in gi:
        ya_rows = []
        for c in range(gsz // CHUNK):
            vc = v[i][c * CHUNK:(c + 1) * CHUNK, :]
            s_parts = []
            for q in range(PAIRS):
                v2 = vc[:, q * LANES:(q + 1) * LANES]
                vm = jnp.concatenate([jnp.where(first_head, v2, 0.0),
                                      jnp.where(first_head, 0.0, v2)], axis=0).astype(BF16)
                s_parts.append(_dot(ws[q], vm))
            s = jnp.concatenate(s_parts, axis=1) + bs_ref[...]
            ya_rows.append(u[i][c * CHUNK:(c + 1) * CHUNK, :] * s)
        ya.append(jnp.concatenate(ya_rows, axis=0).astype(BF16))
    zga = [proj(i, COLS_GA) for i in gi]
    pa = [_dot(ya[i], wba_ref[...]) for i in gi]

    for i, rows in enumerate(groups):
        gb_ref[0, rows, :] = _sigmoid(zgb[i]).astype(gb_ref.dtype)
        a = _sigmoid(a0_ref[...] + up[i][:, R_WIDTH:2 * R_WIDTH])
        kkn = kk[i] * lax.rsqrt(jnp.maximum(n2[i], 1e-24))
        k2 = z[i][:, R_WIDTH:2 * R_WIDTH] * (1.0 + (a - 1.0) * ka_ref[...])
        nkk = -kkn
        kka = kkn * a
        for q in range(PAIRS):
            sl = slice(q * LANES, (q + 1) * LANES)
            sc_ref[0, I_K, q, rows, :] = k2[:, sl]
            sc_ref[0, I_A, q, rows, :] = nkk[:, sl]
            sc_ref[0, I_B, q, rows, :] = kka[:, sl]
        ma_ref[0, rows, :] = (_sigmoid(zga[i]) * pa[i]).astype(ma_ref.dtype)


def _inproj(x, mod, p, tl):
    bsz, seq, _ = x.shape
    grid = (bsz, seq // tl)
    row = lambda b, s: (b, s, 0)
    sc_shape = jax.ShapeDtypeStruct((bsz, 6, PAIRS, seq, LANES), F32)
    sc_spec = pl.BlockSpec((1, 6, PAIRS, tl, LANES), lambda b, s: (b, 0, 0, s, 0))
    consts = [p["win"], p["bin"], p["glnv"], p["blnv"],
              p["ws"], p["bs"], p["mu"], p["w0"], p["a0"], p["wup"], p["kk"], p["ka"], p["bd"],
              p["wba"]]
    return pl.pallas_call(
        _inproj_kernel,
        grid=grid,
        in_specs=[pl.BlockSpec((1, tl, D_MODEL), row),
                  pl.BlockSpec((1, 6, D_MODEL), lambda b, s: (b, 0, 0))]
                 + [_const_spec(c.shape) for c in consts],
        out_specs=[pl.BlockSpec((1, tl, D_MODEL), row),
                   pl.BlockSpec((1, tl, D_MODEL), row)]
                  + [sc_spec]
                  + [pl.BlockSpec((1, tl, R_WIDTH), row)],
        out_shape=[jax.ShapeDtypeStruct((bsz, seq, D_MODEL), BF16),
                   jax.ShapeDtypeStruct((bsz, seq, D_MODEL), BF16)]
                  + [sc_shape]
                  + [jax.ShapeDtypeStruct((bsz, seq, R_WIDTH), BF16)],
        scratch_shapes=[pltpu.VMEM((tl + 8, R_COLS), F32)],
        compiler_params=pltpu.CompilerParams(
            dimension_semantics=("arbitrary", "arbitrary"), vmem_limit_bytes=VMEM_LIMIT),
        name="inproj",
    )(x, mod, *consts)


def _scan_kernel(sc_ref, rk_ref, gng_ref, gnb_ref, y_ref, st_ref):
    bt, _, pp, cs, _ = sc_ref.shape

    @pl.when(pl.program_id(0) == 0)
    def _():
        st_ref[...] = jnp.zeros(st_ref.shape, F32)

    lane = lax.broadcasted_iota(jnp.int32, (cs, LANES), 1)
    first_head = lane < R_HEAD

    def stack(t):
        return jnp.concatenate([jnp.where(first_head, t, 0.0),
                                jnp.where(first_head, 0.0, t)], axis=0)

    def fold(t):
        return jnp.where(first_head, t[:cs], t[cs:])

    ri = lax.broadcasted_iota(jnp.int32, (cs, cs), 0)
    ci = lax.broadcasted_iota(jnp.int32, (cs, cs), 1)
    tri_ones = jnp.where(ri >= ci, 1.0, 0.0).astype(BF16)
    si = lax.broadcasted_iota(jnp.int32, (cs, LANES), 0)
    sj = lane % cs
    strict = si > sj
    incl = si >= sj
    eye = jnp.where(si == sj, 1.0, 0.0).astype(F32)
    bsi = lax.broadcasted_iota(jnp.int32, (LANES, LANES), 0) // R_HEAD
    bsj = lax.broadcasted_iota(jnp.int32, (LANES, LANES), 1) // R_HEAD
    head_ones = jnp.where(bsi == bsj, 1.0, 0.0).astype(BF16)
    head_mean = jnp.where(bsi == bsj, 1.0 / R_HEAD, 0.0).astype(BF16)

    chains = [(b, q) for q in range(pp) for b in range(bt)]
    n_sq = (cs - 1).bit_length() - 1
    cstate = [dict(j=j, b=b, q=q) for j, (b, q) in enumerate(chains)]

    def s_cumsum(c):
        lw = sc_ref[c["b"], I_LW, c["q"]]
        p1 = lw.astype(BF16)
        p2 = (lw - p1.astype(F32)).astype(BF16)
        c["lw"] = lw
        c["cum"] = _dot(tri_ones, p1) + _dot(tri_ones, p2)

    def s_operands(c):
        b, q, cum, lw = c["b"], c["q"], c["cum"], c["lw"]
        last = cum[cs - 1:cs, :]
        e_pos = jnp.exp(cum)
        e_pos_ex = jnp.exp(cum - lw)
        e_neg = 1.0 / e_pos
        wcol = jnp.broadcast_to(jnp.exp(last), (LANES, LANES)).T
        r = sc_ref[b, I_R, q]
        k = sc_ref[b, I_K, q]
        bb = sc_ref[b, I_B, q]
        c["at"] = sc_ref[b, I_A, q] * e_pos_ex
        c["rt_b"] = (r * e_pos).astype(BF16)
        tb = stack(bb * e_neg).T
        tk = stack(k * e_neg).T
        c["vm"] = stack(sc_ref[b, I_V, q]).astype(BF16)
        c["bt"] = fold(tb).astype(BF16)
        c["kt"] = fold(tk)
        c["wm"] = fold(wcol)
        c["aa"] = _dot(jnp.concatenate([c["at"].astype(BF16), c["rt_b"]], axis=0),
                       jnp.concatenate([tb, tk], axis=1).astype(BF16))
        lanes = slice(q * LANES, (q + 1) * LANES)
        c["rkr"] = (r * k * rk_ref[:, lanes]).astype(BF16)

    def s_intra(c):
        aa = c.pop("aa")
        a_ab = jnp.where(strict, aa[:cs, :LANES], 0.0)
        c["a_rb"] = jnp.where(incl, aa[cs:, :LANES], 0.0).astype(BF16)
        c["akv_y0"] = _dot(jnp.concatenate([jnp.where(strict, aa[:cs, LANES:], 0.0),
                                            jnp.where(incl, aa[cs:, LANES:], 0.0),
                                            c.pop("kt")], axis=0).astype(BF16), c.pop("vm"))
        c["xk"] = _dot(a_ab.astype(BF16), stack(a_ab).astype(BF16))
        c["tk"] = eye + a_ab

    def s_square(c):
        x_bd = stack(c["xk"]).astype(BF16)
        xt = _dot(jnp.concatenate([c["xk"], c["tk"]], axis=0).astype(BF16), x_bd)
        c["xk"] = xt[:cs]
        c["tk"] = c["tk"] + xt[cs:]

    def s_solve(c):
        x_bd = stack(c.pop("xk")).astype(BF16)
        tk = c["tk"] + _dot(c["tk"].astype(BF16), x_bd)
        c["pq"] = _dot(tk.astype(BF16),
                       jnp.concatenate([stack(c.pop("at")), stack(c["akv_y0"][:cs])],
                                       axis=1).astype(BF16))
        del c["tk"]

    def s_u(c):
        c["st"] = st_ref[c["j"]]
        pq = c.pop("pq")
        ps = _dot(jnp.concatenate([pq[:, :LANES].astype(BF16), c.pop("rt_b")], axis=0),
                  stack(c["st"]).astype(BF16))
        c["u_b"] = stack(ps[:cs] + pq[:, LANES:]).astype(BF16)
        c["rs"] = ps[cs:]

    def s_y(c):
        au = _dot(jnp.concatenate([c.pop("a_rb"), c.pop("bt")], axis=0), c.pop("u_b"))
        akv_y0 = c.pop("akv_y0")
        c["y"] = c.pop("rs") + au[:cs] + akv_y0[cs:2 * cs]
        st_ref[c["j"]] = c.pop("wm") * (c.pop("st") + au[cs:] + akv_y0[2 * cs:])

    for stage in [s_cumsum, s_operands, s_intra]:
        for c in cstate:
            stage(c)
    rk_sum = _dot(jnp.concatenate([c.pop("rkr") for c in cstate], axis=0), head_ones)
    for stage in [s_square] * (n_sq - 1) + [s_solve, s_u]:
        for c in cstate:
            stage(c)

    gsz = max(1, len(cstate) // GN_BATCHES)
    batches = [cstate[i:i + gsz] for i in range(0, len(cstate), gsz)]
    d = {}
    for t in range(len(batches) + 2):
        if t < len(batches):
            for c in batches[t]:
                s_y(c)
        if 0 <= t - 1 < len(batches):
            y_b = jnp.concatenate([c.pop("y") for c in batches[t - 1]], axis=0)
            d[t - 1] = y_b - _dot(y_b.astype(BF16), head_mean)
        if 0 <= t - 2 < len(batches):
            d_b = d.pop(t - 2)
            yn = d_b * lax.rsqrt(_dot((d_b * d_b).astype(BF16), head_mean) + GN_EPS)
            for i, c in enumerate(batches[t - 2]):
                b, q, j = c["b"], c["q"], c["j"]
                lanes = slice(q * LANES, (q + 1) * LANES)
                y_ref[b, q] = (yn[i * cs:(i + 1) * cs] * gng_ref[:, lanes] + gnb_ref[:, lanes]
                               + rk_sum[j * cs:(j + 1) * cs] * sc_ref[b, I_V, q]).astype(y_ref.dtype)


def _scan(sc, rk, gng, gnb):
    bsz, _, pp, seq, _ = sc.shape
    blk = pl.BlockSpec((bsz, pp, SCAN_CHUNK, LANES), lambda s: (0, 0, s, 0))
    vec = pl.BlockSpec((1, pp * LANES), lambda s: (0, 0))
    return pl.pallas_call(
        _scan_kernel,
        grid=(seq // SCAN_CHUNK,),
        in_specs=[pl.BlockSpec((bsz, 6, pp, SCAN_CHUNK, LANES), lambda s: (0, 0, 0, s, 0))] + [vec] * 3,
        out_specs=blk,
        out_shape=jax.ShapeDtypeStruct((bsz, pp, seq, LANES), BF16),
        scratch_shapes=[pltpu.VMEM((bsz * pp, SCAN_CHUNK, LANES), F32)],
        compiler_params=pltpu.CompilerParams(
            dimension_semantics=("arbitrary",), vmem_limit_bytes=VMEM_LIMIT),
        name="scan",
    )(sc, rk, gng, gnb)


def _post_kernel(x_ref, mod_ref, yb_ref, g_ref, gb_ref, ma_ref, wbb_ref, wout_ref, bout_ref,
                 ln1g_ref, ln1b_ref, w1_ref, b1_ref, w2_ref, b2_ref, ln2g_ref, ln2b_ref, o_ref):
    tl = x_ref.shape[1]
    gt1 = mod_ref[0, 2:3, :]
    sh2 = mod_ref[0, 3:4, :]
    sc2 = mod_ref[0, 4:5, :]
    gt2 = mod_ref[0, 5:6, :]
    groups = [slice(i * tl // POST_GROUPS, (i + 1) * tl // POST_GROUPS) for i in range(POST_GROUPS)]
    pb = []
    for rows in groups:
        yb = (jnp.concatenate([yb_ref[0, q, rows, :] for q in range(PAIRS)], axis=1).astype(F32)
              * g_ref[0, rows, :].astype(F32))
        pb.append(_dot(yb.astype(BF16), wbb_ref[...]))
    h1 = []
    for i, rows in enumerate(groups):
        merged = ma_ref[0, rows, :].astype(F32) + gb_ref[0, rows, :].astype(F32) * pb[i]
        mix = _dot(merged.astype(BF16), wout_ref[...]) + bout_ref[...]
        h1.append(_layer_norm(ALPHA * x_ref[0, rows, :] + gt1 * mix,
                              ln1g_ref[...], ln1b_ref[...], LN_EPS))
    t = []
    for i in range(POST_GROUPS):
        h = (h1[i] * (1.0 + sc2) + sh2).astype(BF16)
        t.append(jnp.maximum(_dot(h, w1_ref[...]) + b1_ref[...], 0.0))
    for i, rows in enumerate(groups):
        ff = _dot((t[i] * t[i]).astype(BF16), w2_ref[...]) + b2_ref[...]
        o_ref[0, rows, :] = _layer_norm(ALPHA * h1[i] + gt2 * ff,
                                        ln2g_ref[...], ln2b_ref[...], LN_EPS).astype(o_ref.dtype)


def _post(x, mod, yb, g, gb, ma, p, tl, out_dtype):
    bsz, seq, _ = x.shape
    row = lambda b, s: (b, s, 0)
    consts = [p["wbb"], p["wout"], p["bout"], p["ln1g"], p["ln1b"],
              p["w1"], p["b1"], p["w2"], p["b2"], p["ln2g"], p["ln2b"]]
    return pl.pallas_call(
        _post_kernel,
        grid=(bsz, seq // tl),
        in_specs=[pl.BlockSpec((1, tl, D_MODEL), row),
                  pl.BlockSpec((1, 6, D_MODEL), lambda b, s: (b, 0, 0)),
                  pl.BlockSpec((1, PAIRS, tl, LANES), lambda b, s: (b, 0, s, 0)),
                  pl.BlockSpec((1, tl, R_WIDTH), row),
                  pl.BlockSpec((1, tl, D_MODEL), row),
                  pl.BlockSpec((1, tl, D_MODEL), row)]
                 + [_const_spec(c.shape) for c in consts],
        out_specs=pl.BlockSpec((1, tl, D_MODEL), row),
        out_shape=jax.ShapeDtypeStruct((bsz, seq, D_MODEL), out_dtype),
        compiler_params=pltpu.CompilerParams(
            dimension_semantics=("arbitrary", "arbitrary"), vmem_limit_bytes=VMEM_LIMIT),
        name="post",
    )(x, mod, yb, g, gb, ma, *consts)


W_IN_SPLIT = 2 * G_WIDTH + 3 * R_WIDTH + LORA_COLS
GAP_BLOCK = W_IN_SPLIT // LANES


def _relayout_w_in_kernel(wt_ref, o_ref):
    j = pl.program_id(0)
    rows = lax.broadcasted_iota(jnp.int32, wt_ref.shape, 0)
    keep = jnp.logical_or(j != GAP_BLOCK, rows < W_IN_SPLIT % LANES)
    o_ref[...] = jnp.where(keep, wt_ref[...], 0.0).T.astype(BF16)


def _relayout_w_in(w_in, l):
    wt = jnp.transpose(w_in[l])
    gap = COLS_R.stop - W_IN_SPLIT

    unit = math.gcd(LANES, gap)

    def src_row(j):
        return (j * (LANES // unit) - jnp.where(j <= GAP_BLOCK, 0, gap // unit)) * unit

    return pl.pallas_call(
        _relayout_w_in_kernel,
        grid=(COLS_GB.stop // LANES,),
        in_specs=[pl.BlockSpec((pl.Element(LANES), pl.Element(D_MODEL)),
                               lambda j: (src_row(j), 0))],
        out_specs=pl.BlockSpec((D_MODEL, LANES), lambda j: (0, j)),
        out_shape=jax.ShapeDtypeStruct((D_MODEL, COLS_GB.stop), BF16),
        name="wprep",
    )(wt)


def _prepare_params(l, w_in, b_in, g_ln_v, b_ln_v, w_spatial, b_spatial, mu_shift, w0,
                    w_decay_up, a0, w_aaa_up, w_gate_up, k_k, k_a, r_k, gn_gain, gn_bias,
                    w_branch_a, w_branch_b, w_out, b_out, ln1_g, ln1_b, w_ff1, b_ff1, w_ff2,
                    b_ff2, ln2_g, ln2_b):
    g_end = 2 * G_WIDTH
    rkv_end = g_end + 3 * R_WIDTH
    r_end = rkv_end + LORA_COLS
    pad = LORA_PAD - LORA_COLS
    row2 = lambda t: t.reshape(1, -1)
    wi, bi = w_in[l], b_in[l]
    b_all = jnp.concatenate([bi[:r_end], jnp.zeros((pad,), F32), bi[r_end:]])
    mu = jnp.concatenate([mu_shift[l], jnp.zeros((pad,), F32)])
    wup = jnp.zeros((LORA_PAD, 3 * R_WIDTH), F32)
    wup = wup.at[0:DECAY_LORA, 0:R_WIDTH].set(w_decay_up[l])
    wup = wup.at[DECAY_LORA:DECAY_LORA + AAA_LORA, R_WIDTH:2 * R_WIDTH].set(w_aaa_up[l])
    wup = wup.at[DECAY_LORA + AAA_LORA:LORA_COLS, 2 * R_WIDTH:].set(w_gate_up[l])
    ws = w_spatial[l].reshape(PAIRS, 2, CHUNK, CHUNK).transpose(0, 2, 1, 3).reshape(PAIRS, CHUNK, 2 * CHUNK)
    bs = jnp.repeat(b_spatial[l].T, G_WIDTH // G_GROUPS, axis=1)
    hid = jnp.arange(R_WIDTH) // R_HEAD
    bd = (hid[:, None] == hid[None, :]).astype(BF16)
    return dict(
        win=_relayout_w_in(w_in, l), bin=row2(b_all),
        glnv=row2(g_ln_v[l]), blnv=row2(b_ln_v[l]), ws=ws, bs=bs, mu=row2(mu),
        w0=row2(w0[l]), a0=row2(a0[l]), wup=wup.astype(BF16), kk=row2(k_k[l]), ka=row2(k_a[l]),
        bd=bd, wba=w_branch_a[l].astype(BF16),
        rk=row2(r_k[l]), gng=row2(gn_gain[l]), gnb=row2(gn_bias[l]),
        wbb=w_branch_b[l].astype(BF16), wout=w_out[l].astype(BF16), bout=row2(b_out[l]),
        ln1g=row2(ln1_g[l]), ln1b=row2(ln1_b[l]),
        w1=w_ff1[l].astype(BF16), b1=row2(b_ff1[l]), w2=w_ff2[l].astype(BF16), b2=row2(b_ff2[l]),
        ln2g=row2(ln2_g[l]), ln2b=row2(ln2_b[l]),
    )


def _tile(seq, want):
    t = min(want, seq)
    while seq % t:
        t //= 2
    return t


def kernel(x, c, w_ada, b_ada, w_in, b_in, g_ln_v, b_ln_v, w_spatial, b_spatial, mu_shift, w0, w_decay_up, a0, w_aaa_up, w_gate_up, k_k, k_a, r_k, gn_gain, gn_bias, w_branch_a, w_branch_b, w_out, b_out, ln1_g, ln1_b, w_ff1, b_ff1, w_ff2, b_ff2, ln2_g, ln2_b):
    bsz, seq, _ = x.shape
    assert seq % CHUNK == 0 and x.shape[2] == D_MODEL
    out_dtype = x.dtype
    h_res = x.astype(F32)
    tl = _tile(seq, 512)
    for l in range(DEPTH):
        p = _prepare_params(l, w_in, b_in, g_ln_v, b_ln_v, w_spatial, b_spatial, mu_shift, w0,
                            w_decay_up, a0, w_aaa_up, w_gate_up, k_k, k_a, r_k, gn_gain, gn_bias,
                            w_branch_a, w_branch_b, w_out, b_out, ln1_g, ln1_b, w_ff1, b_ff1,
                            w_ff2, b_ff2, ln2_g, ln2_b)
        mod = _modulation(c.astype(F32), w_ada, b_ada, l).reshape(bsz, 6, D_MODEL)
        ma, gb, sc, g = _inproj(h_res, mod, p, tl)
        yb = _scan(sc, p["rk"], p["gng"], p["gnb"])
        h_res = _post(h_res, mod, yb, g, gb, ma, p, tl, F32)
    return h_res.astype(out_dtype)
```

```python
import math

import jax
import jax.numpy as jnp
from jax import lax
from jax.experimental import pallas as pl
from jax.experimental.pallas import tpu as pltpu

D_MODEL = 1024
G_GROUPS = 8
G_WIDTH = 512
CHUNK = 128
R_WIDTH = 512
R_HEAD = 64
R_HEADS = R_WIDTH // R_HEAD
DECAY_LORA = 32
AAA_LORA = 32
GATE_LORA = 96
LORA_COLS = DECAY_LORA + AAA_LORA + GATE_LORA
D_FF = 4 * D_MODEL
DEPTH = 1
ALPHA = (2.0 * DEPTH) ** 0.25
LN_EPS = 1e-5
GN_EPS = 64e-5
DECAY_SCALE = math.exp(-0.5)

LANES = 128
PAIRS = R_WIDTH // LANES
LORA_PAD = 2 * LANES
R_COLS = 3 * R_WIDTH + LORA_PAD
COLS_U = slice(0, G_WIDTH)
COLS_V = slice(G_WIDTH, 2 * G_WIDTH)
COLS_R = slice(2 * G_WIDTH, 2 * G_WIDTH + R_COLS)
COLS_GA = slice(COLS_R.stop, COLS_R.stop + D_MODEL)
COLS_GB = slice(COLS_GA.stop, COLS_GA.stop + D_MODEL)
SCAN_CHUNK = 64
STACK = 2 * SCAN_CHUNK
POST_GROUPS = 2
IN_GROUPS = 2
I_R, I_LW, I_K, I_V, I_A, I_B = range(6)
GN_BATCHES = 4
VMEM_LIMIT = 56 * 1024 * 1024

F32 = jnp.float32
BF16 = jnp.bfloat16


def _dot(a, b):
    return jnp.dot(a, b, preferred_element_type=F32)


def _sigmoid(x):
    return 1.0 / (1.0 + jnp.exp(-x))


def _gelu_tanh(x):
    c = 0.7978845608028654
    hx = 0.5 * x
    return hx + hx * jnp.tanh(x * (c + (c * 0.044715) * (x * x)))


def _layer_norm(x, g, b, eps):
    mu = jnp.mean(x, axis=-1, keepdims=True)
    d = x - mu
    var = jnp.mean(d * d, axis=-1, keepdims=True)
    return d * lax.rsqrt(var + eps) * g + b


def _const_spec(shape):
    n = len(shape)
    return pl.BlockSpec(shape, lambda *_: (0,) * n)


def _mod_kernel(c_ref, w_ref, b_ref, o_ref):
    c = c_ref[...]
    c_act = c * _sigmoid(c)
    bsz = c.shape[0]
    ch = c_act.astype(BF16)
    cl = (c_act - ch.astype(F32)).astype(BF16)
    w = w_ref[...]
    wh = w.astype(BF16)
    wl = (w - wh.astype(F32)).astype(BF16)
    hh = _dot(jnp.concatenate([ch, cl], axis=0), wh)
    o_ref[...] = hh[:bsz] + hh[bsz:] + _dot(ch, wl) + b_ref[...]


def _modulation(c, w_ada, b_ada, l):
    bsz = c.shape[0]
    n = w_ada.shape[2]
    tn = D_MODEL
    return pl.pallas_call(
        _mod_kernel,
        grid=(n // tn,),
        in_specs=[pl.BlockSpec((bsz, D_MODEL), lambda j: (0, 0)),
                  pl.BlockSpec((None, D_MODEL, tn), lambda j: (l, 0, j)),
                  pl.BlockSpec((None, 1, tn), lambda j: (l, 0, j))],
        out_specs=pl.BlockSpec((bsz, tn), lambda j: (0, j)),
        out_shape=jax.ShapeDtypeStruct((bsz, n), F32),
        name="mod",
    )(c, w_ada, b_ada.reshape(b_ada.shape[0], 1, n))


def _inproj_kernel(x_ref, mod_ref, win_ref, bin_ref,
                   glnv_ref, blnv_ref, ws_ref, bs_ref, mu_ref, w0_ref, a0_ref, wup_ref,
                   kk_ref, ka_ref, bd_ref, wba_ref,
                   ma_ref, gb_ref, sc_ref, g_ref,
                   zsh_ref):
    tl = x_ref.shape[1]

    @pl.when(pl.program_id(1) == 0)
    def _():
        zsh_ref[0:8, :] = jnp.zeros((8, R_COLS), F32)

    sh1 = mod_ref[0, 0:1, :]
    sc1 = mod_ref[0, 1:2, :]
    gsz = tl // IN_GROUPS
    groups = [slice(i * gsz, (i + 1) * gsz) for i in range(IN_GROUPS)]
    gi = range(IN_GROUPS)
    h = [(x_ref[0, rows, :] * (1.0 + sc1) + sh1).astype(BF16) for rows in groups]
    def proj(i, cols):
        return _dot(h[i], win_ref[:, cols]) + bin_ref[:, cols]

    zr = [proj(i, COLS_R) for i in gi]
    for i, rows in enumerate(groups):
        zsh_ref[8 + rows.start:8 + rows.stop, :] = zr[i]
    zu = [proj(i, COLS_U) for i in gi]
    zv = [proj(i, COLS_V) for i in gi]

    z = []
    for i, rows in enumerate(groups):
        prev = zsh_ref[7 + rows.start:7 + rows.stop, :]
        z.append(zr[i] + (prev - zr[i]) * mu_ref[...])
    zsh_ref[7:8, :] = zr[-1][gsz - 1:gsz, :]
    llane = lax.broadcasted_iota(jnp.int32, (gsz, LORA_PAD), 1)
    lin = []
    for i in gi:
        xl = z[i][:, 3 * R_WIDTH:]
        lin.append(jnp.where(llane < DECAY_LORA, jnp.tanh(xl),
                             jnp.where(llane < DECAY_LORA + AAA_LORA, xl, _sigmoid(xl))).astype(BF16))
    zgb = [proj(i, COLS_GB) for i in gi]
    up = [_dot(lin[i], wup_ref[...]) for i in gi]

    u = [_gelu_tanh(zu[i]) for i in gi]
    v = [_layer_norm(_gelu_tanh(zv[i]), glnv_ref[...], blnv_ref[...], LN_EPS) for i in gi]

    kk, n2 = [], []
    for i, rows in enumerate(groups):
        lw = -DECAY_SCALE * _sigmoid(w0_ref[...] + up[i][:, 0:R_WIDTH])
        g_ref[0, rows, :] = up[i][:, 2 * R_WIDTH:].astype(g_ref.dtype)
        kk.append(z[i][:, R_WIDTH:2 * R_WIDTH] * kk_ref[...])
        n2.append(_dot((kk[i] * kk[i]).astype(BF16), bd_ref[...]))
        for q in range(PAIRS):
            sl = slice(q * LANES, (q + 1) * LANES)
            sc_ref[0, I_LW, q, rows, :] = lw[:, sl]
            sc_ref[0, I_R, q, rows, :] = z[i][:, sl]
            sc_ref[0, I_V, q, rows, :] = z[i][:, 2 * R_WIDTH + q * LANES:2 * R_WIDTH + (q + 1) * LANES]

    lane = lax.broadcasted_iota(jnp.int32, (CHUNK, LANES), 1)
    first_head = lane < R_HEAD
    trow = lax.broadcasted_iota(jnp.int32, (CHUNK, 2 * CHUNK), 0)
    scol = lax.broadcasted_iota(jnp.int32, (CHUNK, 2 * CHUNK), 1) % CHUNK
    causal = trow >= scol
    ws = [jnp.where(causal, ws_ref[q], 0.0).astype(BF16) for q in range(PAIRS)]
    ya = []
    for i in gi:
        ya_rows = []
        for c in range(gsz // CHUNK):
            vc = v[i][c * CHUNK:(c + 1) * CHUNK, :]
            s_parts = []
            for q in range(PAIRS):
                v2 = vc[:, q * LANES:(q + 1) * LANES]
                vm = jnp.concatenate([jnp.where(first_head, v2, 0.0),
                                      jnp.where(first_head, 0.0, v2)], axis=0).astype(BF16)
                s_parts.append(_dot(ws[q], vm))
            s = jnp.concatenate(s_parts, axis=1) + bs_ref[...]
            ya_rows.append(u[i][c * CHUNK:(c + 1) * CHUNK, :] * s)
        ya.append(jnp.concatenate(ya_rows, axis=0).astype(BF16))
    zga = [proj(i, COLS_GA) for i in gi]
    pa = [_dot(ya[i], wba_ref[...]) for i in gi]

    for i, rows in enumerate(groups):
        gb_ref[0, rows, :] = _sigmoid(zgb[i]).astype(gb_ref.dtype)
        a = _sigmoid(a0_ref[...] + up[i][:, R_WIDTH:2 * R_WIDTH])
        kkn = kk[i] * lax.rsqrt(jnp.maximum(n2[i], 1e-24))
        k2 = z[i][:, R_WIDTH:2 * R_WIDTH] * (1.0 + (a - 1.0) * ka_ref[...])
        nkk = -kkn
        kka = kkn * a
        for q in range(PAIRS):
            sl = slice(q * LANES, (q + 1) * LANES)
            sc_ref[0, I_K, q, rows, :] = k2[:, sl]
            sc_ref[0, I_A, q, rows, :] = nkk[:, sl]
            sc_ref[0, I_B, q, rows, :] = kka[:, sl]
        ma_ref[0, rows, :] = (_sigmoid(zga[i]) * pa[i]).astype(ma_ref.dtype)


def _inproj(x, mod, p, tl):
    bsz, seq, _ = x.shape
    grid = (bsz, seq // tl)
    row = lambda b, s: (b, s, 0)
    sc_shape = jax.ShapeDtypeStruct((bsz, 6, PAIRS, seq, LANES), F32)
    sc_spec = pl.BlockSpec((1, 6, PAIRS, tl, LANES), lambda b, s: (b, 0, 0, s, 0))
    consts = [p["win"], p["bin"], p["glnv"], p["blnv"],
              p["ws"], p["bs"], p["mu"], p["w0"], p["a0"], p["wup"], p["kk"], p["ka"], p["bd"],
              p["wba"]]
    return pl.pallas_call(
        _inproj_kernel,
        grid=grid,
        in_specs=[pl.BlockSpec((1, tl, D_MODEL), row),
                  pl.BlockSpec((1, 6, D_MODEL), lambda b, s: (b, 0, 0))]
                 + [_const_spec(c.shape) for c in consts],
        out_specs=[pl.BlockSpec((1, tl, D_MODEL), row),
                   pl.BlockSpec((1, tl, D_MODEL), row)]
                  + [sc_spec]
                  + [pl.BlockSpec((1, tl, R_WIDTH), row)],
        out_shape=[jax.ShapeDtypeStruct((bsz, seq, D_MODEL), BF16),
                   jax.ShapeDtypeStruct((bsz, seq, D_MODEL), BF16)]
                  + [sc_shape]
                  + [jax.ShapeDtypeStruct((bsz, seq, R_WIDTH), BF16)],
        scratch_shapes=[pltpu.VMEM((tl + 8, R_COLS), F32)],
        compiler_params=pltpu.CompilerParams(
            dimension_semantics=("arbitrary", "arbitrary"), vmem_limit_bytes=VMEM_LIMIT),
        name="inproj",
    )(x, mod, *consts)


def _scan_kernel(sc_ref, rk_ref, gng_ref, gnb_ref, y_ref, st_ref):
    bt, _, pp, cs, _ = sc_ref.shape

    @pl.when(pl.program_id(0) == 0)
    def _():
        st_ref[...] = jnp.zeros(st_ref.shape, F32)

    lane = lax.broadcasted_iota(jnp.int32, (cs, LANES), 1)
    first_head = lane < R_HEAD

    def stack(t):
        return jnp.concatenate([jnp.where(first_head, t, 0.0),
                                jnp.where(first_head, 0.0, t)], axis=0)

    def fold(t):
        return jnp.where(first_head, t[:cs], t[cs:])

    ri = lax.broadcasted_iota(jnp.int32, (cs, cs), 0)
    ci = lax.broadcasted_iota(jnp.int32, (cs, cs), 1)
    tri_ones = jnp.where(ri >= ci, 1.0, 0.0).astype(BF16)
    si = lax.broadcasted_iota(jnp.int32, (cs, LANES), 0)
    sj = lane % cs
    strict = si > sj
    incl = si >= sj
    eye = jnp.where(si == sj, 1.0, 0.0).astype(F32)
    bsi = lax.broadcasted_iota(jnp.int32, (LANES, LANES), 0) // R_HEAD
    bsj = lax.broadcasted_iota(jnp.int32, (LANES, LANES), 1) // R_HEAD
    head_ones = jnp.where(bsi == bsj, 1.0, 0.0).astype(BF16)
    head_mean = jnp.where(bsi == bsj, 1.0 / R_HEAD, 0.0).astype(BF16)

    chains = [(b, q) for q in range(pp) for b in range(bt)]
    n_sq = (cs - 1).bit_length() - 1
    cstate = [dict(j=j, b=b, q=q) for j, (b, q) in enumerate(chains)]

    def s_cumsum(c):
        lw = sc_ref[c["b"], I_LW, c["q"]]
        p1 = lw.astype(BF16)
        p2 = (lw - p1.astype(F32)).astype(BF16)
        c["lw"] = lw
        c["cum"] = _dot(tri_ones, p1) + _dot(tri_ones, p2)

    def s_operands(c):
        b, q, cum, lw = c["b"], c["q"], c["cum"], c["lw"]
        last = cum[cs - 1:cs, :]
        e_pos = jnp.exp(cum)
        e_pos_ex = jnp.exp(cum - lw)
        e_neg = 1.0 / e_pos
        wcol = jnp.broadcast_to(jnp.exp(last), (LANES, LANES)).T
        r = sc_ref[b, I_R, q]
        k = sc_ref[b, I_K, q]
        bb = sc_ref[b, I_B, q]
        c["at"] = sc_ref[b, I_A, q] * e_pos_ex
        c["rt_b"] = (r * e_pos).astype(BF16)
        tb = stack(bb * e_neg).T
        tk = stack(k * e_neg).T
        c["vm"] = stack(sc_ref[b, I_V, q]).astype(BF16)
        c["bt"] = fold(tb).astype(BF16)
        c["kt"] = fold(tk)
        c["wm"] = fold(wcol)
        c["aa"] = _dot(jnp.concatenate([c["at"].astype(BF16), c["rt_b"]], axis=0),
                       jnp.concatenate([tb, tk], axis=1).astype(BF16))
        lanes = slice(q * LANES, (q + 1) * LANES)
        c["rkr"] = (r * k * rk_ref[:, lanes]).astype(BF16)

    def s_intra(c):
        aa = c.pop("aa")
        a_ab = jnp.where(strict, aa[:cs, :LANES], 0.0)
        c["a_rb"] = jnp.where(incl, aa[cs:, :LANES], 0.0).astype(BF16)
        c["akv_y0"] = _dot(jnp.concatenate([jnp.where(strict, aa[:cs, LANES:], 0.0),
                                            jnp.where(incl, aa[cs:, LANES:], 0.0),
                                            c.pop("kt")], axis=0).astype(BF16), c.pop("vm"))
        c["xk"] = _dot(a_ab.astype(BF16), stack(a_ab).astype(BF16))
        c["tk"] = eye + a_ab

    def s_square(c):
        x_bd = stack(c["xk"]).astype(BF16)
        xt = _dot(jnp.concatenate([c["xk"], c["tk"]], axis=0).astype(BF16), x_bd)
        c["xk"] = xt[:cs]
        c["tk"] = c["tk"] + xt[cs:]

    def s_solve(c):
        x_bd = stack(c.pop("xk")).astype(BF16)
        tk = c["tk"] + _dot(c["tk"].astype(BF16), x_bd)
        c["pq"] = _dot(tk.astype(BF16),
                       jnp.concatenate([stack(c.pop("at")), stack(c["akv_y0"][:cs])],
                                       axis=1).astype(BF16))
        del c["tk"]

    def s_u(c):
        c["st"] = st_ref[c["j"]]
        pq = c.pop("pq")
        ps = _dot(jnp.concatenate([pq[:, :LANES].astype(BF16), c.pop("rt_b")], axis=0),
                  stack(c["st"]).astype(BF16))
        c["u_b"] = stack(ps[:cs] + pq[:, LANES:]).astype(BF16)
        c["rs"] = ps[cs:]

    def s_y(c):
        au = _dot(jnp.concatenate([c.pop("a_rb"), c.pop("bt")], axis=0), c.pop("u_b"))
        akv_y0 = c.pop("akv_y0")
        c["y"] = c.pop("rs") + au[:cs] + akv_y0[cs:2 * cs]
        st_ref[c["j"]] = c.pop("wm") * (c.pop("st") + au[cs:] + akv_y0[2 * cs:])

    for stage in [s_cumsum, s_operands, s_intra]:
        for c in cstate:
            stage(c)
    rk_sum = _dot(jnp.concatenate([c.pop("rkr") for c in cstate], axis=0), head_ones)
    for stage in [s_square] * (n_sq - 1) + [s_solve, s_u]:
        for c in cstate:
            stage(c)

    gsz = max(1, len(cstate) // GN_BATCHES)
    batches = [cstate[i:i + gsz] for i in range(0, len(cstate), gsz)]
    d = {}
    for t in range(len(batches) + 2):
        if t < len(batches):
            for c in batches[t]:
                s_y(c)
        if 0 <= t - 1 < len(batches):
            y_b = jnp.concatenate([c.pop("y") for c in batches[t - 1]], axis=0)
            d[t - 1] = y_b - _dot(y_b.astype(BF16), head_mean)
        if 0 <= t - 2 < len(batches):
            d_b = d.pop(t - 2)
            yn = d_b * lax.rsqrt(_dot((d_b * d_b).astype(BF16), head_mean) + GN_EPS)
            for i, c in enumerate(batches[t - 2]):
                b, q, j = c["b"], c["q"], c["j"]
                lanes = slice(q * LANES, (q + 1) * LANES)
                y_ref[b, q] = (yn[i * cs:(i + 1) * cs] * gng_ref[:, lanes] + gnb_ref[:, lanes]
                               + rk_sum[j * cs:(j + 1) * cs] * sc_ref[b, I_V, q]).astype(y_ref.dtype)


def _scan(sc, rk, gng, gnb):
    bsz, _, pp, seq, _ = sc.shape
    blk = pl.BlockSpec((bsz, pp, SCAN_CHUNK, LANES), lambda s: (0, 0, s, 0))
    vec = pl.BlockSpec((1, pp * LANES), lambda s: (0, 0))
    return pl.pallas_call(
        _scan_kernel,
        grid=(seq // SCAN_CHUNK,),
        in_specs=[pl.BlockSpec((bsz, 6, pp, SCAN_CHUNK, LANES), lambda s: (0, 0, 0, s, 0))] + [vec] * 3,
        out_specs=blk,
        out_shape=jax.ShapeDtypeStruct((bsz, pp, seq, LANES), BF16),
        scratch_shapes=[pltpu.VMEM((bsz * pp, SCAN_CHUNK, LANES), F32)],
        compiler_params=pltpu.CompilerParams(
            dimension_semantics=("arbitrary",), vmem_limit_bytes=VMEM_LIMIT),
        name="scan",
    )(sc, rk, gng, gnb)


def _post_kernel(x_ref, mod_ref, yb_ref, g_ref, gb_ref, ma_ref, wbb_ref, wout_ref, bout_ref,
                 ln1g_ref, ln1b_ref, w1_ref, b1_ref, w2_ref, b2_ref, ln2g_ref, ln2b_ref, o_ref):
    tl = x_ref.shape[1]
    gt1 = mod_ref[0, 2:3, :]
    sh2 = mod_ref[0, 3:4, :]
    sc2 = mod_ref[0, 4:5, :]
    gt2 = mod_ref[0, 5:6, :]
    groups = [slice(i * tl // POST_GROUPS, (i + 1) * tl // POST_GROUPS) for i in range(POST_GROUPS)]
    pb = []
    for rows in groups:
        yb = (jnp.concatenate([yb_ref[0, q, rows, :] for q in range(PAIRS)], axis=1).astype(F32)
              * g_ref[0, rows, :].astype(F32))
        pb.append(_dot(yb.astype(BF16), wbb_ref[...]))
    h1 = []
    for i, rows in enumerate(groups):
        merged = ma_ref[0, rows, :].astype(F32) + gb_ref[0, rows, :].astype(F32) * pb[i]
        mix = _dot(merged.astype(BF16), wout_ref[...]) + bout_ref[...]
        h1.append(_layer_norm(ALPHA * x_ref[0, rows, :] + gt1 * mix,
                              ln1g_ref[...], ln1b_ref[...], LN_EPS))
    t = []
    for i in range(POST_GROUPS):
        h = (h1[i] * (1.0 + sc2) + sh2).astype(BF16)
        t.append(jnp.maximum(_dot(h, w1_ref[...]) + b1_ref[...], 0.0))
    for i, rows in enumerate(groups):
        ff = _dot((t[i] * t[i]).astype(BF16), w2_ref[...]) + b2_ref[...]
        o_ref[0, rows, :] = _layer_norm(ALPHA * h1[i] + gt2 * ff,
                                        ln2g_ref[...], ln2b_ref[...], LN_EPS).astype(o_ref.dtype)


def _post(x, mod, yb, g, gb, ma, p, tl, out_dtype):
    bsz, seq, _ = x.shape
    row = lambda b, s: (b, s, 0)
    consts = [p["wbb"], p["wout"], p["bout"], p["ln1g"], p["ln1b"],
              p["w1"], p["b1"], p["w2"], p["b2"], p["ln2g"], p["ln2b"]]
    return pl.pallas_call(
        _post_kernel,
        grid=(bsz, seq // tl),
        in_specs=[pl.BlockSpec((1, tl, D_MODEL), row),
                  pl.BlockSpec((1, 6, D_MODEL), lambda b, s: (b, 0, 0)),
                  pl.BlockSpec((1, PAIRS, tl, LANES), lambda b, s: (b, 0, s, 0)),
                  pl.BlockSpec((1, tl, R_WIDTH), row),
                  pl.BlockSpec((1, tl, D_MODEL), row),
                  pl.BlockSpec((1, tl, D_MODEL), row)]
                 + [_const_spec(c.shape) for c in consts],
        out_specs=pl.BlockSpec((1, tl, D_MODEL), row),
        out_shape=jax.ShapeDtypeStruct((bsz, seq, D_MODEL), out_dtype),
        compiler_params=pltpu.CompilerParams(
            dimension_semantics=("arbitrary", "arbitrary"), vmem_limit_bytes=VMEM_LIMIT),
        name="post",
    )(x, mod, yb, g, gb, ma, *consts)


W_IN_SPLIT = 2 * G_WIDTH + 3 * R_WIDTH + LORA_COLS
WPREP_COLS = 2 * LANES
GAP_BLOCK = W_IN_SPLIT // WPREP_COLS


def _relayout_w_in_kernel(wt_ref, o_ref):
    j = pl.program_id(0)
    rows = lax.broadcasted_iota(jnp.int32, wt_ref.shape, 0)
    keep = jnp.logical_or(j != GAP_BLOCK, rows < W_IN_SPLIT % WPREP_COLS)
    o_ref[...] = jnp.where(keep, wt_ref[...], 0.0).T.astype(BF16)


def _relayout_w_in(w_in, l):
    wt = jnp.transpose(w_in[l])
    gap = COLS_R.stop - W_IN_SPLIT

    unit = math.gcd(WPREP_COLS, gap)

    def src_row(j):
        return (j * (WPREP_COLS // unit) - jnp.where(j <= GAP_BLOCK, 0, gap // unit)) * unit

    return pl.pallas_call(
        _relayout_w_in_kernel,
        grid=(COLS_GB.stop // WPREP_COLS,),
        in_specs=[pl.BlockSpec((pl.Element(WPREP_COLS), pl.Element(D_MODEL)),
                               lambda j: (src_row(j), 0))],
        out_specs=pl.BlockSpec((D_MODEL, WPREP_COLS), lambda j: (0, j)),
        out_shape=jax.ShapeDtypeStruct((D_MODEL, COLS_GB.stop), BF16),
        name="wprep",
    )(wt)


def _prepare_params(l, w_in, b_in, g_ln_v, b_ln_v, w_spatial, b_spatial, mu_shift, w0,
                    w_decay_up, a0, w_aaa_up, w_gate_up, k_k, k_a, r_k, gn_gain, gn_bias,
                    w_branch_a, w_branch_b, w_out, b_out, ln1_g, ln1_b, w_ff1, b_ff1, w_ff2,
                    b_ff2, ln2_g, ln2_b):
    g_end = 2 * G_WIDTH
    rkv_end = g_end + 3 * R_WIDTH
    r_end = rkv_end + LORA_COLS
    pad = LORA_PAD - LORA_COLS
    row2 = lambda t: t.reshape(1, -1)
    wi, bi = w_in[l], b_in[l]
    b_all = jnp.concatenate([bi[:r_end], jnp.zeros((pad,), F32), bi[r_end:]])
    mu = jnp.concatenate([mu_shift[l], jnp.zeros((pad,), F32)])
    wup = jnp.zeros((LORA_PAD, 3 * R_WIDTH), F32)
    wup = wup.at[0:DECAY_LORA, 0:R_WIDTH].set(w_decay_up[l])
    wup = wup.at[DECAY_LORA:DECAY_LORA + AAA_LORA, R_WIDTH:2 * R_WIDTH].set(w_aaa_up[l])
    wup = wup.at[DECAY_LORA + AAA_LORA:LORA_COLS, 2 * R_WIDTH:].set(w_gate_up[l])
    ws = w_spatial[l].reshape(PAIRS, 2, CHUNK, CHUNK).transpose(0, 2, 1, 3).reshape(PAIRS, CHUNK, 2 * CHUNK)
    bs = jnp.repeat(b_spatial[l].T, G_WIDTH // G_GROUPS, axis=1)
    hid = jnp.arange(R_WIDTH) // R_HEAD
    bd = (hid[:, None] == hid[None, :]).astype(BF16)
    return dict(
        win=_relayout_w_in(w_in, l), bin=row2(b_all),
        glnv=row2(g_ln_v[l]), blnv=row2(b_ln_v[l]), ws=ws, bs=bs, mu=row2(mu),
        w0=row2(w0[l]), a0=row2(a0[l]), wup=wup.astype(BF16), kk=row2(k_k[l]), ka=row2(k_a[l]),
        bd=bd, wba=w_branch_a[l].astype(BF16),
        rk=row2(r_k[l]), gng=row2(gn_gain[l]), gnb=row2(gn_bias[l]),
        wbb=w_branch_b[l].astype(BF16), wout=w_out[l].astype(BF16), bout=row2(b_out[l]),
        ln1g=row2(ln1_g[l]), ln1b=row2(ln1_b[l]),
        w1=w_ff1[l].astype(BF16), b1=row2(b_ff1[l]), w2=w_ff2[l].astype(BF16), b2=row2(b_ff2[l]),
        ln2g=row2(ln2_g[l]), ln2b=row2(ln2_b[l]),
    )


def _tile(seq, want):
    t = min(want, seq)
    while seq % t:
        t //= 2
    return t


def kernel(x, c, w_ada, b_ada, w_in, b_in, g_ln_v, b_ln_v, w_spatial, b_spatial, mu_shift, w0, w_decay_up, a0, w_aaa_up, w_gate_up, k_k, k_a, r_k, gn_gain, gn_bias, w_branch_a, w_branch_b, w_out, b_out, ln1_g, ln1_b, w_ff1, b_ff1, w_ff2, b_ff2, ln2_g, ln2_b):
    bsz, seq, _ = x.shape
    assert seq % CHUNK == 0 and x.shape[2] == D_MODEL
    out_dtype = x.dtype
    h_res = x.astype(F32)
    tl = _tile(seq, 512)
    for l in range(DEPTH):
        p = _prepare_params(l, w_in, b_in, g_ln_v, b_ln_v, w_spatial, b_spatial, mu_shift, w0,
                            w_decay_up, a0, w_aaa_up, w_gate_up, k_k, k_a, r_k, gn_gain, gn_bias,
                            w_branch_a, w_branch_b, w_out, b_out, ln1_g, ln1_b, w_ff1, b_ff1,
                            w_ff2, b_ff2, ln2_g, ln2_b)
        mod = _modulation(c.astype(F32), w_ada, b_ada, l).reshape(bsz, 6, D_MODEL)
        ma, gb, sc, g = _inproj(h_res, mod, p, tl)
        yb = _scan(sc, p["rk"], p["gng"], p["gnb"])
        h_res = _post(h_res, mod, yb, g, gb, ma, p, tl, F32)
    return h_res.astype(out_dtype)
```

```python
import math

import jax
import jax.numpy as jnp
from jax import lax
from jax.experimental import pallas as pl
from jax.experimental.pallas import tpu as pltpu

D_MODEL = 1024
G_GROUPS = 8
G_WIDTH = 512
CHUNK = 128
R_WIDTH = 512
R_HEAD = 64
R_HEADS = R_WIDTH // R_HEAD
DECAY_LORA = 32
AAA_LORA = 32
GATE_LORA = 96
LORA_COLS = DECAY_LORA + AAA_LORA + GATE_LORA
D_FF = 4 * D_MODEL
DEPTH = 1
ALPHA = (2.0 * DEPTH) ** 0.25
LN_EPS = 1e-5
GN_EPS = 64e-5
DECAY_SCALE = math.exp(-0.5)

LANES = 128
PAIRS = R_WIDTH // LANES
LORA_PAD = 2 * LANES
R_COLS = 3 * R_WIDTH + LORA_PAD
COLS_U = slice(0, G_WIDTH)
COLS_V = slice(G_WIDTH, 2 * G_WIDTH)
COLS_R = slice(2 * G_WIDTH, 2 * G_WIDTH + R_COLS)
COLS_GA = slice(COLS_R.stop, COLS_R.stop + D_MODEL)
COLS_GB = slice(COLS_GA.stop, COLS_GA.stop + D_MODEL)
SCAN_CHUNK = 64
STACK = 2 * SCAN_CHUNK
POST_GROUPS = 2
IN_GROUPS = 2
I_R, I_LW, I_K, I_V, I_A, I_B = range(6)
GN_BATCHES = 4
SCAN_TILE = 2 * SCAN_CHUNK
SCAN_LAG = 3
VMEM_LIMIT = 56 * 1024 * 1024

F32 = jnp.float32
BF16 = jnp.bfloat16


def _dot(a, b):
    return jnp.dot(a, b, preferred_element_type=F32)


def _sigmoid(x):
    return 1.0 / (1.0 + jnp.exp(-x))


def _gelu_tanh(x):
    c = 0.7978845608028654
    hx = 0.5 * x
    return hx + hx * jnp.tanh(x * (c + (c * 0.044715) * (x * x)))


def _layer_norm(x, g, b, eps):
    mu = jnp.mean(x, axis=-1, keepdims=True)
    d = x - mu
    var = jnp.mean(d * d, axis=-1, keepdims=True)
    return d * lax.rsqrt(var + eps) * g + b


def _const_spec(shape):
    n = len(shape)
    return pl.BlockSpec(shape, lambda *_: (0,) * n)


def _mod_kernel(c_ref, w_ref, b_ref, o_ref):
    c = c_ref[...]
    c_act = c * _sigmoid(c)
    bsz = c.shape[0]
    ch = c_act.astype(BF16)
    cl = (c_act - ch.astype(F32)).astype(BF16)
    w = w_ref[...]
    wh = w.astype(BF16)
    wl = (w - wh.astype(F32)).astype(BF16)
    hh = _dot(jnp.concatenate([ch, cl], axis=0), wh)
    o_ref[...] = hh[:bsz] + hh[bsz:] + _dot(ch, wl) + b_ref[...]


def _modulation(c, w_ada, b_ada, l):
    bsz = c.shape[0]
    n = w_ada.shape[2]
    tn = D_MODEL
    return pl.pallas_call(
        _mod_kernel,
        grid=(n // tn,),
        in_specs=[pl.BlockSpec((bsz, D_MODEL), lambda j: (0, 0)),
                  pl.BlockSpec((None, D_MODEL, tn), lambda j: (l, 0, j)),
                  pl.BlockSpec((None, 1, tn), lambda j: (l, 0, j))],
        out_specs=pl.BlockSpec((bsz, tn), lambda j: (0, j)),
        out_shape=jax.ShapeDtypeStruct((bsz, n), F32),
        name="mod",
    )(c, w_ada, b_ada.reshape(b_ada.shape[0], 1, n))


def _inproj_kernel(x_ref, mod_ref, win_ref, bin_ref,
                   glnv_ref, blnv_ref, ws_ref, bs_ref, mu_ref, w0_ref, a0_ref, wup_ref,
                   kk_ref, ka_ref, bd_ref, wba_ref,
                   ma_ref, gb_ref, sc_ref, g_ref,
                   zsh_ref):
    tl = x_ref.shape[1]

    @pl.when(pl.program_id(1) == 0)
    def _():
        zsh_ref[0:8, :] = jnp.zeros((8, R_COLS), F32)

    sh1 = mod_ref[0, 0:1, :]
    sc1 = mod_ref[0, 1:2, :]
    gsz = tl // IN_GROUPS
    groups = [slice(i * gsz, (i + 1) * gsz) for i in range(IN_GROUPS)]
    gi = range(IN_GROUPS)
    h = [(x_ref[0, rows, :] * (1.0 + sc1) + sh1).astype(BF16) for rows in groups]
    def proj(i, cols):
        return _dot(h[i], win_ref[:, cols]) + bin_ref[:, cols]

    zr = [proj(i, COLS_R) for i in gi]
    for i, rows in enumerate(groups):
        zsh_ref[8 + rows.start:8 + rows.stop, :] = zr[i]
    zu = [proj(i, COLS_U) for i in gi]
    zv = [proj(i, COLS_V) for i in gi]

    z = []
    for i, rows in enumerate(groups):
        prev = zsh_ref[7 + rows.start:7 + rows.stop, :]
        z.append(zr[i] + (prev - zr[i]) * mu_ref[...])
    zsh_ref[7:8, :] = zr[-1][gsz - 1:gsz, :]
    llane = lax.broadcasted_iota(jnp.int32, (gsz, LORA_PAD), 1)
    lin = []
    for i in gi:
        xl = z[i][:, 3 * R_WIDTH:]
        lin.append(jnp.where(llane < DECAY_LORA, jnp.tanh(xl),
                             jnp.where(llane < DECAY_LORA + AAA_LORA, xl, _sigmoid(xl))).astype(BF16))
    zgb = [proj(i, COLS_GB) for i in gi]
    up = [_dot(lin[i], wup_ref[...]) for i in gi]

    u = [_gelu_tanh(zu[i]) for i in gi]
    v = [_layer_norm(_gelu_tanh(zv[i]), glnv_ref[...], blnv_ref[...], LN_EPS) for i in gi]

    kk, n2 = [], []
    for i, rows in enumerate(groups):
        lw = -DECAY_SCALE * _sigmoid(w0_ref[...] + up[i][:, 0:R_WIDTH])
        g_ref[0, rows, :] = up[i][:, 2 * R_WIDTH:].astype(g_ref.dtype)
        kk.append(z[i][:, R_WIDTH:2 * R_WIDTH] * kk_ref[...])
        n2.append(_dot((kk[i] * kk[i]).astype(BF16), bd_ref[...]))
        for q in range(PAIRS):
            sl = slice(q * LANES, (q + 1) * LANES)
            sc_ref[0, I_LW, q, rows, :] = lw[:, sl]
            sc_ref[0, I_R, q, rows, :] = z[i][:, sl]
            sc_ref[0, I_V, q, rows, :] = z[i][:, 2 * R_WIDTH + q * LANES:2 * R_WIDTH + (q + 1) * LANES]

    lane = lax.broadcasted_iota(jnp.int32, (CHUNK, LANES), 1)
    first_head = lane < R_HEAD
    trow = lax.broadcasted_iota(jnp.int32, (CHUNK, 2 * CHUNK), 0)
    scol = lax.broadcasted_iota(jnp.int32, (CHUNK, 2 * CHUNK), 1) % CHUNK
    causal = trow >= scol
    ws = [jnp.where(causal, ws_ref[q], 0.0).astype(BF16) for q in range(PAIRS)]
    ya = []
    for i in gi:
        ya_rows = []
        for c in range(gsz // CHUNK):
            vc = v[i][c * CHUNK:(c + 1) * CHUNK, :]
            s_parts = []
            for q in range(PAIRS):
                v2 = vc[:, q * LANES:(q + 1) * LANES]
                vm = jnp.concatenate([jnp.where(first_head, v2, 0.0),
                                      jnp.where(first_head, 0.0, v2)], axis=0).astype(BF16)
                s_parts.append(_dot(ws[q], vm))
            s = jnp.concatenate(s_parts, axis=1) + bs_ref[...]
            ya_rows.append(u[i][c * CHUNK:(c + 1) * CHUNK, :] * s)
        ya.append(jnp.concatenate(ya_rows, axis=0).astype(BF16))
    zga = [proj(i, COLS_GA) for i in gi]
    pa = [_dot(ya[i], wba_ref[...]) for i in gi]

    for i, rows in enumerate(groups):
        gb_ref[0, rows, :] = _sigmoid(zgb[i]).astype(gb_ref.dtype)
        a = _sigmoid(a0_ref[...] + up[i][:, R_WIDTH:2 * R_WIDTH])
        kkn = kk[i] * lax.rsqrt(jnp.maximum(n2[i], 1e-24))
        k2 = z[i][:, R_WIDTH:2 * R_WIDTH] * (1.0 + (a - 1.0) * ka_ref[...])
        nkk = -kkn
        kka = kkn * a
        for q in range(PAIRS):
            sl = slice(q * LANES, (q + 1) * LANES)
            sc_ref[0, I_K, q, rows, :] = k2[:, sl]
            sc_ref[0, I_A, q, rows, :] = nkk[:, sl]
            sc_ref[0, I_B, q, rows, :] = kka[:, sl]
        ma_ref[0, rows, :] = (_sigmoid(zga[i]) * pa[i]).astype(ma_ref.dtype)


def _inproj(x, mod, p, tl):
    bsz, seq, _ = x.shape
    grid = (bsz, seq // tl)
    row = lambda b, s: (b, s, 0)
    sc_shape = jax.ShapeDtypeStruct((bsz, 6, PAIRS, seq, LANES), F32)
    sc_spec = pl.BlockSpec((1, 6, PAIRS, tl, LANES), lambda b, s: (b, 0, 0, s, 0))
    consts = [p["win"], p["bin"], p["glnv"], p["blnv"],
              p["ws"], p["bs"], p["mu"], p["w0"], p["a0"], p["wup"], p["kk"], p["ka"], p["bd"],
              p["wba"]]
    return pl.pallas_call(
        _inproj_kernel,
        grid=grid,
        in_specs=[pl.BlockSpec((1, tl, D_MODEL), row),
                  pl.BlockSpec((1, 6, D_MODEL), lambda b, s: (b, 0, 0))]
                 + [_const_spec(c.shape) for c in consts],
        out_specs=[pl.BlockSpec((1, tl, D_MODEL), row),
                   pl.BlockSpec((1, tl, D_MODEL), row)]
                  + [sc_spec]
                  + [pl.BlockSpec((1, tl, R_WIDTH), row)],
        out_shape=[jax.ShapeDtypeStruct((bsz, seq, D_MODEL), BF16),
                   jax.ShapeDtypeStruct((bsz, seq, D_MODEL), BF16)]
                  + [sc_shape]
                  + [jax.ShapeDtypeStruct((bsz, seq, R_WIDTH), BF16)],
        scratch_shapes=[pltpu.VMEM((tl + 8, R_COLS), F32)],
        compiler_params=pltpu.CompilerParams(
            dimension_semantics=("arbitrary", "arbitrary"), vmem_limit_bytes=VMEM_LIMIT),
        name="inproj",
    )(x, mod, *consts)


def _scan_kernel(sc_ref, rk_ref, gng_ref, gnb_ref, y_ref, st_ref):
    bt, _, pp, lt, _ = sc_ref.shape
    cs = SCAN_CHUNK

    @pl.when(pl.program_id(0) == 0)
    def _():
        st_ref[...] = jnp.zeros(st_ref.shape, F32)

    lane = lax.broadcasted_iota(jnp.int32, (cs, LANES), 1)
    first_head = lane < R_HEAD

    def stack(t):
        return jnp.concatenate([jnp.where(first_head, t, 0.0),
                                jnp.where(first_head, 0.0, t)], axis=0)

    def fold(t):
        return jnp.where(first_head, t[:cs], t[cs:])

    ri = lax.broadcasted_iota(jnp.int32, (cs, cs), 0)
    ci = lax.broadcasted_iota(jnp.int32, (cs, cs), 1)
    tri_ones = jnp.where(ri >= ci, 1.0, 0.0).astype(BF16)
    si = lax.broadcasted_iota(jnp.int32, (cs, LANES), 0)
    sj = lane % cs
    strict = si > sj
    incl = si >= sj
    eye = jnp.where(si == sj, 1.0, 0.0).astype(F32)
    bsi = lax.broadcasted_iota(jnp.int32, (LANES, LANES), 0) // R_HEAD
    bsj = lax.broadcasted_iota(jnp.int32, (LANES, LANES), 1) // R_HEAD
    head_ones = jnp.where(bsi == bsj, 1.0, 0.0).astype(BF16)
    head_mean = jnp.where(bsi == bsj, 1.0 / R_HEAD, 0.0).astype(BF16)

    chains = [(b, q) for q in range(pp) for b in range(bt)]
    n_sq = (cs - 1).bit_length() - 1
    chunks = [[dict(j=j, b=b, q=q, rows=slice(ck * cs, (ck + 1) * cs))
               for j, (b, q) in enumerate(chains)] for ck in range(lt // cs)]

    def s_cumsum(c):
        lw = sc_ref[c["b"], I_LW, c["q"], c["rows"], :]
        p1 = lw.astype(BF16)
        p2 = (lw - p1.astype(F32)).astype(BF16)
        c["lw"] = lw
        c["cum"] = _dot(tri_ones, p1) + _dot(tri_ones, p2)

    def s_operands(c):
        b, q, rows, cum, lw = c["b"], c["q"], c["rows"], c["cum"], c["lw"]
        last = cum[cs - 1:cs, :]
        e_pos = jnp.exp(cum)
        e_pos_ex = jnp.exp(cum - lw)
        e_neg = 1.0 / e_pos
        wcol = jnp.broadcast_to(jnp.exp(last), (LANES, LANES)).T
        r = sc_ref[b, I_R, q, rows, :]
        k = sc_ref[b, I_K, q, rows, :]
        bb = sc_ref[b, I_B, q, rows, :]
        c["at"] = sc_ref[b, I_A, q, rows, :] * e_pos_ex
        c["rt_b"] = (r * e_pos).astype(BF16)
        tb = stack(bb * e_neg).T
        tk = stack(k * e_neg).T
        c["vm"] = stack(sc_ref[b, I_V, q, rows, :]).astype(BF16)
        c["bt"] = fold(tb).astype(BF16)
        c["kt"] = fold(tk)
        c["wm"] = fold(wcol)
        c["aa"] = _dot(jnp.concatenate([c["at"].astype(BF16), c["rt_b"]], axis=0),
                       jnp.concatenate([tb, tk], axis=1).astype(BF16))
        lanes = slice(q * LANES, (q + 1) * LANES)
        c["rkr"] = (r * k * rk_ref[:, lanes]).astype(BF16)

    def s_intra(c):
        aa = c.pop("aa")
        a_ab = jnp.where(strict, aa[:cs, :LANES], 0.0)
        c["a_rb"] = jnp.where(incl, aa[cs:, :LANES], 0.0).astype(BF16)
        c["akv_y0"] = _dot(jnp.concatenate([jnp.where(strict, aa[:cs, LANES:], 0.0),
                                            jnp.where(incl, aa[cs:, LANES:], 0.0),
                                            c.pop("kt")], axis=0).astype(BF16), c.pop("vm"))
        c["xk"] = _dot(a_ab.astype(BF16), stack(a_ab).astype(BF16))
        c["tk"] = eye + a_ab

    def s_square(c):
        x_bd = stack(c["xk"]).astype(BF16)
        xt = _dot(jnp.concatenate([c["xk"], c["tk"]], axis=0).astype(BF16), x_bd)
        c["xk"] = xt[:cs]
        c["tk"] = c["tk"] + xt[cs:]

    def s_solve(c):
        x_bd = stack(c.pop("xk")).astype(BF16)
        tk = c["tk"] + _dot(c["tk"].astype(BF16), x_bd)
        c["pq"] = _dot(tk.astype(BF16),
                       jnp.concatenate([stack(c.pop("at")), stack(c["akv_y0"][:cs])],
                                       axis=1).astype(BF16))
        del c["tk"]

    def s_u(c):
        c["st"] = st_ref[c["j"]]
        pq = c.pop("pq")
        ps = _dot(jnp.concatenate([pq[:, :LANES].astype(BF16), c.pop("rt_b")], axis=0),
                  stack(c["st"]).astype(BF16))
        c["u_b"] = stack(ps[:cs] + pq[:, LANES:]).astype(BF16)
        c["rs"] = ps[cs:]

    def s_y(c):
        au = _dot(jnp.concatenate([c.pop("a_rb"), c.pop("bt")], axis=0), c.pop("u_b"))
        akv_y0 = c.pop("akv_y0")
        c["y"] = c.pop("rs") + au[:cs] + akv_y0[cs:2 * cs]
        st_ref[c["j"]] = c.pop("wm") * (c.pop("st") + au[cs:] + akv_y0[2 * cs:])

    def s_bonus(cstate):
        rk_sum = _dot(jnp.concatenate([c.pop("rkr") for c in cstate], axis=0), head_ones)
        for c in cstate:
            c["rk_sum"] = rk_sum[c["j"] * cs:(c["j"] + 1) * cs]

    def per_chain(stage):
        def run(cstate):
            for c in cstate:
                stage(c)
        return run

    def finish(cstate):
        per_chain(s_u)(cstate)
        gsz = max(1, len(cstate) // GN_BATCHES)
        batches = [cstate[i:i + gsz] for i in range(0, len(cstate), gsz)]
        d = {}
        for t in range(len(batches) + 2):
            if t < len(batches):
                per_chain(s_y)(batches[t])
            if 0 <= t - 1 < len(batches):
                y_b = jnp.concatenate([c.pop("y") for c in batches[t - 1]], axis=0)
                d[t - 1] = y_b - _dot(y_b.astype(BF16), head_mean)
            if 0 <= t - 2 < len(batches):
                d_b = d.pop(t - 2)
                yn = d_b * lax.rsqrt(_dot((d_b * d_b).astype(BF16), head_mean) + GN_EPS)
                for i, c in enumerate(batches[t - 2]):
                    b, q, rows = c["b"], c["q"], c["rows"]
                    lanes = slice(q * LANES, (q + 1) * LANES)
                    y_ref[b, q, rows, :] = (
                        yn[i * cs:(i + 1) * cs] * gng_ref[:, lanes] + gnb_ref[:, lanes]
                        + c.pop("rk_sum") * sc_ref[b, I_V, q, rows, :]).astype(y_ref.dtype)

    pre = ([per_chain(s_cumsum), per_chain(s_operands), per_chain(s_intra), s_bonus]
           + [per_chain(s_square)] * (n_sq - 1) + [per_chain(s_solve)])
    for t in range(len(pre) + SCAN_LAG * (len(chunks) - 1)):
        for ck, cstate in enumerate(chunks):
            k = t - ck * SCAN_LAG
            if 0 <= k < len(pre):
                pre[k](cstate)
                if k == len(pre) - 1:
                    finish(cstate)


def _scan(sc, rk, gng, gnb):
    bsz, _, pp, seq, _ = sc.shape
    blk = pl.BlockSpec((bsz, pp, SCAN_TILE, LANES), lambda s: (0, 0, s, 0))
    vec = pl.BlockSpec((1, pp * LANES), lambda s: (0, 0))
    return pl.pallas_call(
        _scan_kernel,
        grid=(seq // SCAN_TILE,),
        in_specs=[pl.BlockSpec((bsz, 6, pp, SCAN_TILE, LANES), lambda s: (0, 0, 0, s, 0))] + [vec] * 3,
        out_specs=blk,
        out_shape=jax.ShapeDtypeStruct((bsz, pp, seq, LANES), BF16),
        scratch_shapes=[pltpu.VMEM((bsz * pp, SCAN_CHUNK, LANES), F32)],
        compiler_params=pltpu.CompilerParams(
            dimension_semantics=("arbitrary",), vmem_limit_bytes=VMEM_LIMIT),
        name="scan",
    )(sc, rk, gng, gnb)


def _post_kernel(x_ref, mod_ref, yb_ref, g_ref, gb_ref, ma_ref, wbb_ref, wout_ref, bout_ref,
                 ln1g_ref, ln1b_ref, w1_ref, b1_ref, w2_ref, b2_ref, ln2g_ref, ln2b_ref, o_ref):
    tl = x_ref.shape[1]
    gt1 = mod_ref[0, 2:3, :]
    sh2 = mod_ref[0, 3:4, :]
    sc2 = mod_ref[0, 4:5, :]
    gt2 = mod_ref[0, 5:6, :]
    groups = [slice(i * tl // POST_GROUPS, (i + 1) * tl // POST_GROUPS) for i in range(POST_GROUPS)]
    pb = []
    for rows in groups:
        yb = (jnp.concatenate([yb_ref[0, q, rows, :] for q in range(PAIRS)], axis=1).astype(F32)
              * g_ref[0, rows, :].astype(F32))
        pb.append(_dot(yb.astype(BF16), wbb_ref[...]))
    h1 = []
    for i, rows in enumerate(groups):
        merged = ma_ref[0, rows, :].astype(F32) + gb_ref[0, rows, :].astype(F32) * pb[i]
        mix = _dot(merged.astype(BF16), wout_ref[...]) + bout_ref[...]
        h1.append(_layer_norm(ALPHA * x_ref[0, rows, :] + gt1 * mix,
                              ln1g_ref[...], ln1b_ref[...], LN_EPS))
    t = []
    for i in range(POST_GROUPS):
        h = (h1[i] * (1.0 + sc2) + sh2).astype(BF16)
        t.append(jnp.maximum(_dot(h, w1_ref[...]) + b1_ref[...], 0.0))
    for i, rows in enumerate(groups):
        ff = _dot((t[i] * t[i]).astype(BF16), w2_ref[...]) + b2_ref[...]
        o_ref[0, rows, :] = _layer_norm(ALPHA * h1[i] + gt2 * ff,
                                        ln2g_ref[...], ln2b_ref[...], LN_EPS).astype(o_ref.dtype)


def _post(x, mod, yb, g, gb, ma, p, tl, out_dtype):
    bsz, seq, _ = x.shape
    row = lambda b, s: (b, s, 0)
    consts = [p["wbb"], p["wout"], p["bout"], p["ln1g"], p["ln1b"],
              p["w1"], p["b1"], p["w2"], p["b2"], p["ln2g"], p["ln2b"]]
    return pl.pallas_call(
        _post_kernel,
        grid=(bsz, seq // tl),
        in_specs=[pl.BlockSpec((1, tl, D_MODEL), row),
                  pl.BlockSpec((1, 6, D_MODEL), lambda b, s: (b, 0, 0)),
                  pl.BlockSpec((1, PAIRS, tl, LANES), lambda b, s: (b, 0, s, 0)),
                  pl.BlockSpec((1, tl, R_WIDTH), row),
                  pl.BlockSpec((1, tl, D_MODEL), row),
                  pl.BlockSpec((1, tl, D_MODEL), row)]
                 + [_const_spec(c.shape) for c in consts],
        out_specs=pl.BlockSpec((1, tl, D_MODEL), row),
        out_shape=jax.ShapeDtypeStruct((bsz, seq, D_MODEL), out_dtype),
        compiler_params=pltpu.CompilerParams(
            dimension_semantics=("arbitrary", "arbitrary"), vmem_limit_bytes=VMEM_LIMIT),
        name="post",
    )(x, mod, yb, g, gb, ma, *consts)


W_IN_SPLIT = 2 * G_WIDTH + 3 * R_WIDTH + LORA_COLS
WPREP_COLS = 2 * LANES
GAP_BLOCK = W_IN_SPLIT // WPREP_COLS


def _relayout_w_in_kernel(wt_ref, o_ref):
    j = pl.program_id(0)
    rows = lax.broadcasted_iota(jnp.int32, wt_ref.shape, 0)
    keep = jnp.logical_or(j != GAP_BLOCK, rows < W_IN_SPLIT % WPREP_COLS)
    o_ref[...] = jnp.where(keep, wt_ref[...], 0.0).T.astype(BF16)


def _relayout_w_in(w_in, l):
    wt = jnp.transpose(w_in[l])
    gap = COLS_R.stop - W_IN_SPLIT

    unit = math.gcd(WPREP_COLS, gap)

    def src_row(j):
        return (j * (WPREP_COLS // unit) - jnp.where(j <= GAP_BLOCK, 0, gap // unit)) * unit

    return pl.pallas_call(
        _relayout_w_in_kernel,
        grid=(COLS_GB.stop // WPREP_COLS,),
        in_specs=[pl.BlockSpec((pl.Element(WPREP_COLS), pl.Element(D_MODEL)),
                               lambda j: (src_row(j), 0))],
        out_specs=pl.BlockSpec((D_MODEL, WPREP_COLS), lambda j: (0, j)),
        out_shape=jax.ShapeDtypeStruct((D_MODEL, COLS_GB.stop), BF16),
        name="wprep",
    )(wt)


def _prepare_params(l, w_in, b_in, g_ln_v, b_ln_v, w_spatial, b_spatial, mu_shift, w0,
                    w_decay_up, a0, w_aaa_up, w_gate_up, k_k, k_a, r_k, gn_gain, gn_bias,
                    w_branch_a, w_branch_b, w_out, b_out, ln1_g, ln1_b, w_ff1, b_ff1, w_ff2,
                    b_ff2, ln2_g, ln2_b):
    g_end = 2 * G_WIDTH
    rkv_end = g_end + 3 * R_WIDTH
    r_end = rkv_end + LORA_COLS
    pad = LORA_PAD - LORA_COLS
    row2 = lambda t: t.reshape(1, -1)
    wi, bi = w_in[l], b_in[l]
    b_all = jnp.concatenate([bi[:r_end], jnp.zeros((pad,), F32), bi[r_end:]])
    mu = jnp.concatenate([mu_shift[l], jnp.zeros((pad,), F32)])
    wup = jnp.zeros((LORA_PAD, 3 * R_WIDTH), F32)
    wup = wup.at[0:DECAY_LORA, 0:R_WIDTH].set(w_decay_up[l])
    wup = wup.at[DECAY_LORA:DECAY_LORA + AAA_LORA, R_WIDTH:2 * R_WIDTH].set(w_aaa_up[l])
    wup = wup.at[DECAY_LORA + AAA_LORA:LORA_COLS, 2 * R_WIDTH:].set(w_gate_up[l])
    ws = w_spatial[l].reshape(PAIRS, 2, CHUNK, CHUNK).transpose(0, 2, 1, 3).reshape(PAIRS, CHUNK, 2 * CHUNK)
    bs = jnp.repeat(b_spatial[l].T, G_WIDTH // G_GROUPS, axis=1)
    hid = jnp.arange(R_WIDTH) // R_HEAD
    bd = (hid[:, None] == hid[None, :]).astype(BF16)
    return dict(
        win=_relayout_w_in(w_in, l), bin=row2(b_all),
        glnv=row2(g_ln_v[l]), blnv=row2(b_ln_v[l]), ws=ws, bs=bs, mu=row2(mu),
        w0=row2(w0[l]), a0=row2(a0[l]), wup=wup.astype(BF16), kk=row2(k_k[l]), ka=row2(k_a[l]),
        bd=bd, wba=w_branch_a[l].astype(BF16),
        rk=row2(r_k[l]), gng=row2(gn_gain[l]), gnb=row2(gn_bias[l]),
        wbb=w_branch_b[l].astype(BF16), wout=w_out[l].astype(BF16), bout=row2(b_out[l]),
        ln1g=row2(ln1_g[l]), ln1b=row2(ln1_b[l]),
        w1=w_ff1[l].astype(BF16), b1=row2(b_ff1[l]), w2=w_ff2[l].astype(BF16), b2=row2(b_ff2[l]),
        ln2g=row2(ln2_g[l]), ln2b=row2(ln2_b[l]),
    )


def _tile(seq, want):
    t = min(want, seq)
    while seq % t:
        t //= 2
    return t


def kernel(x, c, w_ada, b_ada, w_in, b_in, g_ln_v, b_ln_v, w_spatial, b_spatial, mu_shift, w0, w_decay_up, a0, w_aaa_up, w_gate_up, k_k, k_a, r_k, gn_gain, gn_bias, w_branch_a, w_branch_b, w_out, b_out, ln1_g, ln1_b, w_ff1, b_ff1, w_ff2, b_ff2, ln2_g, ln2_b):
    bsz, seq, _ = x.shape
    assert seq % CHUNK == 0 and x.shape[2] == D_MODEL
    out_dtype = x.dtype
    h_res = x.astype(F32)
    tl = _tile(seq, 512)
    for l in range(DEPTH):
        p = _prepare_params(l, w_in, b_in, g_ln_v, b_ln_v, w_spatial, b_spatial, mu_shift, w0,
                            w_decay_up, a0, w_aaa_up, w_gate_up, k_k, k_a, r_k, gn_gain, gn_bias,
                            w_branch_a, w_branch_b, w_out, b_out, ln1_g, ln1_b, w_ff1, b_ff1,
                            w_ff2, b_ff2, ln2_g, ln2_b)
        mod = _modulation(c.astype(F32), w_ada, b_ada, l).reshape(bsz, 6, D_MODEL)
        ma, gb, sc, g = _inproj(h_res, mod, p, tl)
        yb = _scan(sc, p["rk"], p["gng"], p["gnb"])
        h_res = _post(h_res, mod, yb, g, gb, ma, p, tl, F32)
    return h_res.astype(out_dtype)
```

```python
import math

import jax
import jax.numpy as jnp
from jax import lax
from jax.experimental import pallas as pl
from jax.experimental.pallas import tpu as pltpu

D_MODEL = 1024
G_GROUPS = 8
G_WIDTH = 512
CHUNK = 128
R_WIDTH = 512
R_HEAD = 64
R_HEADS = R_WIDTH // R_HEAD
DECAY_LORA = 32
AAA_LORA = 32
GATE_LORA = 96
LORA_COLS = DECAY_LORA + AAA_LORA + GATE_LORA
D_FF = 4 * D_MODEL
DEPTH = 1
ALPHA = (2.0 * DEPTH) ** 0.25
LN_EPS = 1e-5
GN_EPS = 64e-5
DECAY_SCALE = math.exp(-0.5)

LANES = 128
PAIRS = R_WIDTH // LANES
LORA_PAD = 2 * LANES
R_COLS = 3 * R_WIDTH + LORA_PAD
COLS_U = slice(0, G_WIDTH)
COLS_V = slice(G_WIDTH, 2 * G_WIDTH)
COLS_R = slice(2 * G_WIDTH, 2 * G_WIDTH + R_COLS)
COLS_GA = slice(COLS_R.stop, COLS_R.stop + D_MODEL)
COLS_GB = slice(COLS_GA.stop, COLS_GA.stop + D_MODEL)
SCAN_CHUNK = 64
STACK = 2 * SCAN_CHUNK
POST_GROUPS = 2
IN_GROUPS = 2
SHIFT_PAD = 16
I_R, I_LW, I_K, I_V, I_A, I_B = range(6)
GN_BATCHES = 4
SCAN_TILE = 2 * SCAN_CHUNK
SCAN_LAG = 3
VMEM_LIMIT = 56 * 1024 * 1024

F32 = jnp.float32
BF16 = jnp.bfloat16


def _dot(a, b):
    return jnp.dot(a, b, preferred_element_type=F32)


def _sigmoid(x):
    return 1.0 / (1.0 + jnp.exp(-x))


def _gelu_tanh(x):
    c = 0.7978845608028654
    hx = 0.5 * x
    return hx + hx * jnp.tanh(x * (c + (c * 0.044715) * (x * x)))


def _layer_norm(x, g, b, eps):
    mu = jnp.mean(x, axis=-1, keepdims=True)
    d = x - mu
    var = jnp.mean(d * d, axis=-1, keepdims=True)
    return d * lax.rsqrt(var + eps) * g + b


def _const_spec(shape):
    n = len(shape)
    return pl.BlockSpec(shape, lambda *_: (0,) * n)


def _mod_kernel(c_ref, w_ref, b_ref, o_ref):
    c = c_ref[...]
    c_act = c * _sigmoid(c)
    bsz = c.shape[0]
    ch = c_act.astype(BF16)
    cl = (c_act - ch.astype(F32)).astype(BF16)
    w = w_ref[...]
    wh = w.astype(BF16)
    wl = (w - wh.astype(F32)).astype(BF16)
    hh = _dot(jnp.concatenate([ch, cl], axis=0), wh)
    o_ref[...] = hh[:bsz] + hh[bsz:] + _dot(ch, wl) + b_ref[...]


def _modulation(c, w_ada, b_ada, l):
    bsz = c.shape[0]
    n = w_ada.shape[2]
    tn = D_MODEL
    return pl.pallas_call(
        _mod_kernel,
        grid=(n // tn,),
        in_specs=[pl.BlockSpec((bsz, D_MODEL), lambda j: (0, 0)),
                  pl.BlockSpec((None, D_MODEL, tn), lambda j: (l, 0, j)),
                  pl.BlockSpec((None, 1, tn), lambda j: (l, 0, j))],
        out_specs=pl.BlockSpec((bsz, tn), lambda j: (0, j)),
        out_shape=jax.ShapeDtypeStruct((bsz, n), F32),
        name="mod",
    )(c, w_ada, b_ada.reshape(b_ada.shape[0], 1, n))


def _inproj_kernel(x_ref, mod_ref, win_ref, bin_ref,
                   glnv_ref, blnv_ref, ws_ref, bs_ref, mu_ref, w0_ref, a0_ref, wup_ref,
                   kk_ref, ka_ref, bd_ref, wba_ref,
                   ma_ref, gb_ref, sc_ref, g_ref,
                   zsh_ref):
    tl = x_ref.shape[1]

    @pl.when(pl.program_id(1) == 0)
    def _():
        zsh_ref[SHIFT_PAD - 8:SHIFT_PAD, :] = jnp.zeros((8, R_COLS), F32)

    sh1 = mod_ref[0, 0:1, :]
    sc1 = mod_ref[0, 1:2, :]
    gsz = tl // IN_GROUPS
    groups = [slice(i * gsz, (i + 1) * gsz) for i in range(IN_GROUPS)]
    gi = range(IN_GROUPS)
    h = [(x_ref[0, rows, :] * (1.0 + sc1) + sh1).astype(BF16) for rows in groups]
    def proj(i, cols):
        return _dot(h[i], win_ref[:, cols]) + bin_ref[:, cols]

    zr = [proj(i, COLS_R) for i in gi]
    for i, rows in enumerate(groups):
        zsh_ref[SHIFT_PAD + rows.start:SHIFT_PAD + rows.stop, :] = zr[i]
    zu = [proj(i, COLS_U) for i in gi]
    zv = [proj(i, COLS_V) for i in gi]

    z = []
    for i, rows in enumerate(groups):
        prev = zsh_ref[SHIFT_PAD - 1 + rows.start:SHIFT_PAD - 1 + rows.stop, :]
        z.append(zr[i] + (prev - zr[i]) * mu_ref[...])
    zsh_ref[SHIFT_PAD - 1:SHIFT_PAD, :] = zr[-1][gsz - 1:gsz, :]
    llane = lax.broadcasted_iota(jnp.int32, (gsz, LORA_PAD), 1)
    lin = []
    for i in gi:
        xl = z[i][:, 3 * R_WIDTH:]
        lin.append(jnp.where(llane < DECAY_LORA, jnp.tanh(xl),
                             jnp.where(llane < DECAY_LORA + AAA_LORA, xl, _sigmoid(xl))).astype(BF16))
    zgb = [proj(i, COLS_GB) for i in gi]
    up = [_dot(lin[i], wup_ref[...]) for i in gi]

    u = [_gelu_tanh(zu[i]) for i in gi]
    v = [_layer_norm(_gelu_tanh(zv[i]), glnv_ref[...], blnv_ref[...], LN_EPS) for i in gi]

    kk, n2 = [], []
    for i, rows in enumerate(groups):
        lw = -DECAY_SCALE * _sigmoid(w0_ref[...] + up[i][:, 0:R_WIDTH])
        g_ref[0, rows, :] = up[i][:, 2 * R_WIDTH:].astype(g_ref.dtype)
        kk.append(z[i][:, R_WIDTH:2 * R_WIDTH] * kk_ref[...])
        n2.append(_dot((kk[i] * kk[i]).astype(BF16), bd_ref[...]))
        for q in range(PAIRS):
            sl = slice(q * LANES, (q + 1) * LANES)
            sc_ref[0, I_LW, q, rows, :] = lw[:, sl]
            sc_ref[0, I_R, q, rows, :] = z[i][:, sl]
            sc_ref[0, I_V, q, rows, :] = z[i][:, 2 * R_WIDTH + q * LANES:2 * R_WIDTH + (q + 1) * LANES]

    lane = lax.broadcasted_iota(jnp.int32, (CHUNK, LANES), 1)
    first_head = lane < R_HEAD
    trow = lax.broadcasted_iota(jnp.int32, (CHUNK, 2 * CHUNK), 0)
    scol = lax.broadcasted_iota(jnp.int32, (CHUNK, 2 * CHUNK), 1) % CHUNK
    causal = trow >= scol
    ws = [jnp.where(causal, ws_ref[q], 0.0).astype(BF16) for q in range(PAIRS)]
    ya = []
    for i in gi:
        ya_rows = []
        for c in range(gsz // CHUNK):
            vc = v[i][c * CHUNK:(c + 1) * CHUNK, :]
            s_parts = []
            for q in range(PAIRS):
                v2 = vc[:, q * LANES:(q + 1) * LANES]
                vm = jnp.concatenate([jnp.where(first_head, v2, 0.0),
                                      jnp.where(first_head, 0.0, v2)], axis=0).astype(BF16)
                s_parts.append(_dot(ws[q], vm))
            s = jnp.concatenate(s_parts, axis=1) + bs_ref[...]
            ya_rows.append(u[i][c * CHUNK:(c + 1) * CHUNK, :] * s)
        ya.append(jnp.concatenate(ya_rows, axis=0).astype(BF16))
    zga = [proj(i, COLS_GA) for i in gi]
    pa = [_dot(ya[i], wba_ref[...]) for i in gi]

    for i, rows in enumerate(groups):
        gb_ref[0, rows, :] = _sigmoid(zgb[i]).astype(gb_ref.dtype)
        a = _sigmoid(a0_ref[...] + up[i][:, R_WIDTH:2 * R_WIDTH])
        kkn = kk[i] * lax.rsqrt(jnp.maximum(n2[i], 1e-24))
        k2 = z[i][:, R_WIDTH:2 * R_WIDTH] * (1.0 + (a - 1.0) * ka_ref[...])
        nkk = -kkn
        kka = kkn * a
        for q in range(PAIRS):
            sl = slice(q * LANES, (q + 1) * LANES)
            sc_ref[0, I_K, q, rows, :] = k2[:, sl]
            sc_ref[0, I_A, q, rows, :] = nkk[:, sl]
            sc_ref[0, I_B, q, rows, :] = kka[:, sl]
        ma_ref[0, rows, :] = (_sigmoid(zga[i]) * pa[i]).astype(ma_ref.dtype)


def _inproj(x, mod, p, tl):
    bsz, seq, _ = x.shape
    grid = (bsz, seq // tl)
    row = lambda b, s: (b, s, 0)
    sc_shape = jax.ShapeDtypeStruct((bsz, 6, PAIRS, seq, LANES), F32)
    sc_spec = pl.BlockSpec((1, 6, PAIRS, tl, LANES), lambda b, s: (b, 0, 0, s, 0))
    consts = [p["win"], p["bin"], p["glnv"], p["blnv"],
              p["ws"], p["bs"], p["mu"], p["w0"], p["a0"], p["wup"], p["kk"], p["ka"], p["bd"],
              p["wba"]]
    return pl.pallas_call(
        _inproj_kernel,
        grid=grid,
        in_specs=[pl.BlockSpec((1, tl, D_MODEL), row),
                  pl.BlockSpec((1, 6, D_MODEL), lambda b, s: (b, 0, 0))]
                 + [_const_spec(c.shape) for c in consts],
        out_specs=[pl.BlockSpec((1, tl, D_MODEL), row),
                   pl.BlockSpec((1, tl, D_MODEL), row)]
                  + [sc_spec]
                  + [pl.BlockSpec((1, tl, R_WIDTH), row)],
        out_shape=[jax.ShapeDtypeStruct((bsz, seq, D_MODEL), BF16),
                   jax.ShapeDtypeStruct((bsz, seq, D_MODEL), BF16)]
                  + [sc_shape]
                  + [jax.ShapeDtypeStruct((bsz, seq, R_WIDTH), BF16)],
        scratch_shapes=[pltpu.VMEM((tl + SHIFT_PAD, R_COLS), F32)],
        compiler_params=pltpu.CompilerParams(
            dimension_semantics=("arbitrary", "arbitrary"), vmem_limit_bytes=VMEM_LIMIT),
        name="inproj",
    )(x, mod, *consts)


def _scan_kernel(sc_ref, rk_ref, gng_ref, gnb_ref, y_ref, st_ref):
    bt, _, pp, lt, _ = sc_ref.shape
    cs = SCAN_CHUNK

    @pl.when(pl.program_id(0) == 0)
    def _():
        st_ref[...] = jnp.zeros(st_ref.shape, F32)

    lane = lax.broadcasted_iota(jnp.int32, (cs, LANES), 1)
    first_head = lane < R_HEAD

    def stack(t):
        return jnp.concatenate([jnp.where(first_head, t, 0.0),
                                jnp.where(first_head, 0.0, t)], axis=0)

    def fold(t):
        return jnp.where(first_head, t[:cs], t[cs:])

    ri = lax.broadcasted_iota(jnp.int32, (cs, cs), 0)
    ci = lax.broadcasted_iota(jnp.int32, (cs, cs), 1)
    tri_ones = jnp.where(ri >= ci, 1.0, 0.0).astype(BF16)
    si = lax.broadcasted_iota(jnp.int32, (cs, LANES), 0)
    sj = lane % cs
    strict = si > sj
    incl = si >= sj
    eye = jnp.where(si == sj, 1.0, 0.0).astype(F32)
    bsi = lax.broadcasted_iota(jnp.int32, (LANES, LANES), 0) // R_HEAD
    bsj = lax.broadcasted_iota(jnp.int32, (LANES, LANES), 1) // R_HEAD
    head_ones = jnp.where(bsi == bsj, 1.0, 0.0).astype(BF16)
    head_mean = jnp.where(bsi == bsj, 1.0 / R_HEAD, 0.0).astype(BF16)

    chains = [(b, q) for q in range(pp) for b in range(bt)]
    n_sq = (cs - 1).bit_length() - 1
    chunks = [[dict(j=j, b=b, q=q, rows=slice(ck * cs, (ck + 1) * cs))
               for j, (b, q) in enumerate(chains)] for ck in range(lt // cs)]

    def s_cumsum(c):
        lw = sc_ref[c["b"], I_LW, c["q"], c["rows"], :]
        p1 = lw.astype(BF16)
        p2 = (lw - p1.astype(F32)).astype(BF16)
        c["lw"] = lw
        c["cum"] = _dot(tri_ones, p1) + _dot(tri_ones, p2)

    def s_operands(c):
        b, q, rows, cum, lw = c["b"], c["q"], c["rows"], c["cum"], c["lw"]
        last = cum[cs - 1:cs, :]
        e_pos = jnp.exp(cum)
        e_pos_ex = jnp.exp(cum - lw)
        e_neg = 1.0 / e_pos
        wcol = jnp.broadcast_to(jnp.exp(last), (LANES, LANES)).T
        r = sc_ref[b, I_R, q, rows, :]
        k = sc_ref[b, I_K, q, rows, :]
        bb = sc_ref[b, I_B, q, rows, :]
        c["at"] = sc_ref[b, I_A, q, rows, :] * e_pos_ex
        c["rt_b"] = (r * e_pos).astype(BF16)
        tb = stack(bb * e_neg).T
        tk = stack(k * e_neg).T
        c["vm"] = stack(sc_ref[b, I_V, q, rows, :]).astype(BF16)
        c["bt"] = fold(tb).astype(BF16)
        c["kt"] = fold(tk)
        c["wm"] = fold(wcol)
        c["aa"] = _dot(jnp.concatenate([c["at"].astype(BF16), c["rt_b"]], axis=0),
                       jnp.concatenate([tb, tk], axis=1).astype(BF16))
        lanes = slice(q * LANES, (q + 1) * LANES)
        c["rkr"] = (r * k * rk_ref[:, lanes]).astype(BF16)

    def s_intra(c):
        aa = c.pop("aa")
        a_ab = jnp.where(strict, aa[:cs, :LANES], 0.0)
        c["a_rb"] = jnp.where(incl, aa[cs:, :LANES], 0.0).astype(BF16)
        c["akv_y0"] = _dot(jnp.concatenate([jnp.where(strict, aa[:cs, LANES:], 0.0),
                                            jnp.where(incl, aa[cs:, LANES:], 0.0),
                                            c.pop("kt")], axis=0).astype(BF16), c.pop("vm"))
        c["xk"] = _dot(a_ab.astype(BF16), stack(a_ab).astype(BF16))
        c["tk"] = eye + a_ab

    def s_square(c):
        x_bd = stack(c["xk"]).astype(BF16)
        xt = _dot(jnp.concatenate([c["xk"], c["tk"]], axis=0).astype(BF16), x_bd)
        c["xk"] = xt[:cs]
        c["tk"] = c["tk"] + xt[cs:]

    def s_solve(c):
        x_bd = stack(c.pop("xk")).astype(BF16)
        tk = c["tk"] + _dot(c["tk"].astype(BF16), x_bd)
        c["pq"] = _dot(tk.astype(BF16),
                       jnp.concatenate([stack(c.pop("at")), stack(c["akv_y0"][:cs])],
                                       axis=1).astype(BF16))
        del c["tk"]

    def s_u(c):
        c["st"] = st_ref[c["j"]]
        pq = c.pop("pq")
        ps = _dot(jnp.concatenate([pq[:, :LANES].astype(BF16), c.pop("rt_b")], axis=0),
                  stack(c["st"]).astype(BF16))
        c["u_b"] = stack(ps[:cs] + pq[:, LANES:]).astype(BF16)
        c["rs"] = ps[cs:]

    def s_y(c):
        au = _dot(jnp.concatenate([c.pop("a_rb"), c.pop("bt")], axis=0), c.pop("u_b"))
        akv_y0 = c.pop("akv_y0")
        c["y"] = c.pop("rs") + au[:cs] + akv_y0[cs:2 * cs]
        st_ref[c["j"]] = c.pop("wm") * (c.pop("st") + au[cs:] + akv_y0[2 * cs:])

    def s_bonus(cstate):
        rk_sum = _dot(jnp.concatenate([c.pop("rkr") for c in cstate], axis=0), head_ones)
        for c in cstate:
            c["rk_sum"] = rk_sum[c["j"] * cs:(c["j"] + 1) * cs]

    def per_chain(stage):
        def run(cstate):
            for c in cstate:
                stage(c)
        return run

    def finish(cstate):
        per_chain(s_u)(cstate)
        gsz = max(1, len(cstate) // GN_BATCHES)
        batches = [cstate[i:i + gsz] for i in range(0, len(cstate), gsz)]
        d = {}
        for t in range(len(batches) + 2):
            if t < len(batches):
                per_chain(s_y)(batches[t])
            if 0 <= t - 1 < len(batches):
                y_b = jnp.concatenate([c.pop("y") for c in batches[t - 1]], axis=0)
                d[t - 1] = y_b - _dot(y_b.astype(BF16), head_mean)
            if 0 <= t - 2 < len(batches):
                d_b = d.pop(t - 2)
                yn = d_b * lax.rsqrt(_dot((d_b * d_b).astype(BF16), head_mean) + GN_EPS)
                for i, c in enumerate(batches[t - 2]):
                    b, q, rows = c["b"], c["q"], c["rows"]
                    lanes = slice(q * LANES, (q + 1) * LANES)
                    y_ref[b, q, rows, :] = (
                        yn[i * cs:(i + 1) * cs] * gng_ref[:, lanes] + gnb_ref[:, lanes]
                        + c.pop("rk_sum") * sc_ref[b, I_V, q, rows, :]).astype(y_ref.dtype)

    pre = ([per_chain(s_cumsum), per_chain(s_operands), per_chain(s_intra), s_bonus]
           + [per_chain(s_square)] * (n_sq - 1) + [per_chain(s_solve)])
    for t in range(len(pre) + SCAN_LAG * (len(chunks) - 1)):
        for ck, cstate in enumerate(chunks):
            k = t - ck * SCAN_LAG
            if 0 <= k < len(pre):
                pre[k](cstate)
                if k == len(pre) - 1:
                    finish(cstate)


def _scan(sc, rk, gng, gnb):
    bsz, _, pp, seq, _ = sc.shape
    blk = pl.BlockSpec((bsz, pp, SCAN_TILE, LANES), lambda s: (0, 0, s, 0))
    vec = pl.BlockSpec((1, pp * LANES), lambda s: (0, 0))
    return pl.pallas_call(
        _scan_kernel,
        grid=(seq // SCAN_TILE,),
        in_specs=[pl.BlockSpec((bsz, 6, pp, SCAN_TILE, LANES), lambda s: (0, 0, 0, s, 0))] + [vec] * 3,
        out_specs=blk,
        out_shape=jax.ShapeDtypeStruct((bsz, pp, seq, LANES), BF16),
        scratch_shapes=[pltpu.VMEM((bsz * pp, SCAN_CHUNK, LANES), F32)],
        compiler_params=pltpu.CompilerParams(
            dimension_semantics=("arbitrary",), vmem_limit_bytes=VMEM_LIMIT),
        name="scan",
    )(sc, rk, gng, gnb)


def _post_kernel(x_ref, mod_ref, yb_ref, g_ref, gb_ref, ma_ref, wbb_ref, wout_ref, bout_ref,
                 ln1g_ref, ln1b_ref, w1_ref, b1_ref, w2_ref, b2_ref, ln2g_ref, ln2b_ref, o_ref):
    tl = x_ref.shape[1]
    gt1 = mod_ref[0, 2:3, :]
    sh2 = mod_ref[0, 3:4, :]
    sc2 = mod_ref[0, 4:5, :]
    gt2 = mod_ref[0, 5:6, :]
    groups = [slice(i * tl // POST_GROUPS, (i + 1) * tl // POST_GROUPS) for i in range(POST_GROUPS)]
    pb = []
    for rows in groups:
        yb = (jnp.concatenate([yb_ref[0, q, rows, :] for q in range(PAIRS)], axis=1).astype(F32)
              * g_ref[0, rows, :].astype(F32))
        pb.append(_dot(yb.astype(BF16), wbb_ref[...]))
    h1 = []
    for i, rows in enumerate(groups):
        merged = ma_ref[0, rows, :].astype(F32) + gb_ref[0, rows, :].astype(F32) * pb[i]
        mix = _dot(merged.astype(BF16), wout_ref[...]) + bout_ref[...]
        h1.append(_layer_norm(ALPHA * x_ref[0, rows, :] + gt1 * mix,
                              ln1g_ref[...], ln1b_ref[...], LN_EPS))
    t = []
    for i in range(POST_GROUPS):
        h = (h1[i] * (1.0 + sc2) + sh2).astype(BF16)
        t.append(jnp.maximum(_dot(h, w1_ref[...]) + b1_ref[...], 0.0))
    for i, rows in enumerate(groups):
        ff = _dot((t[i] * t[i]).astype(BF16), w2_ref[...]) + b2_ref[...]
        o_ref[0, rows, :] = _layer_norm(ALPHA * h1[i] + gt2 * ff,
                                        ln2g_ref[...], ln2b_ref[...], LN_EPS).astype(o_ref.dtype)


def _post(x, mod, yb, g, gb, ma, p, tl, out_dtype):
    bsz, seq, _ = x.shape
    row = lambda b, s: (b, s, 0)
    consts = [p["wbb"], p["wout"], p["bout"], p["ln1g"], p["ln1b"],
              p["w1"], p["b1"], p["w2"], p["b2"], p["ln2g"], p["ln2b"]]
    return pl.pallas_call(
        _post_kernel,
        grid=(bsz, seq // tl),
        in_specs=[pl.BlockSpec((1, tl, D_MODEL), row),
                  pl.BlockSpec((1, 6, D_MODEL), lambda b, s: (b, 0, 0)),
                  pl.BlockSpec((1, PAIRS, tl, LANES), lambda b, s: (b, 0, s, 0)),
                  pl.BlockSpec((1, tl, R_WIDTH), row),
                  pl.BlockSpec((1, tl, D_MODEL), row),
                  pl.BlockSpec((1, tl, D_MODEL), row)]
                 + [_const_spec(c.shape) for c in consts],
        out_specs=pl.BlockSpec((1, tl, D_MODEL), row),
        out_shape=jax.ShapeDtypeStruct((bsz, seq, D_MODEL), out_dtype),
        compiler_params=pltpu.CompilerParams(
            dimension_semantics=("arbitrary", "arbitrary"), vmem_limit_bytes=VMEM_LIMIT),
        name="post",
    )(x, mod, yb, g, gb, ma, *consts)


W_IN_SPLIT = 2 * G_WIDTH + 3 * R_WIDTH + LORA_COLS
WPREP_COLS = 2 * LANES
GAP_BLOCK = W_IN_SPLIT // WPREP_COLS


def _relayout_w_in_kernel(wt_ref, o_ref):
    j = pl.program_id(0)
    rows = lax.broadcasted_iota(jnp.int32, wt_ref.shape, 0)
    keep = jnp.logical_or(j != GAP_BLOCK, rows < W_IN_SPLIT % WPREP_COLS)
    o_ref[...] = jnp.where(keep, wt_ref[...], 0.0).T.astype(BF16)


def _relayout_w_in(w_in, l):
    wt = jnp.transpose(w_in[l])
    gap = COLS_R.stop - W_IN_SPLIT

    unit = math.gcd(WPREP_COLS, gap)

    def src_row(j):
        return (j * (WPREP_COLS // unit) - jnp.where(j <= GAP_BLOCK, 0, gap // unit)) * unit

    return pl.pallas_call(
        _relayout_w_in_kernel,
        grid=(COLS_GB.stop // WPREP_COLS,),
        in_specs=[pl.BlockSpec((pl.Element(WPREP_COLS), pl.Element(D_MODEL)),
                               lambda j: (src_row(j), 0))],
        out_specs=pl.BlockSpec((D_MODEL, WPREP_COLS), lambda j: (0, j)),
        out_shape=jax.ShapeDtypeStruct((D_MODEL, COLS_GB.stop), BF16),
        name="wprep",
    )(wt)


def _prepare_params(l, w_in, b_in, g_ln_v, b_ln_v, w_spatial, b_spatial, mu_shift, w0,
                    w_decay_up, a0, w_aaa_up, w_gate_up, k_k, k_a, r_k, gn_gain, gn_bias,
                    w_branch_a, w_branch_b, w_out, b_out, ln1_g, ln1_b, w_ff1, b_ff1, w_ff2,
                    b_ff2, ln2_g, ln2_b):
    g_end = 2 * G_WIDTH
    rkv_end = g_end + 3 * R_WIDTH
    r_end = rkv_end + LORA_COLS
    pad = LORA_PAD - LORA_COLS
    row2 = lambda t: t.reshape(1, -1)
    wi, bi = w_in[l], b_in[l]
    b_all = jnp.concatenate([bi[:r_end], jnp.zeros((pad,), F32), bi[r_end:]])
    mu = jnp.concatenate([mu_shift[l], jnp.zeros((pad,), F32)])
    wup = jnp.zeros((LORA_PAD, 3 * R_WIDTH), F32)
    wup = wup.at[0:DECAY_LORA, 0:R_WIDTH].set(w_decay_up[l])
    wup = wup.at[DECAY_LORA:DECAY_LORA + AAA_LORA, R_WIDTH:2 * R_WIDTH].set(w_aaa_up[l])
    wup = wup.at[DECAY_LORA + AAA_LORA:LORA_COLS, 2 * R_WIDTH:].set(w_gate_up[l])
    ws = w_spatial[l].reshape(PAIRS, 2, CHUNK, CHUNK).transpose(0, 2, 1, 3).reshape(PAIRS, CHUNK, 2 * CHUNK)
    bs = jnp.repeat(b_spatial[l].T, G_WIDTH // G_GROUPS, axis=1)
    hid = jnp.arange(R_WIDTH) // R_HEAD
    bd = (hid[:, None] == hid[None, :]).astype(BF16)
    return dict(
        win=_relayout_w_in(w_in, l), bin=row2(b_all),
        glnv=row2(g_ln_v[l]), blnv=row2(b_ln_v[l]), ws=ws, bs=bs, mu=row2(mu),
        w0=row2(w0[l]), a0=row2(a0[l]), wup=wup.astype(BF16), kk=row2(k_k[l]), ka=row2(k_a[l]),
        bd=bd, wba=w_branch_a[l].astype(BF16),
        rk=row2(r_k[l]), gng=row2(gn_gain[l]), gnb=row2(gn_bias[l]),
        wbb=w_branch_b[l].astype(BF16), wout=w_out[l].astype(BF16), bout=row2(b_out[l]),
        ln1g=row2(ln1_g[l]), ln1b=row2(ln1_b[l]),
        w1=w_ff1[l].astype(BF16), b1=row2(b_ff1[l]), w2=w_ff2[l].astype(BF16), b2=row2(b_ff2[l]),
        ln2g=row2(ln2_g[l]), ln2b=row2(ln2_b[l]),
    )


def _tile(seq, want):
    t = min(want, seq)
    while seq % t:
        t //= 2
    return t


def kernel(x, c, w_ada, b_ada, w_in, b_in, g_ln_v, b_ln_v, w_spatial, b_spatial, mu_shift, w0, w_decay_up, a0, w_aaa_up, w_gate_up, k_k, k_a, r_k, gn_gain, gn_bias, w_branch_a, w_branch_b, w_out, b_out, ln1_g, ln1_b, w_ff1, b_ff1, w_ff2, b_ff2, ln2_g, ln2_b):
    bsz, seq, _ = x.shape
    assert seq % CHUNK == 0 and x.shape[2] == D_MODEL
    out_dtype = x.dtype
    h_res = x.astype(F32)
    tl = _tile(seq, 512)
    for l in range(DEPTH):
        p = _prepare_params(l, w_in, b_in, g_ln_v, b_ln_v, w_spatial, b_spatial, mu_shift, w0,
                            w_decay_up, a0, w_aaa_up, w_gate_up, k_k, k_a, r_k, gn_gain, gn_bias,
                            w_branch_a, w_branch_b, w_out, b_out, ln1_g, ln1_b, w_ff1, b_ff1,
                            w_ff2, b_ff2, ln2_g, ln2_b)
        mod = _modulation(c.astype(F32), w_ada, b_ada, l).reshape(bsz, 6, D_MODEL)
        ma, gb, sc, g = _inproj(h_res, mod, p, tl)
        yb = _scan(sc, p["rk"], p["gng"], p["gnb"])
        h_res = _post(h_res, mod, yb, g, gb, ma, p, tl, F32)
    return h_res.astype(out_dtype)
```

```python
import math

import jax
import jax.numpy as jnp
from jax import lax
from jax.experimental import pallas as pl
from jax.experimental.pallas import tpu as pltpu

D_MODEL = 1024
G_GROUPS = 8
G_WIDTH = 512
CHUNK = 128
R_WIDTH = 512
R_HEAD = 64
R_HEADS = R_WIDTH // R_HEAD
DECAY_LORA = 32
AAA_LORA = 32
GATE_LORA = 96
LORA_COLS = DECAY_LORA + AAA_LORA + GATE_LORA
D_FF = 4 * D_MODEL
DEPTH = 1
ALPHA = (2.0 * DEPTH) ** 0.25
LN_EPS = 1e-5
GN_EPS = 64e-5
DECAY_SCALE = math.exp(-0.5)

LANES = 128
PAIRS = R_WIDTH // LANES
LORA_PAD = 2 * LANES
R_COLS = 3 * R_WIDTH + LORA_PAD
COLS_U = slice(0, G_WIDTH)
COLS_V = slice(G_WIDTH, 2 * G_WIDTH)
COLS_R = slice(2 * G_WIDTH, 2 * G_WIDTH + R_COLS)
COLS_GA = slice(COLS_R.stop, COLS_R.stop + D_MODEL)
COLS_GB = slice(COLS_GA.stop, COLS_GA.stop + D_MODEL)
SCAN_CHUNK = 64
STACK = 2 * SCAN_CHUNK
INV_BASE = 8
POST_GROUPS = 2
IN_GROUPS = 2
I_R, I_LW, I_K, I_V, I_A, I_B = range(6)
GN_BATCHES = 4
SCAN_TILE = 2 * SCAN_CHUNK
SCAN_LAG = 3
VMEM_LIMIT = 56 * 1024 * 1024

F32 = jnp.float32
BF16 = jnp.bfloat16


def _dot(a, b):
    return jnp.dot(a, b, preferred_element_type=F32)


def _sigmoid(x):
    return 1.0 / (1.0 + jnp.exp(-x))


def _gelu_tanh(x):
    c = 0.7978845608028654
    hx = 0.5 * x
    return hx + hx * jnp.tanh(x * (c + (c * 0.044715) * (x * x)))


def _layer_norm(x, g, b, eps):
    mu = jnp.mean(x, axis=-1, keepdims=True)
    d = x - mu
    var = jnp.mean(d * d, axis=-1, keepdims=True)
    return d * lax.rsqrt(var + eps) * g + b


def _const_spec(shape):
    n = len(shape)
    return pl.BlockSpec(shape, lambda *_: (0,) * n)


def _mod_kernel(c_ref, w_ref, b_ref, o_ref):
    c = c_ref[...]
    c_act = c * _sigmoid(c)
    bsz = c.shape[0]
    ch = c_act.astype(BF16)
    cl = (c_act - ch.astype(F32)).astype(BF16)
    w = w_ref[...]
    wh = w.astype(BF16)
    wl = (w - wh.astype(F32)).astype(BF16)
    hh = _dot(jnp.concatenate([ch, cl], axis=0), wh)
    o_ref[...] = hh[:bsz] + hh[bsz:] + _dot(ch, wl) + b_ref[...]


def _modulation(c, w_ada, b_ada, l):
    bsz = c.shape[0]
    n = w_ada.shape[2]
    tn = D_MODEL
    return pl.pallas_call(
        _mod_kernel,
        grid=(n // tn,),
        in_specs=[pl.BlockSpec((bsz, D_MODEL), lambda j: (0, 0)),
                  pl.BlockSpec((None, D_MODEL, tn), lambda j: (l, 0, j)),
                  pl.BlockSpec((None, 1, tn), lambda j: (l, 0, j))],
        out_specs=pl.BlockSpec((bsz, tn), lambda j: (0, j)),
        out_shape=jax.ShapeDtypeStruct((bsz, n), F32),
        name="mod",
    )(c, w_ada, b_ada.reshape(b_ada.shape[0], 1, n))


def _inproj_kernel(x_ref, mod_ref, win_ref, bin_ref,
                   glnv_ref, blnv_ref, ws_ref, bs_ref, mu_ref, w0_ref, a0_ref, wup_ref,
                   kk_ref, ka_ref, bd_ref, wba_ref,
                   ma_ref, gb_ref, sc_ref, g_ref,
                   zsh_ref):
    tl = x_ref.shape[1]

    @pl.when(pl.program_id(1) == 0)
    def _():
        zsh_ref[0:8, :] = jnp.zeros((8, R_COLS), F32)

    sh1 = mod_ref[0, 0:1, :]
    sc1 = mod_ref[0, 1:2, :]
    gsz = tl // IN_GROUPS
    groups = [slice(i * gsz, (i + 1) * gsz) for i in range(IN_GROUPS)]
    gi = range(IN_GROUPS)
    h = [(x_ref[0, rows, :] * (1.0 + sc1) + sh1).astype(BF16) for rows in groups]
    def proj(i, cols):
        return _dot(h[i], win_ref[:, cols]) + bin_ref[:, cols]

    zr = [proj(i, COLS_R) for i in gi]
    for i, rows in enumerate(groups):
        zsh_ref[8 + rows.start:8 + rows.stop, :] = zr[i]
    zu = [proj(i, COLS_U) for i in gi]
    zv = [proj(i, COLS_V) for i in gi]

    z = []
    for i, rows in enumerate(groups):
        prev = zsh_ref[7 + rows.start:7 + rows.stop, :]
        z.append(zr[i] + (prev - zr[i]) * mu_ref[...])
    zsh_ref[7:8, :] = zr[-1][gsz - 1:gsz, :]
    llane = lax.broadcasted_iota(jnp.int32, (gsz, LORA_PAD), 1)
    lin = []
    for i in gi:
        xl = z[i][:, 3 * R_WIDTH:]
        lin.append(jnp.where(llane < DECAY_LORA, jnp.tanh(xl),
                             jnp.where(llane < DECAY_LORA + AAA_LORA, xl, _sigmoid(xl))).astype(BF16))
    zgb = [proj(i, COLS_GB) for i in gi]
    up = [_dot(lin[i], wup_ref[...]) for i in gi]

    u = [_gelu_tanh(zu[i]) for i in gi]
    v = [_layer_norm(_gelu_tanh(zv[i]), glnv_ref[...], blnv_ref[...], LN_EPS) for i in gi]

    kk, n2 = [], []
    for i, rows in enumerate(groups):
        lw = -DECAY_SCALE * _sigmoid(w0_ref[...] + up[i][:, 0:R_WIDTH])
        g_ref[0, rows, :] = up[i][:, 2 * R_WIDTH:].astype(g_ref.dtype)
        kk.append(z[i][:, R_WIDTH:2 * R_WIDTH] * kk_ref[...])
        n2.append(_dot((kk[i] * kk[i]).astype(BF16), bd_ref[...]))
        for q in range(PAIRS):
            sl = slice(q * LANES, (q + 1) * LANES)
            sc_ref[0, I_LW, q, rows, :] = lw[:, sl]
            sc_ref[0, I_R, q, rows, :] = z[i][:, sl]
            sc_ref[0, I_V, q, rows, :] = z[i][:, 2 * R_WIDTH + q * LANES:2 * R_WIDTH + (q + 1) * LANES]

    lane = lax.broadcasted_iota(jnp.int32, (CHUNK, LANES), 1)
    first_head = lane < R_HEAD
    trow = lax.broadcasted_iota(jnp.int32, (CHUNK, 2 * CHUNK), 0)
    scol = lax.broadcasted_iota(jnp.int32, (CHUNK, 2 * CHUNK), 1) % CHUNK
    causal = trow >= scol
    ws = [jnp.where(causal, ws_ref[q], 0.0).astype(BF16) for q in range(PAIRS)]
    ya = []
    for i in gi:
        ya_rows = []
        for c in range(gsz // CHUNK):
            vc = v[i][c * CHUNK:(c + 1) * CHUNK, :]
            s_parts = []
            for q in range(PAIRS):
                v2 = vc[:, q * LANES:(q + 1) * LANES]
                vm = jnp.concatenate([jnp.where(first_head, v2, 0.0),
                                      jnp.where(first_head, 0.0, v2)], axis=0).astype(BF16)
                s_parts.append(_dot(ws[q], vm))
            s = jnp.concatenate(s_parts, axis=1) + bs_ref[...]
            ya_rows.append(u[i][c * CHUNK:(c + 1) * CHUNK, :] * s)
        ya.append(jnp.concatenate(ya_rows, axis=0).astype(BF16))
    zga = [proj(i, COLS_GA) for i in gi]
    pa = [_dot(ya[i], wba_ref[...]) for i in gi]

    for i, rows in enumerate(groups):
        gb_ref[0, rows, :] = _sigmoid(zgb[i]).astype(gb_ref.dtype)
        a = _sigmoid(a0_ref[...] + up[i][:, R_WIDTH:2 * R_WIDTH])
        kkn = kk[i] * lax.rsqrt(jnp.maximum(n2[i], 1e-24))
        k2 = z[i][:, R_WIDTH:2 * R_WIDTH] * (1.0 + (a - 1.0) * ka_ref[...])
        nkk = -kkn
        kka = kkn * a
        for q in range(PAIRS):
            sl = slice(q * LANES, (q + 1) * LANES)
            sc_ref[0, I_K, q, rows, :] = k2[:, sl]
            sc_ref[0, I_A, q, rows, :] = nkk[:, sl]
            sc_ref[0, I_B, q, rows, :] = kka[:, sl]
        ma_ref[0, rows, :] = (_sigmoid(zga[i]) * pa[i]).astype(ma_ref.dtype)


def _inproj(x, mod, p, tl):
    bsz, seq, _ = x.shape
    grid = (bsz, seq // tl)
    row = lambda b, s: (b, s, 0)
    sc_shape = jax.ShapeDtypeStruct((bsz, 6, PAIRS, seq, LANES), F32)
    sc_spec = pl.BlockSpec((1, 6, PAIRS, tl, LANES), lambda b, s: (b, 0, 0, s, 0))
    consts = [p["win"], p["bin"], p["glnv"], p["blnv"],
              p["ws"], p["bs"], p["mu"], p["w0"], p["a0"], p["wup"], p["kk"], p["ka"], p["bd"],
              p["wba"]]
    return pl.pallas_call(
        _inproj_kernel,
        grid=grid,
        in_specs=[pl.BlockSpec((1, tl, D_MODEL), row),
                  pl.BlockSpec((1, 6, D_MODEL), lambda b, s: (b, 0, 0))]
                 + [_const_spec(c.shape) for c in consts],
        out_specs=[pl.BlockSpec((1, tl, D_MODEL), row),
                   pl.BlockSpec((1, tl, D_MODEL), row)]
                  + [sc_spec]
                  + [pl.BlockSpec((1, tl, R_WIDTH), row)],
        out_shape=[jax.ShapeDtypeStruct((bsz, seq, D_MODEL), BF16),
                   jax.ShapeDtypeStruct((bsz, seq, D_MODEL), BF16)]
                  + [sc_shape]
                  + [jax.ShapeDtypeStruct((bsz, seq, R_WIDTH), BF16)],
        scratch_shapes=[pltpu.VMEM((tl + 8, R_COLS), F32)],
        compiler_params=pltpu.CompilerParams(
            dimension_semantics=("arbitrary", "arbitrary"), vmem_limit_bytes=VMEM_LIMIT),
        name="inproj",
    )(x, mod, *consts)


def _scan_kernel(sc_ref, rk_ref, gng_ref, gnb_ref, y_ref, st_ref):
    bt, _, pp, lt, _ = sc_ref.shape
    cs = SCAN_CHUNK

    @pl.when(pl.program_id(0) == 0)
    def _():
        st_ref[...] = jnp.zeros(st_ref.shape, F32)

    lane = lax.broadcasted_iota(jnp.int32, (cs, LANES), 1)
    first_head = lane < R_HEAD

    def stack(t):
        return jnp.concatenate([jnp.where(first_head, t, 0.0),
                                jnp.where(first_head, 0.0, t)], axis=0)

    def fold(t):
        return jnp.where(first_head, t[:cs], t[cs:])

    ri = lax.broadcasted_iota(jnp.int32, (cs, cs), 0)
    ci = lax.broadcasted_iota(jnp.int32, (cs, cs), 1)
    tri_ones = jnp.where(ri >= ci, 1.0, 0.0).astype(BF16)
    si = lax.broadcasted_iota(jnp.int32, (cs, LANES), 0)
    sj = lane % cs
    strict = si > sj
    incl = si >= sj
    eye = jnp.where(si == sj, 1.0, 0.0).astype(F32)
    inv_base_mask = (si // INV_BASE) == (sj // INV_BASE)
    inv_merge_masks = []
    m = INV_BASE
    while m < cs:
        inv_merge_masks.append(((si // (2 * m)) == (sj // (2 * m)))
                               & ((si % (2 * m)) >= m) & ((sj % (2 * m)) < m))
        m *= 2
    bsi =lax.broadcasted_iota(jnp.int32, (LANES, LANES), 0) // R_HEAD
    bsj = lax.broadcasted_iota(jnp.int32, (LANES, LANES), 1) // R_HEAD
    head_ones = jnp.where(bsi == bsj, 1.0, 0.0).astype(BF16)
    head_mean = jnp.where(bsi == bsj, 1.0 / R_HEAD, 0.0).astype(BF16)

    chains = [(b, q) for q in range(pp) for b in range(bt)]
    chunks = [[dict(j=j, b=b, q=q, rows=slice(ck * cs, (ck + 1) * cs))
               for j, (b, q) in enumerate(chains)] for ck in range(lt // cs)]

    def s_cumsum(c):
        lw = sc_ref[c["b"], I_LW, c["q"], c["rows"], :]
        p1 = lw.astype(BF16)
        p2 = (lw - p1.astype(F32)).astype(BF16)
        c["lw"] = lw
        c["cum"] = _dot(tri_ones, p1) + _dot(tri_ones, p2)

    def s_operands(c):
        b, q, rows, cum, lw = c["b"], c["q"], c["rows"], c["cum"], c["lw"]
        last = cum[cs - 1:cs, :]
        e_pos = jnp.exp(cum)
        e_pos_ex = jnp.exp(cum - lw)
        e_neg = 1.0 / e_pos
        wcol = jnp.broadcast_to(jnp.exp(last), (LANES, LANES)).T
        r = sc_ref[b, I_R, q, rows, :]
        k = sc_ref[b, I_K, q, rows, :]
        bb = sc_ref[b, I_B, q, rows, :]
        c["at"] = sc_ref[b, I_A, q, rows, :] * e_pos_ex
        c["rt_b"] = (r * e_pos).astype(BF16)
        tb = stack(bb * e_neg).T
        tk = stack(k * e_neg).T
        c["vm"] = stack(sc_ref[b, I_V, q, rows, :]).astype(BF16)
        c["bt"] = fold(tb).astype(BF16)
        c["kt"] = fold(tk)
        c["wm"] = fold(wcol)
        c["aa"] = _dot(jnp.concatenate([c["at"].astype(BF16), c["rt_b"]], axis=0),
                       jnp.concatenate([tb, tk], axis=1).astype(BF16))
        lanes = slice(q * LANES, (q + 1) * LANES)
        c["rkr"] = (r * k * rk_ref[:, lanes]).astype(BF16)

    def s_intra(c):
        aa = c.pop("aa")
        a_ab = jnp.where(strict, aa[:cs, :LANES], 0.0)
        c["a_rb"] = jnp.where(incl, aa[cs:, :LANES], 0.0).astype(BF16)
        c["akv_y0"] = _dot(jnp.concatenate([jnp.where(strict, aa[:cs, LANES:], 0.0),
                                            jnp.where(incl, aa[cs:, LANES:], 0.0),
                                            c.pop("kt")], axis=0).astype(BF16), c.pop("vm"))
        a_d = jnp.where(inv_base_mask, a_ab, 0.0)
        c["a_ab"] = a_ab
        c["xk"] = _dot(a_d.astype(BF16), stack(a_d).astype(BF16))
        c["tk"] = eye + a_d

    def s_base_square(c):
        x_bd = stack(c["xk"]).astype(BF16)
        xt = _dot(jnp.concatenate([c["xk"], c["tk"]], axis=0).astype(BF16), x_bd)
        c["xk"] = xt[:cs]
        c["tk"] = c["tk"] + xt[cs:]

    def s_base_last(c):
        c["tk"] = c["tk"] + _dot(c["tk"].astype(BF16), stack(c.pop("xk")).astype(BF16))

    def s_merge_bt(level):
        def run(c):
            b_off = jnp.where(inv_merge_masks[level], c["a_ab"], 0.0)
            c["bt_m"] = _dot(b_off.astype(BF16), stack(c["tk"]).astype(BF16))
        return run

    def s_merge_tbt(c):
        c["tk"] = c["tk"] + _dot(c["tk"].astype(BF16), stack(c.pop("bt_m")).astype(BF16))

    def s_solve(c):
        del c["a_ab"]
        c["pq"] = _dot(c.pop("tk").astype(BF16),
                       jnp.concatenate([stack(c.pop("at")), stack(c["akv_y0"][:cs])],
                                       axis=1).astype(BF16))

    def s_u(c):
        c["st"] = st_ref[c["j"]]
        pq = c.pop("pq")
        ps = _dot(jnp.concatenate([pq[:, :LANES].astype(BF16), c.pop("rt_b")], axis=0),
                  stack(c["st"]).astype(BF16))
        c["u_b"] = stack(ps[:cs] + pq[:, LANES:]).astype(BF16)
        c["rs"] = ps[cs:]

    def s_y(c):
        au = _dot(jnp.concatenate([c.pop("a_rb"), c.pop("bt")], axis=0), c.pop("u_b"))
        akv_y0 = c.pop("akv_y0")
        c["y"] = c.pop("rs") + au[:cs] + akv_y0[cs:2 * cs]
        st_ref[c["j"]] = c.pop("wm") * (c.pop("st") + au[cs:] + akv_y0[2 * cs:])

    def s_bonus(cstate):
        rk_sum = _dot(jnp.concatenate([c.pop("rkr") for c in cstate], axis=0), head_ones)
        for c in cstate:
            c["rk_sum"] = rk_sum[c["j"] * cs:(c["j"] + 1) * cs]

    def per_chain(stage):
        def run(cstate):
            for c in cstate:
                stage(c)
        return run

    def finish(cstate):
        per_chain(s_u)(cstate)
        gsz = max(1, len(cstate) // GN_BATCHES)
        batches = [cstate[i:i + gsz] for i in range(0, len(cstate), gsz)]
        d = {}
        for t in range(len(batches) + 2):
            if t < len(batches):
                per_chain(s_y)(batches[t])
            if 0 <= t - 1 < len(batches):
                y_b = jnp.concatenate([c.pop("y") for c in batches[t - 1]], axis=0)
                d[t - 1] = y_b - _dot(y_b.astype(BF16), head_mean)
            if 0 <= t - 2 < len(batches):
                d_b = d.pop(t - 2)
                yn = d_b * lax.rsqrt(_dot((d_b * d_b).astype(BF16), head_mean) + GN_EPS)
                for i, c in enumerate(batches[t - 2]):
                    b, q, rows = c["b"], c["q"], c["rows"]
                    lanes = slice(q * LANES, (q + 1) * LANES)
                    y_ref[b, q, rows, :] = (
                        yn[i * cs:(i + 1) * cs] * gng_ref[:, lanes] + gnb_ref[:, lanes]
                        + c.pop("rk_sum") * sc_ref[b, I_V, q, rows, :]).astype(y_ref.dtype)

    n_base_sq = (INV_BASE - 1).bit_length() - 1
    pre = ([per_chain(s_cumsum), per_chain(s_operands), per_chain(s_intra), s_bonus]
           + [per_chain(s_base_square)] * (n_base_sq - 1) + [per_chain(s_base_last)])
    for level in range(len(inv_merge_masks)):
        pre += [per_chain(s_merge_bt(level)), per_chain(s_merge_tbt)]
    pre.append(per_chain(s_solve))
    for t in range(len(pre) + SCAN_LAG * (len(chunks) - 1)):
        for ck, cstate in enumerate(chunks):
            k = t - ck * SCAN_LAG
            if 0 <= k < len(pre):
                pre[k](cstate)
                if k == len(pre) - 1:
                    finish(cstate)


def _scan(sc, rk, gng, gnb):
    bsz, _, pp, seq, _ = sc.shape
    blk = pl.BlockSpec((bsz, pp, SCAN_TILE, LANES), lambda s: (0, 0, s, 0))
    vec = pl.BlockSpec((1, pp * LANES), lambda s: (0, 0))
    return pl.pallas_call(
        _scan_kernel,
        grid=(seq // SCAN_TILE,),
        in_specs=[pl.BlockSpec((bsz, 6, pp, SCAN_TILE, LANES), lambda s: (0, 0, 0, s, 0))] + [vec] * 3,
        out_specs=blk,
        out_shape=jax.ShapeDtypeStruct((bsz, pp, seq, LANES), BF16),
        scratch_shapes=[pltpu.VMEM((bsz * pp, SCAN_CHUNK, LANES), F32)],
        compiler_params=pltpu.CompilerParams(
            dimension_semantics=("arbitrary",), vmem_limit_bytes=VMEM_LIMIT),
        name="scan",
    )(sc, rk, gng, gnb)


def _post_kernel(x_ref, mod_ref, yb_ref, g_ref, gb_ref, ma_ref, wbb_ref, wout_ref, bout_ref,
                 ln1g_ref, ln1b_ref, w1_ref, b1_ref, w2_ref, b2_ref, ln2g_ref, ln2b_ref, o_ref):
    tl = x_ref.shape[1]
    gt1 = mod_ref[0, 2:3, :]
    sh2 = mod_ref[0, 3:4, :]
    sc2 = mod_ref[0, 4:5, :]
    gt2 = mod_ref[0, 5:6, :]
    groups = [slice(i * tl // POST_GROUPS, (i + 1) * tl // POST_GROUPS) for i in range(POST_GROUPS)]
    pb = []
    for rows in groups:
        yb = (jnp.concatenate([yb_ref[0, q, rows, :] for q in range(PAIRS)], axis=1).astype(F32)
              * g_ref[0, rows, :].astype(F32))
        pb.append(_dot(yb.astype(BF16), wbb_ref[...]))
    h1 = []
    for i, rows in enumerate(groups):
        merged = ma_ref[0, rows, :].astype(F32) + gb_ref[0, rows, :].astype(F32) * pb[i]
        mix = _dot(merged.astype(BF16), wout_ref[...]) + bout_ref[...]
        h1.append(_layer_norm(ALPHA * x_ref[0, rows, :] + gt1 * mix,
                              ln1g_ref[...], ln1b_ref[...], LN_EPS))
    t = []
    for i in range(POST_GROUPS):
        h = (h1[i] * (1.0 + sc2) + sh2).astype(BF16)
        t.append(jnp.maximum(_dot(h, w1_ref[...]) + b1_ref[...], 0.0))
    for i, rows in enumerate(groups):
        ff = _dot((t[i] * t[i]).astype(BF16), w2_ref[...]) + b2_ref[...]
        o_ref[0, rows, :] = _layer_norm(ALPHA * h1[i] + gt2 * ff,
                                        ln2g_ref[...], ln2b_ref[...], LN_EPS).astype(o_ref.dtype)


def _post(x, mod, yb, g, gb, ma, p, tl, out_dtype):
    bsz, seq, _ = x.shape
    row = lambda b, s: (b, s, 0)
    consts = [p["wbb"], p["wout"], p["bout"], p["ln1g"], p["ln1b"],
              p["w1"], p["b1"], p["w2"], p["b2"], p["ln2g"], p["ln2b"]]
    return pl.pallas_call(
        _post_kernel,
        grid=(bsz, seq // tl),
        in_specs=[pl.BlockSpec((1, tl, D_MODEL), row),
                  pl.BlockSpec((1, 6, D_MODEL), lambda b, s: (b, 0, 0)),
                  pl.BlockSpec((1, PAIRS, tl, LANES), lambda b, s: (b, 0, s, 0)),
                  pl.BlockSpec((1, tl, R_WIDTH), row),
                  pl.BlockSpec((1, tl, D_MODEL), row),
                  pl.BlockSpec((1, tl, D_MODEL), row)]
                 + [_const_spec(c.shape) for c in consts],
        out_specs=pl.BlockSpec((1, tl, D_MODEL), row),
        out_shape=jax.ShapeDtypeStruct((bsz, seq, D_MODEL), out_dtype),
        compiler_params=pltpu.CompilerParams(
            dimension_semantics=("arbitrary", "arbitrary"), vmem_limit_bytes=VMEM_LIMIT),
        name="post",
    )(x, mod, yb, g, gb, ma, *consts)


W_IN_SPLIT = 2 * G_WIDTH + 3 * R_WIDTH + LORA_COLS
WPREP_COLS = 2 * LANES
GAP_BLOCK = W_IN_SPLIT // WPREP_COLS


def _relayout_w_in_kernel(wt_ref, o_ref):
    j = pl.program_id(0)
    rows = lax.broadcasted_iota(jnp.int32, wt_ref.shape, 0)
    keep = jnp.logical_or(j != GAP_BLOCK, rows < W_IN_SPLIT % WPREP_COLS)
    o_ref[...] = jnp.where(keep, wt_ref[...], 0.0).T.astype(BF16)


def _relayout_w_in(w_in, l):
    wt = jnp.transpose(w_in[l])
    gap = COLS_R.stop - W_IN_SPLIT

    unit = math.gcd(WPREP_COLS, gap)

    def src_row(j):
        return (j * (WPREP_COLS // unit) - jnp.where(j <= GAP_BLOCK, 0, gap // unit)) * unit

    return pl.pallas_call(
        _relayout_w_in_kernel,
        grid=(COLS_GB.stop // WPREP_COLS,),
        in_specs=[pl.BlockSpec((pl.Element(WPREP_COLS), pl.Element(D_MODEL)),
                               lambda j: (src_row(j), 0))],
        out_specs=pl.BlockSpec((D_MODEL, WPREP_COLS), lambda j: (0, j)),
        out_shape=jax.ShapeDtypeStruct((D_MODEL, COLS_GB.stop), BF16),
        name="wprep",
    )(wt)


def _prepare_params(l, w_in, b_in, g_ln_v, b_ln_v, w_spatial, b_spatial, mu_shift, w0,
                    w_decay_up, a0, w_aaa_up, w_gate_up, k_k, k_a, r_k, gn_gain, gn_bias,
                    w_branch_a, w_branch_b, w_out, b_out, ln1_g, ln1_b, w_ff1, b_ff1, w_ff2,
                    b_ff2, ln2_g, ln2_b):
    g_end = 2 * G_WIDTH
    rkv_end = g_end + 3 * R_WIDTH
    r_end = rkv_end + LORA_COLS
    pad = LORA_PAD - LORA_COLS
    row2 = lambda t: t.reshape(1, -1)
    wi, bi = w_in[l], b_in[l]
    b_all = jnp.concatenate([bi[:r_end], jnp.zeros((pad,), F32), bi[r_end:]])
    mu = jnp.concatenate([mu_shift[l], jnp.zeros((pad,), F32)])
    wup = jnp.zeros((LORA_PAD, 3 * R_WIDTH), F32)
    wup = wup.at[0:DECAY_LORA, 0:R_WIDTH].set(w_decay_up[l])
    wup = wup.at[DECAY_LORA:DECAY_LORA + AAA_LORA, R_WIDTH:2 * R_WIDTH].set(w_aaa_up[l])
    wup = wup.at[DECAY_LORA + AAA_LORA:LORA_COLS, 2 * R_WIDTH:].set(w_gate_up[l])
    ws = w_spatial[l].reshape(PAIRS, 2, CHUNK, CHUNK).transpose(0, 2, 1, 3).reshape(PAIRS, CHUNK, 2 * CHUNK)
    bs = jnp.repeat(b_spatial[l].T, G_WIDTH // G_GROUPS, axis=1)
    hid = jnp.arange(R_WIDTH) // R_HEAD
    bd = (hid[:, None] == hid[None, :]).astype(BF16)
    return dict(
        win=_relayout_w_in(w_in, l), bin=row2(b_all),
        glnv=row2(g_ln_v[l]), blnv=row2(b_ln_v[l]), ws=ws, bs=bs, mu=row2(mu),
        w0=row2(w0[l]), a0=row2(a0[l]), wup=wup.astype(BF16), kk=row2(k_k[l]), ka=row2(k_a[l]),
        bd=bd, wba=w_branch_a[l].astype(BF16),
        rk=row2(r_k[l]), gng=row2(gn_gain[l]), gnb=row2(gn_bias[l]),
        wbb=w_branch_b[l].astype(BF16), wout=w_out[l].astype(BF16), bout=row2(b_out[l]),
        ln1g=row2(ln1_g[l]), ln1b=row2(ln1_b[l]),
        w1=w_ff1[l].astype(BF16), b1=row2(b_ff1[l]), w2=w_ff2[l].astype(BF16), b2=row2(b_ff2[l]),
        ln2g=row2(ln2_g[l]), ln2b=row2(ln2_b[l]),
    )


def _tile(seq, want):
    t = min(want, seq)
    while seq % t:
        t //= 2
    return t


def kernel(x, c, w_ada, b_ada, w_in, b_in, g_ln_v, b_ln_v, w_spatial, b_spatial, mu_shift, w0, w_decay_up, a0, w_aaa_up, w_gate_up, k_k, k_a, r_k, gn_gain, gn_bias, w_branch_a, w_branch_b, w_out, b_out, ln1_g, ln1_b, w_ff1, b_ff1, w_ff2, b_ff2, ln2_g, ln2_b):
    bsz, seq, _ = x.shape
    assert seq % CHUNK == 0 and x.shape[2] == D_MODEL
    out_dtype = x.dtype
    h_res = x.astype(F32)
    tl = _tile(seq, 512)
    for l in range(DEPTH):
        p = _prepare_params(l, w_in, b_in, g_ln_v, b_ln_v, w_spatial, b_spatial, mu_shift, w0,
                            w_decay_up, a0, w_aaa_up, w_gate_up, k_k, k_a, r_k, gn_gain, gn_bias,
                            w_branch_a, w_branch_b, w_out, b_out, ln1_g, ln1_b, w_ff1, b_ff1,
                            w_ff2, b_ff2, ln2_g, ln2_b)
        mod = _modulation(c.astype(F32), w_ada, b_ada, l).reshape(bsz, 6, D_MODEL)
        ma, gb, sc, g = _inproj(h_res, mod, p, tl)
        yb = _scan(sc, p["rk"], p["gng"], p["gnb"])
        h_res = _post(h_res, mod, yb, g, gb, ma, p, tl, F32)
    return h_res.astype(out_dtype)
```

```python
import math

import jax
import jax.numpy as jnp
from jax import lax
from jax.experimental import pallas as pl
from jax.experimental.pallas import tpu as pltpu

D_MODEL = 1024
G_GROUPS = 8
G_WIDTH = 512
CHUNK = 128
R_WIDTH = 512
R_HEAD = 64
DECAY_LORA = 32
AAA_LORA = 32
GATE_LORA = 96
LORA_COLS = DECAY_LORA + AAA_LORA + GATE_LORA
DEPTH = 1
ALPHA = (2.0 * DEPTH) ** 0.25
LN_EPS = 1e-5
GN_EPS = 64e-5
DECAY_SCALE = math.exp(-0.5)

LANES = 128
PAIRS = R_WIDTH // LANES
LORA_PAD = 2 * LANES
R_COLS = 3 * R_WIDTH + LORA_PAD
COLS_U = slice(0, G_WIDTH)
COLS_V = slice(G_WIDTH, 2 * G_WIDTH)
COLS_R = slice(2 * G_WIDTH, 2 * G_WIDTH + R_COLS)
COLS_GA = slice(COLS_R.stop, COLS_R.stop + D_MODEL)
COLS_GB = slice(COLS_GA.stop, COLS_GA.stop + D_MODEL)
SCAN_CHUNK = 64
INV_BASE = 8
POST_GROUPS = 2
IN_GROUPS = 2
I_R, I_LW, I_K, I_V, I_A, I_B = range(6)
GN_BATCHES = 4
SCAN_TILE = 2 * SCAN_CHUNK
SCAN_LAG = 3
VMEM_LIMIT = 56 * 1024 * 1024

F32 = jnp.float32
BF16 = jnp.bfloat16


def _dot(a, b):
    return jnp.dot(a, b, preferred_element_type=F32)


def _sigmoid(x):
    return 1.0 / (1.0 + jnp.exp(-x))


def _gelu_tanh(x):
    c = 0.7978845608028654
    hx = 0.5 * x
    return hx + hx * jnp.tanh(x * (c + (c * 0.044715) * (x * x)))


def _layer_norm(x, g, b, eps):
    mu = jnp.mean(x, axis=-1, keepdims=True)
    d = x - mu
    var = jnp.mean(d * d, axis=-1, keepdims=True)
    return d * lax.rsqrt(var + eps) * g + b


def _const_spec(shape):
    n = len(shape)
    return pl.BlockSpec(shape, lambda *_: (0,) * n)


def _mod_kernel(c_ref, w_ref, b_ref, o_ref):
    c = c_ref[...]
    c_act = c * _sigmoid(c)
    bsz = c.shape[0]
    ch = c_act.astype(BF16)
    cl = (c_act - ch.astype(F32)).astype(BF16)
    w = w_ref[...]
    wh = w.astype(BF16)
    wl = (w - wh.astype(F32)).astype(BF16)
    hh = _dot(jnp.concatenate([ch, cl], axis=0), wh)
    o_ref[...] = hh[:bsz] + hh[bsz:] + _dot(ch, wl) + b_ref[...]


def _modulation(c, w_ada, b_ada, l):
    bsz = c.shape[0]
    n = w_ada.shape[2]
    tn = D_MODEL
    return pl.pallas_call(
        _mod_kernel,
        grid=(n // tn,),
        in_specs=[pl.BlockSpec((bsz, D_MODEL), lambda j: (0, 0)),
                  pl.BlockSpec((None, D_MODEL, tn), lambda j: (l, 0, j)),
                  pl.BlockSpec((None, 1, tn), lambda j: (l, 0, j))],
        out_specs=pl.BlockSpec((bsz, tn), lambda j: (0, j)),
        out_shape=jax.ShapeDtypeStruct((bsz, n), F32),
        name="mod",
    )(c, w_ada, b_ada.reshape(b_ada.shape[0], 1, n))


def _inproj_kernel(x_ref, mod_ref, win_ref, bin_ref,
                   glnv_ref, blnv_ref, ws_ref, bs_ref, mu_ref, w0_ref, a0_ref, wup_ref,
                   kk_ref, ka_ref, bd_ref, wba_ref,
                   ma_ref, gb_ref, sc_ref, g_ref,
                   zsh_ref):
    tl = x_ref.shape[1]

    @pl.when(pl.program_id(1) == 0)
    def _():
        zsh_ref[0:8, :] = jnp.zeros((8, R_COLS), F32)

    sh1 = mod_ref[0, 0:1, :]
    sc1 = mod_ref[0, 1:2, :]
    gsz = tl // IN_GROUPS
    groups = [slice(i * gsz, (i + 1) * gsz) for i in range(IN_GROUPS)]
    gi = range(IN_GROUPS)
    h = [(x_ref[0, rows, :] * (1.0 + sc1) + sh1).astype(BF16) for rows in groups]
    def proj(i, cols):
        return _dot(h[i], win_ref[:, cols]) + bin_ref[:, cols]

    zr = [proj(i, COLS_R) for i in gi]
    for i, rows in enumerate(groups):
        zsh_ref[8 + rows.start:8 + rows.stop, :] = zr[i]
    zu = [proj(i, COLS_U) for i in gi]
    zv = [proj(i, COLS_V) for i in gi]

    z = []
    for i, rows in enumerate(groups):
        prev = zsh_ref[7 + rows.start:7 + rows.stop, :]
        z.append(zr[i] + (prev - zr[i]) * mu_ref[...])
    zsh_ref[7:8, :] = zr[-1][gsz - 1:gsz, :]
    llane = lax.broadcasted_iota(jnp.int32, (gsz, LORA_PAD), 1)
    lin = []
    for i in gi:
        xl = z[i][:, 3 * R_WIDTH:]
        lin.append(jnp.where(llane < DECAY_LORA, jnp.tanh(xl),
                             jnp.where(llane < DECAY_LORA + AAA_LORA, xl, _sigmoid(xl))).astype(BF16))
    zgb = [proj(i, COLS_GB) for i in gi]
    up = [_dot(lin[i], wup_ref[...]) for i in gi]

    u = [_gelu_tanh(zu[i]) for i in gi]
    v = [_layer_norm(_gelu_tanh(zv[i]), glnv_ref[...], blnv_ref[...], LN_EPS) for i in gi]

    kk, n2 = [], []
    for i, rows in enumerate(groups):
        lw = -DECAY_SCALE * _sigmoid(w0_ref[...] + up[i][:, 0:R_WIDTH])
        g_ref[0, rows, :] = up[i][:, 2 * R_WIDTH:].astype(g_ref.dtype)
        kk.append(z[i][:, R_WIDTH:2 * R_WIDTH] * kk_ref[...])
        n2.append(_dot((kk[i] * kk[i]).astype(BF16), bd_ref[...]))
        for q in range(PAIRS):
            sl = slice(q * LANES, (q + 1) * LANES)
            sc_ref[0, I_LW, q, rows, :] = lw[:, sl]
            sc_ref[0, I_R, q, rows, :] = z[i][:, sl]
            sc_ref[0, I_V, q, rows, :] = z[i][:, 2 * R_WIDTH + q * LANES:2 * R_WIDTH + (q + 1) * LANES]

    lane = lax.broadcasted_iota(jnp.int32, (CHUNK, LANES), 1)
    first_head = lane < R_HEAD
    trow = lax.broadcasted_iota(jnp.int32, (CHUNK, 2 * CHUNK), 0)
    scol = lax.broadcasted_iota(jnp.int32, (CHUNK, 2 * CHUNK), 1) % CHUNK
    causal = trow >= scol
    ws = [jnp.where(causal, ws_ref[q], 0.0).astype(BF16) for q in range(PAIRS)]
    ya = []
    for i in gi:
        ya_rows = []
        for c in range(gsz // CHUNK):
            vc = v[i][c * CHUNK:(c + 1) * CHUNK, :]
            s_parts = []
            for q in range(PAIRS):
                v2 = vc[:, q * LANES:(q + 1) * LANES]
                vm = jnp.concatenate([jnp.where(first_head, v2, 0.0),
                                      jnp.where(first_head, 0.0, v2)], axis=0).astype(BF16)
                s_parts.append(_dot(ws[q], vm))
            s = jnp.concatenate(s_parts, axis=1) + bs_ref[...]
            ya_rows.append(u[i][c * CHUNK:(c + 1) * CHUNK, :] * s)
        ya.append(jnp.concatenate(ya_rows, axis=0).astype(BF16))
    zga = [proj(i, COLS_GA) for i in gi]
    pa = [_dot(ya[i], wba_ref[...]) for i in gi]

    for i, rows in enumerate(groups):
        gb_ref[0, rows, :] = _sigmoid(zgb[i]).astype(gb_ref.dtype)
        a = _sigmoid(a0_ref[...] + up[i][:, R_WIDTH:2 * R_WIDTH])
        kkn = kk[i] * lax.rsqrt(jnp.maximum(n2[i], 1e-24))
        k2 = z[i][:, R_WIDTH:2 * R_WIDTH] * (1.0 + (a - 1.0) * ka_ref[...])
        nkk = -kkn
        kka = kkn * a
        for q in range(PAIRS):
            sl = slice(q * LANES, (q + 1) * LANES)
            sc_ref[0, I_K, q, rows, :] = k2[:, sl]
            sc_ref[0, I_A, q, rows, :] = nkk[:, sl]
            sc_ref[0, I_B, q, rows, :] = kka[:, sl]
        ma_ref[0, rows, :] = (_sigmoid(zga[i]) * pa[i]).astype(ma_ref.dtype)


def _inproj(x, mod, p, tl):
    bsz, seq, _ = x.shape
    grid = (bsz, seq // tl)
    row = lambda b, s: (b, s, 0)
    sc_shape = jax.ShapeDtypeStruct((bsz, 6, PAIRS, seq, LANES), F32)
    sc_spec = pl.BlockSpec((1, 6, PAIRS, tl, LANES), lambda b, s: (b, 0, 0, s, 0))
    consts = [p["win"], p["bin"], p["glnv"], p["blnv"],
              p["ws"], p["bs"], p["mu"], p["w0"], p["a0"], p["wup"], p["kk"], p["ka"], p["bd"],
              p["wba"]]
    return pl.pallas_call(
        _inproj_kernel,
        grid=grid,
        in_specs=[pl.BlockSpec((1, tl, D_MODEL), row),
                  pl.BlockSpec((1, 6, D_MODEL), lambda b, s: (b, 0, 0))]
                 + [_const_spec(c.shape) for c in consts],
        out_specs=[pl.BlockSpec((1, tl, D_MODEL), row),
                   pl.BlockSpec((1, tl, D_MODEL), row)]
                  + [sc_spec]
                  + [pl.BlockSpec((1, tl, R_WIDTH), row)],
        out_shape=[jax.ShapeDtypeStruct((bsz, seq, D_MODEL), BF16),
                   jax.ShapeDtypeStruct((bsz, seq, D_MODEL), BF16)]
                  + [sc_shape]
                  + [jax.ShapeDtypeStruct((bsz, seq, R_WIDTH), BF16)],
        scratch_shapes=[pltpu.VMEM((tl + 8, R_COLS), F32)],
        compiler_params=pltpu.CompilerParams(
            dimension_semantics=("arbitrary", "arbitrary"), vmem_limit_bytes=VMEM_LIMIT),
        name="inproj",
    )(x, mod, *consts)


def _scan_kernel(sc_ref, rk_ref, gng_ref, gnb_ref, y_ref, st_ref):
    bt, _, pp, lt, _ = sc_ref.shape
    cs = SCAN_CHUNK

    @pl.when(pl.program_id(0) == 0)
    def _():
        st_ref[...] = jnp.zeros(st_ref.shape, F32)

    lane = lax.broadcasted_iota(jnp.int32, (cs, LANES), 1)
    first_head = lane < R_HEAD

    def stack(t):
        return jnp.concatenate([jnp.where(first_head, t, 0.0),
                                jnp.where(first_head, 0.0, t)], axis=0)

    def fold(t):
        return jnp.where(first_head, t[:cs], t[cs:])

    ri = lax.broadcasted_iota(jnp.int32, (cs, cs), 0)
    ci = lax.broadcasted_iota(jnp.int32, (cs, cs), 1)
    tri_ones = jnp.where(ri >= ci, 1.0, 0.0).astype(BF16)
    si = lax.broadcasted_iota(jnp.int32, (cs, LANES), 0)
    sj = lane % cs
    strict = si > sj
    incl = si >= sj
    eye = jnp.where(si == sj, 1.0, 0.0).astype(F32)
    inv_base_mask = (si // INV_BASE) == (sj // INV_BASE)
    inv_merge_masks = []
    m = INV_BASE
    while m < cs:
        inv_merge_masks.append(((si // (2 * m)) == (sj // (2 * m)))
                               & ((si % (2 * m)) >= m) & ((sj % (2 * m)) < m))
        m *= 2
    bsi =lax.broadcasted_iota(jnp.int32, (LANES, LANES), 0) // R_HEAD
    bsj = lax.broadcasted_iota(jnp.int32, (LANES, LANES), 1) // R_HEAD
    head_ones = jnp.where(bsi == bsj, 1.0, 0.0).astype(BF16)
    head_mean = jnp.where(bsi == bsj, 1.0 / R_HEAD, 0.0).astype(BF16)

    chains = [(b, q) for q in range(pp) for b in range(bt)]
    chunks = [[dict(j=j, b=b, q=q, rows=slice(ck * cs, (ck + 1) * cs))
               for j, (b, q) in enumerate(chains)] for ck in range(lt // cs)]

    def s_cumsum(c):
        lw = sc_ref[c["b"], I_LW, c["q"], c["rows"], :]
        p1 = lw.astype(BF16)
        p2 = (lw - p1.astype(F32)).astype(BF16)
        c["lw"] = lw
        c["cum"] = _dot(tri_ones, p1) + _dot(tri_ones, p2)

    def s_operands(c):
        b, q, rows, cum, lw = c["b"], c["q"], c["rows"], c["cum"], c["lw"]
        last = cum[cs - 1:cs, :]
        e_pos = jnp.exp(cum)
        e_pos_ex = jnp.exp(cum - lw)
        e_neg = 1.0 / e_pos
        wcol = jnp.broadcast_to(jnp.exp(last), (LANES, LANES)).T
        r = sc_ref[b, I_R, q, rows, :]
        k = sc_ref[b, I_K, q, rows, :]
        bb = sc_ref[b, I_B, q, rows, :]
        c["at"] = sc_ref[b, I_A, q, rows, :] * e_pos_ex
        c["rt_b"] = (r * e_pos).astype(BF16)
        tb = stack(bb * e_neg).T
        tk = stack(k * e_neg).T
        c["vm"] = stack(sc_ref[b, I_V, q, rows, :]).astype(BF16)
        c["bt"] = fold(tb).astype(BF16)
        c["kt"] = fold(tk)
        c["wm"] = fold(wcol)
        c["aa"] = _dot(jnp.concatenate([c["at"].astype(BF16), c["rt_b"]], axis=0),
                       jnp.concatenate([tb, tk], axis=1).astype(BF16))
        lanes = slice(q * LANES, (q + 1) * LANES)
        c["rkr"] = (r * k * rk_ref[:, lanes]).astype(BF16)

    def s_intra(c):
        aa = c.pop("aa")
        a_ab = jnp.where(strict, aa[:cs, :LANES], 0.0)
        c["a_rb"] = jnp.where(incl, aa[cs:, :LANES], 0.0).astype(BF16)
        c["akv_y0"] = _dot(jnp.concatenate([jnp.where(strict, aa[:cs, LANES:], 0.0),
                                            jnp.where(incl, aa[cs:, LANES:], 0.0),
                                            c.pop("kt")], axis=0).astype(BF16), c.pop("vm"))
        a_d = jnp.where(inv_base_mask, a_ab, 0.0)
        c["a_ab"] = a_ab
        c["xk"] = _dot(a_d.astype(BF16), stack(a_d).astype(BF16))
        c["tk"] = eye + a_d

    def s_base_square(c):
        x_bd = stack(c["xk"]).astype(BF16)
        xt = _dot(jnp.concatenate([c["xk"], c["tk"]], axis=0).astype(BF16), x_bd)
        c["xk"] = xt[:cs]
        c["tk"] = c["tk"] + xt[cs:]

    def s_base_last(c):
        c["tk"] = c["tk"] + _dot(c["tk"].astype(BF16), stack(c.pop("xk")).astype(BF16))

    def s_merge_bt(level):
        def run(c):
            b_off = jnp.where(inv_merge_masks[level], c["a_ab"], 0.0)
            c["bt_m"] = _dot(b_off.astype(BF16), stack(c["tk"]).astype(BF16))
        return run

    def s_merge_tbt(c):
        c["tk"] = c["tk"] + _dot(c["tk"].astype(BF16), stack(c.pop("bt_m")).astype(BF16))

    def s_solve(c):
        del c["a_ab"]
        c["pq"] = _dot(c.pop("tk").astype(BF16),
                       jnp.concatenate([stack(c.pop("at")), stack(c["akv_y0"][:cs])],
                                       axis=1).astype(BF16))

    def s_u(c):
        c["st"] = st_ref[c["j"]]
        pq = c.pop("pq")
        ps = _dot(jnp.concatenate([pq[:, :LANES].astype(BF16), c.pop("rt_b")], axis=0),
                  stack(c["st"]).astype(BF16))
        c["u_b"] = stack(ps[:cs] + pq[:, LANES:]).astype(BF16)
        c["rs"] = ps[cs:]

    def s_y(c):
        au = _dot(jnp.concatenate([c.pop("a_rb"), c.pop("bt")], axis=0), c.pop("u_b"))
        akv_y0 = c.pop("akv_y0")
        c["y"] = c.pop("rs") + au[:cs] + akv_y0[cs:2 * cs]
        st_ref[c["j"]] = c.pop("wm") * (c.pop("st") + au[cs:] + akv_y0[2 * cs:])

    def s_bonus(cstate):
        rk_sum = _dot(jnp.concatenate([c.pop("rkr") for c in cstate], axis=0), head_ones)
        for c in cstate:
            c["rk_sum"] = rk_sum[c["j"] * cs:(c["j"] + 1) * cs]

    def per_chain(stage):
        def run(cstate):
            for c in cstate:
                stage(c)
        return run

    def finish(cstate):
        per_chain(s_u)(cstate)
        gsz = max(1, len(cstate) // GN_BATCHES)
        batches = [cstate[i:i + gsz] for i in range(0, len(cstate), gsz)]
        d = {}
        for t in range(len(batches) + 2):
            if t < len(batches):
                per_chain(s_y)(batches[t])
            if 0 <= t - 1 < len(batches):
                y_b = jnp.concatenate([c.pop("y") for c in batches[t - 1]], axis=0)
                d[t - 1] = y_b - _dot(y_b.astype(BF16), head_mean)
            if 0 <= t - 2 < len(batches):
                d_b = d.pop(t - 2)
                yn = d_b * lax.rsqrt(_dot((d_b * d_b).astype(BF16), head_mean) + GN_EPS)
                for i, c in enumerate(batches[t - 2]):
                    b, q, rows = c["b"], c["q"], c["rows"]
                    lanes = slice(q * LANES, (q + 1) * LANES)
                    y_ref[b, q, rows, :] = (
                        yn[i * cs:(i + 1) * cs] * gng_ref[:, lanes] + gnb_ref[:, lanes]
                        + c.pop("rk_sum") * sc_ref[b, I_V, q, rows, :]).astype(y_ref.dtype)

    n_base_sq = (INV_BASE - 1).bit_length() - 1
    pre = ([per_chain(s_cumsum), per_chain(s_operands), per_chain(s_intra), s_bonus]
           + [per_chain(s_base_square)] * (n_base_sq - 1) + [per_chain(s_base_last)])
    for level in range(len(inv_merge_masks)):
        pre += [per_chain(s_merge_bt(level)), per_chain(s_merge_tbt)]
    pre.append(per_chain(s_solve))
    for t in range(len(pre) + SCAN_LAG * (len(chunks) - 1)):
        for ck, cstate in enumerate(chunks):
            k = t - ck * SCAN_LAG
            if 0 <= k < len(pre):
                pre[k](cstate)
                if k == len(pre) - 1:
                    finish(cstate)


def _scan(sc, rk, gng, gnb):
    bsz, _, pp, seq, _ = sc.shape
    blk = pl.BlockSpec((bsz, pp, SCAN_TILE, LANES), lambda s: (0, 0, s, 0))
    vec = pl.BlockSpec((1, pp * LANES), lambda s: (0, 0))
    return pl.pallas_call(
        _scan_kernel,
        grid=(seq // SCAN_TILE,),
        in_specs=[pl.BlockSpec((bsz, 6, pp, SCAN_TILE, LANES), lambda s: (0, 0, 0, s, 0))] + [vec] * 3,
        out_specs=blk,
        out_shape=jax.ShapeDtypeStruct((bsz, pp, seq, LANES), BF16),
        scratch_shapes=[pltpu.VMEM((bsz * pp, SCAN_CHUNK, LANES), F32)],
        compiler_params=pltpu.CompilerParams(
            dimension_semantics=("arbitrary",), vmem_limit_bytes=VMEM_LIMIT),
        name="scan",
    )(sc, rk, gng, gnb)


def _post_kernel(x_ref, mod_ref, yb_ref, g_ref, gb_ref, ma_ref, wbb_ref, wout_ref, bout_ref,
                 ln1g_ref, ln1b_ref, w1_ref, b1_ref, w2_ref, b2_ref, ln2g_ref, ln2b_ref, o_ref):
    tl = x_ref.shape[1]
    gt1 = mod_ref[0, 2:3, :]
    sh2 = mod_ref[0, 3:4, :]
    sc2 = mod_ref[0, 4:5, :]
    gt2 = mod_ref[0, 5:6, :]
    groups = [slice(i * tl // POST_GROUPS, (i + 1) * tl // POST_GROUPS) for i in range(POST_GROUPS)]
    pb = []
    for rows in groups:
        yb = (jnp.concatenate([yb_ref[0, q, rows, :] for q in range(PAIRS)], axis=1).astype(F32)
              * g_ref[0, rows, :].astype(F32))
        pb.append(_dot(yb.astype(BF16), wbb_ref[...]))
    h1 = []
    for i, rows in enumerate(groups):
        merged = ma_ref[0, rows, :].astype(F32) + gb_ref[0, rows, :].astype(F32) * pb[i]
        mix = _dot(merged.astype(BF16), wout_ref[...]) + bout_ref[...]
        h1.append(_layer_norm(ALPHA * x_ref[0, rows, :] + gt1 * mix,
                              ln1g_ref[...], ln1b_ref[...], LN_EPS))
    t = []
    for i in range(POST_GROUPS):
        h = (h1[i] * (1.0 + sc2) + sh2).astype(BF16)
        t.append(jnp.maximum(_dot(h, w1_ref[...]) + b1_ref[...], 0.0))
    for i, rows in enumerate(groups):
        ff = _dot((t[i] * t[i]).astype(BF16), w2_ref[...]) + b2_ref[...]
        o_ref[0, rows, :] = _layer_norm(ALPHA * h1[i] + gt2 * ff,
                                        ln2g_ref[...], ln2b_ref[...], LN_EPS).astype(o_ref.dtype)


def _post(x, mod, yb, g, gb, ma, p, tl, out_dtype):
    bsz, seq, _ = x.shape
    row = lambda b, s: (b, s, 0)
    consts = [p["wbb"], p["wout"], p["bout"], p["ln1g"], p["ln1b"],
              p["w1"], p["b1"], p["w2"], p["b2"], p["ln2g"], p["ln2b"]]
    return pl.pallas_call(
        _post_kernel,
        grid=(bsz, seq // tl),
        in_specs=[pl.BlockSpec((1, tl, D_MODEL), row),
                  pl.BlockSpec((1, 6, D_MODEL), lambda b, s: (b, 0, 0)),
                  pl.BlockSpec((1, PAIRS, tl, LANES), lambda b, s: (b, 0, s, 0)),
                  pl.BlockSpec((1, tl, R_WIDTH), row),
                  pl.BlockSpec((1, tl, D_MODEL), row),
                  pl.BlockSpec((1, tl, D_MODEL), row)]
                 + [_const_spec(c.shape) for c in consts],
        out_specs=pl.BlockSpec((1, tl, D_MODEL), row),
        out_shape=jax.ShapeDtypeStruct((bsz, seq, D_MODEL), out_dtype),
        compiler_params=pltpu.CompilerParams(
            dimension_semantics=("arbitrary", "arbitrary"), vmem_limit_bytes=VMEM_LIMIT),
        name="post",
    )(x, mod, yb, g, gb, ma, *consts)


W_IN_SPLIT = 2 * G_WIDTH + 3 * R_WIDTH + LORA_COLS
WPREP_COLS = 2 * LANES
GAP_BLOCK = W_IN_SPLIT // WPREP_COLS


def _relayout_w_in_kernel(wt_ref, o_ref):
    j = pl.program_id(0)
    rows = lax.broadcasted_iota(jnp.int32, wt_ref.shape, 0)
    keep = jnp.logical_or(j != GAP_BLOCK, rows < W_IN_SPLIT % WPREP_COLS)
    o_ref[...] = jnp.where(keep, wt_ref[...], 0.0).T.astype(BF16)


def _relayout_w_in(w_in, l):
    wt = jnp.transpose(w_in[l])
    gap = COLS_R.stop - W_IN_SPLIT

    unit = math.gcd(WPREP_COLS, gap)

    def src_row(j):
        return (j * (WPREP_COLS // unit) - jnp.where(j <= GAP_BLOCK, 0, gap // unit)) * unit

    return pl.pallas_call(
        _relayout_w_in_kernel,
        grid=(COLS_GB.stop // WPREP_COLS,),
        in_specs=[pl.BlockSpec((pl.Element(WPREP_COLS), pl.Element(D_MODEL)),
                               lambda j: (src_row(j), 0))],
        out_specs=pl.BlockSpec((D_MODEL, WPREP_COLS), lambda j: (0, j)),
        out_shape=jax.ShapeDtypeStruct((D_MODEL, COLS_GB.stop), BF16),
        name="wprep",
    )(wt)


def _prepare_params(l, w_in, b_in, g_ln_v, b_ln_v, w_spatial, b_spatial, mu_shift, w0,
                    w_decay_up, a0, w_aaa_up, w_gate_up, k_k, k_a, r_k, gn_gain, gn_bias,
                    w_branch_a, w_branch_b, w_out, b_out, ln1_g, ln1_b, w_ff1, b_ff1, w_ff2,
                    b_ff2, ln2_g, ln2_b):
    g_end = 2 * G_WIDTH
    rkv_end = g_end + 3 * R_WIDTH
    r_end = rkv_end + LORA_COLS
    pad = LORA_PAD - LORA_COLS
    row2 = lambda t: t.reshape(1, -1)
    wi, bi = w_in[l], b_in[l]
    b_all = jnp.concatenate([bi[:r_end], jnp.zeros((pad,), F32), bi[r_end:]])
    mu = jnp.concatenate([mu_shift[l], jnp.zeros((pad,), F32)])
    wup = jnp.zeros((LORA_PAD, 3 * R_WIDTH), F32)
    wup = wup.at[0:DECAY_LORA, 0:R_WIDTH].set(w_decay_up[l])
    wup = wup.at[DECAY_LORA:DECAY_LORA + AAA_LORA, R_WIDTH:2 * R_WIDTH].set(w_aaa_up[l])
    wup = wup.at[DECAY_LORA + AAA_LORA:LORA_COLS, 2 * R_WIDTH:].set(w_gate_up[l])
    ws = w_spatial[l].reshape(PAIRS, 2, CHUNK, CHUNK).transpose(0, 2, 1, 3).reshape(PAIRS, CHUNK, 2 * CHUNK)
    bs = jnp.repeat(b_spatial[l].T, G_WIDTH // G_GROUPS, axis=1)
    hid = jnp.arange(R_WIDTH) // R_HEAD
    bd = (hid[:, None] == hid[None, :]).astype(BF16)
    return dict(
        win=_relayout_w_in(w_in, l), bin=row2(b_all),
        glnv=row2(g_ln_v[l]), blnv=row2(b_ln_v[l]), ws=ws, bs=bs, mu=row2(mu),
        w0=row2(w0[l]), a0=row2(a0[l]), wup=wup.astype(BF16), kk=row2(k_k[l]), ka=row2(k_a[l]),
        bd=bd, wba=w_branch_a[l].astype(BF16),
        rk=row2(r_k[l]), gng=row2(gn_gain[l]), gnb=row2(gn_bias[l]),
        wbb=w_branch_b[l].astype(BF16), wout=w_out[l].astype(BF16), bout=row2(b_out[l]),
        ln1g=row2(ln1_g[l]), ln1b=row2(ln1_b[l]),
        w1=w_ff1[l].astype(BF16), b1=row2(b_ff1[l]), w2=w_ff2[l].astype(BF16), b2=row2(b_ff2[l]),
        ln2g=row2(ln2_g[l]), ln2b=row2(ln2_b[l]),
    )


def _tile(seq, want):
    t = min(want, seq)
    while seq % t:
        t //= 2
    return t


def kernel(x, c, w_ada, b_ada, w_in, b_in, g_ln_v, b_ln_v, w_spatial, b_spatial, mu_shift, w0, w_decay_up, a0, w_aaa_up, w_gate_up, k_k, k_a, r_k, gn_gain, gn_bias, w_branch_a, w_branch_b, w_out, b_out, ln1_g, ln1_b, w_ff1, b_ff1, w_ff2, b_ff2, ln2_g, ln2_b):
    bsz, seq, _ = x.shape
    assert seq % CHUNK == 0 and x.shape[2] == D_MODEL
    out_dtype = x.dtype
    h_res = x.astype(F32)
    tl = _tile(seq, 512)
    for l in range(DEPTH):
        p = _prepare_params(l, w_in, b_in, g_ln_v, b_ln_v, w_spatial, b_spatial, mu_shift, w0,
                            w_decay_up, a0, w_aaa_up, w_gate_up, k_k, k_a, r_k, gn_gain, gn_bias,
                            w_branch_a, w_branch_b, w_out, b_out, ln1_g, ln1_b, w_ff1, b_ff1,
                            w_ff2, b_ff2, ln2_g, ln2_b)
        mod = _modulation(c.astype(F32), w_ada, b_ada, l).reshape(bsz, 6, D_MODEL)
        ma, gb, sc, g = _inproj(h_res, mod, p, tl)
        yb = _scan(sc, p["rk"], p["gng"], p["gnb"])
        h_res = _post(h_res, mod, yb, g, gb, ma, p, tl, F32)
    return h_res.astype(out_dtype)
```

```python
import math

import jax
import jax.numpy as jnp
from jax import lax
from jax.experimental import pallas as pl
from jax.experimental.pallas import tpu as pltpu

D_MODEL = 1024
G_GROUPS = 8
G_WIDTH = 512
CHUNK = 128
R_WIDTH = 512
R_HEAD = 64
DECAY_LORA = 32
AAA_LORA = 32
GATE_LORA = 96
LORA_COLS = DECAY_LORA + AAA_LORA + GATE_LORA
DEPTH = 1
ALPHA = (2.0 * DEPTH) ** 0.25
LN_EPS = 1e-5
GN_EPS = 64e-5
DECAY_SCALE = math.exp(-0.5)

LANES = 128
PAIRS = R_WIDTH // LANES
LORA_PAD = 2 * LANES
R_COLS = 3 * R_WIDTH + LORA_PAD
COLS_U = slice(0, G_WIDTH)
COLS_V = slice(G_WIDTH, 2 * G_WIDTH)
COLS_R = slice(2 * G_WIDTH, 2 * G_WIDTH + R_COLS)
COLS_GA = slice(COLS_R.stop, COLS_R.stop + D_MODEL)
COLS_GB = slice(COLS_GA.stop, COLS_GA.stop + D_MODEL)
SCAN_CHUNK = 64
INV_BASE = 8
POST_GROUPS = 2
IN_GROUPS = 2
I_R, I_LW, I_K, I_V, I_A, I_B = range(6)
GN_BATCHES = 4
SCAN_TILE = 2 * SCAN_CHUNK
SCAN_LAG = 3
VMEM_LIMIT = 56 * 1024 * 1024

F32 = jnp.float32
BF16 = jnp.bfloat16


def _dot(a, b):
    return jnp.dot(a, b, preferred_element_type=F32)


def _sigmoid(x):
    return 1.0 / (1.0 + jnp.exp(-x))


def _gelu_tanh(x):
    c = 0.7978845608028654
    hx = 0.5 * x
    return hx + hx * jnp.tanh(x * (c + (c * 0.044715) * (x * x)))


def _layer_norm(x, g, b, eps):
    mu = jnp.mean(x, axis=-1, keepdims=True)
    d = x - mu
    var = jnp.mean(d * d, axis=-1, keepdims=True)
    return d * lax.rsqrt(var + eps) * g + b


def _const_spec(shape):
    n = len(shape)
    return pl.BlockSpec(shape, lambda *_: (0,) * n)


def _mod_kernel(c_ref, w_ref, b_ref, o_ref):
    c = c_ref[...]
    c_act = c * _sigmoid(c)
    bsz = c.shape[0]
    ch = c_act.astype(BF16)
    cl = (c_act - ch.astype(F32)).astype(BF16)
    w = w_ref[...]
    wh = w.astype(BF16)
    wl = (w - wh.astype(F32)).astype(BF16)
    hh = _dot(jnp.concatenate([ch, cl], axis=0), wh)
    o_ref[...] = hh[:bsz] + hh[bsz:] + _dot(ch, wl) + b_ref[...]


def _modulation(c, w_ada, b_ada, l):
    bsz = c.shape[0]
    n = w_ada.shape[2]
    tn = D_MODEL
    return pl.pallas_call(
        _mod_kernel,
        grid=(n // tn,),
        in_specs=[pl.BlockSpec((bsz, D_MODEL), lambda j: (0, 0)),
                  pl.BlockSpec((None, D_MODEL, tn), lambda j: (l, 0, j)),
                  pl.BlockSpec((None, 1, tn), lambda j: (l, 0, j))],
        out_specs=pl.BlockSpec((bsz, tn), lambda j: (0, j)),
        out_shape=jax.ShapeDtypeStruct((bsz, n), F32),
        name="mod",
    )(c, w_ada, b_ada.reshape(b_ada.shape[0], 1, n))


def _inproj_kernel(x_ref, mod_ref, win_ref, bin_ref,
                   glnv_ref, blnv_ref, ws_ref, bs_ref, mu_ref, w0_ref, a0_ref, wup_ref,
                   kk_ref, ka_ref, bd_ref, wba_ref,
                   ma_ref, gb_ref, sc_ref, g_ref,
                   zsh_ref):
    tl = x_ref.shape[1]

    @pl.when(pl.program_id(1) == 0)
    def _():
        zsh_ref[0:8, :] = jnp.zeros((8, R_COLS), F32)

    sh1 = mod_ref[0, 0:1, :]
    sc1 = mod_ref[0, 1:2, :]
    gsz = tl // IN_GROUPS
    groups = [slice(i * gsz, (i + 1) * gsz) for i in range(IN_GROUPS)]
    gi = range(IN_GROUPS)
    h = [(x_ref[0, rows, :] * (1.0 + sc1) + sh1).astype(BF16) for rows in groups]
    def proj(i, cols):
        return _dot(h[i], win_ref[:, cols]) + bin_ref[:, cols]

    zr = [proj(i, COLS_R) for i in gi]
    for i, rows in enumerate(groups):
        zsh_ref[8 + rows.start:8 + rows.stop, :] = zr[i]
    zu = [proj(i, COLS_U) for i in gi]
    zv = [proj(i, COLS_V) for i in gi]

    z = []
    for i, rows in enumerate(groups):
        prev = zsh_ref[7 + rows.start:7 + rows.stop, :]
        z.append(zr[i] + (prev - zr[i]) * mu_ref[...])
    zsh_ref[7:8, :] = zr[-1][gsz - 1:gsz, :]
    llane = lax.broadcasted_iota(jnp.int32, (gsz, LORA_PAD), 1)
    lin = []
    for i in gi:
        xl = z[i][:, 3 * R_WIDTH:]
        lin.append(jnp.where(llane < DECAY_LORA, jnp.tanh(xl),
                             jnp.where(llane < DECAY_LORA + AAA_LORA, xl, _sigmoid(xl))).astype(BF16))
    zgb = [proj(i, COLS_GB) for i in gi]
    up = [_dot(lin[i], wup_ref[...]) for i in gi]

    u = [_gelu_tanh(zu[i]) for i in gi]
    v = [_layer_norm(_gelu_tanh(zv[i]), glnv_ref[...], blnv_ref[...], LN_EPS) for i in gi]

    kk, n2 = [], []
    for i, rows in enumerate(groups):
        lw = -DECAY_SCALE * _sigmoid(w0_ref[...] + up[i][:, 0:R_WIDTH])
        g_ref[0, rows, :] = up[i][:, 2 * R_WIDTH:].astype(g_ref.dtype)
        kk.append(z[i][:, R_WIDTH:2 * R_WIDTH] * kk_ref[...])
        n2.append(_dot((kk[i] * kk[i]).astype(BF16), bd_ref[...]))
        for q in range(PAIRS):
            sl = slice(q * LANES, (q + 1) * LANES)
            sc_ref[0, I_LW, q, rows, :] = lw[:, sl]
            sc_ref[0, I_R, q, rows, :] = z[i][:, sl]
            sc_ref[0, I_V, q, rows, :] = z[i][:, 2 * R_WIDTH + q * LANES:2 * R_WIDTH + (q + 1) * LANES]

    lane = lax.broadcasted_iota(jnp.int32, (CHUNK, LANES), 1)
    first_head = lane < R_HEAD
    trow = lax.broadcasted_iota(jnp.int32, (CHUNK, 2 * CHUNK), 0)
    scol = lax.broadcasted_iota(jnp.int32, (CHUNK, 2 * CHUNK), 1) % CHUNK
    causal = trow >= scol
    ws = [jnp.where(causal, ws_ref[q], 0.0).astype(BF16) for q in range(PAIRS)]
    ya = []
    for i in gi:
        ya_rows = []
        for c in range(gsz // CHUNK):
            vc = v[i][c * CHUNK:(c + 1) * CHUNK, :]
            s_parts = []
            for q in range(PAIRS):
                v2 = vc[:, q * LANES:(q + 1) * LANES]
                vm = jnp.concatenate([jnp.where(first_head, v2, 0.0),
                                      jnp.where(first_head, 0.0, v2)], axis=0).astype(BF16)
                s_parts.append(_dot(ws[q], vm))
            s = jnp.concatenate(s_parts, axis=1) + bs_ref[...]
            ya_rows.append(u[i][c * CHUNK:(c + 1) * CHUNK, :] * s)
        ya.append(jnp.concatenate(ya_rows, axis=0).astype(BF16))
    zga = [proj(i, COLS_GA) for i in gi]
    pa = [_dot(ya[i], wba_ref[...]) for i in gi]

    for i, rows in enumerate(groups):
        gb_ref[0, rows, :] = _sigmoid(zgb[i]).astype(gb_ref.dtype)
        a = _sigmoid(a0_ref[...] + up[i][:, R_WIDTH:2 * R_WIDTH])
        kkn = kk[i] * lax.rsqrt(jnp.maximum(n2[i], 1e-24))
        k2 = z[i][:, R_WIDTH:2 * R_WIDTH] * (1.0 + (a - 1.0) * ka_ref[...])
        nkk = -kkn
        kka = kkn * a
        for q in range(PAIRS):
            sl = slice(q * LANES, (q + 1) * LANES)
            sc_ref[0, I_K, q, rows, :] = k2[:, sl]
            sc_ref[0, I_A, q, rows, :] = nkk[:, sl]
            sc_ref[0, I_B, q, rows, :] = kka[:, sl]
        ma_ref[0, rows, :] = (_sigmoid(zga[i]) * pa[i]).astype(ma_ref.dtype)


def _inproj(x, mod, p, tl):
    bsz, seq, _ = x.shape
    grid = (bsz, seq // tl)
    row = lambda b, s: (b, s, 0)
    sc_shape = jax.ShapeDtypeStruct((bsz, 6, PAIRS, seq, LANES), F32)
    sc_spec = pl.BlockSpec((1, 6, PAIRS, tl, LANES), lambda b, s: (b, 0, 0, s, 0))
    consts = [p["win"], p["bin"], p["glnv"], p["blnv"],
              p["ws"], p["bs"], p["mu"], p["w0"], p["a0"], p["wup"], p["kk"], p["ka"], p["bd"],
              p["wba"]]
    return pl.pallas_call(
        _inproj_kernel,
        grid=grid,
        in_specs=[pl.BlockSpec((1, tl, D_MODEL), row),
                  pl.BlockSpec((1, 6, D_MODEL), lambda b, s: (b, 0, 0))]
                 + [_const_spec(c.shape) for c in consts],
        out_specs=[pl.BlockSpec((1, tl, D_MODEL), row),
                   pl.BlockSpec((1, tl, D_MODEL), row)]
                  + [sc_spec]
                  + [pl.BlockSpec((1, tl, R_WIDTH), row)],
        out_shape=[jax.ShapeDtypeStruct((bsz, seq, D_MODEL), BF16),
                   jax.ShapeDtypeStruct((bsz, seq, D_MODEL), BF16)]
                  + [sc_shape]
                  + [jax.ShapeDtypeStruct((bsz, seq, R_WIDTH), BF16)],
        scratch_shapes=[pltpu.VMEM((tl + 8, R_COLS), F32)],
        compiler_params=pltpu.CompilerParams(
            dimension_semantics=("arbitrary", "arbitrary"), vmem_limit_bytes=VMEM_LIMIT),
        name="inproj",
    )(x, mod, *consts)


def _scan_kernel(sc_ref, rk_ref, gng_ref, gnb_ref, y_ref, st_ref):
    bt, _, pp, lt, _ = sc_ref.shape
    cs = SCAN_CHUNK

    @pl.when(pl.program_id(0) == 0)
    def _():
        st_ref[...] = jnp.zeros(st_ref.shape, F32)

    lane = lax.broadcasted_iota(jnp.int32, (cs, LANES), 1)
    first_head = lane < R_HEAD

    def stack(t):
        return jnp.concatenate([jnp.where(first_head, t, 0.0),
                                jnp.where(first_head, 0.0, t)], axis=0)

    def fold(t):
        return jnp.where(first_head, t[:cs], t[cs:])

    ri = lax.broadcasted_iota(jnp.int32, (cs, cs), 0)
    ci = lax.broadcasted_iota(jnp.int32, (cs, cs), 1)
    tri_ones = jnp.where(ri >= ci, 1.0, 0.0).astype(BF16)
    si = lax.broadcasted_iota(jnp.int32, (cs, LANES), 0)
    sj = lane % cs
    strict = si > sj
    incl = si >= sj
    eye = jnp.where(si == sj, 1.0, 0.0).astype(F32)
    inv_base_mask = (si // INV_BASE) == (sj // INV_BASE)
    inv_merge_masks = []
    m = INV_BASE
    while m < cs:
        inv_merge_masks.append(((si // (2 * m)) == (sj // (2 * m)))
                               & ((si % (2 * m)) >= m) & ((sj % (2 * m)) < m))
        m *= 2
    bsi =lax.broadcasted_iota(jnp.int32, (LANES, LANES), 0) // R_HEAD
    bsj = lax.broadcasted_iota(jnp.int32, (LANES, LANES), 1) // R_HEAD
    head_ones = jnp.where(bsi == bsj, 1.0, 0.0).astype(BF16)
    head_mean = jnp.where(bsi == bsj, 1.0 / R_HEAD, 0.0).astype(BF16)

    chains = [(b, q) for q in range(pp) for b in range(bt)]
    chunks = [[dict(j=j, b=b, q=q, rows=slice(ck * cs, (ck + 1) * cs))
               for j, (b, q) in enumerate(chains)] for ck in range(lt // cs)]

    def s_cumsum(c):
        lw = sc_ref[c["b"], I_LW, c["q"], c["rows"], :]
        p1 = lw.astype(BF16)
        p2 = (lw - p1.astype(F32)).astype(BF16)
        c["lw"] = lw
        c["cum"] = _dot(tri_ones, p1) + _dot(tri_ones, p2)

    def s_operands(c):
        b, q, rows, cum, lw = c["b"], c["q"], c["rows"], c["cum"], c["lw"]
        last = cum[cs - 1:cs, :]
        e_pos = jnp.exp(cum)
        e_pos_ex = jnp.exp(cum - lw)
        e_neg = 1.0 / e_pos
        wcol = jnp.broadcast_to(jnp.exp(last), (LANES, LANES)).T
        r = sc_ref[b, I_R, q, rows, :]
        k = sc_ref[b, I_K, q, rows, :]
        bb = sc_ref[b, I_B, q, rows, :]
        c["at"] = sc_ref[b, I_A, q, rows, :] * e_pos_ex
        c["rt_b"] = (r * e_pos).astype(BF16)
        tb = stack(bb * e_neg).T
        tk = stack(k * e_neg).T
        c["vm"] = stack(sc_ref[b, I_V, q, rows, :]).astype(BF16)
        c["bt"] = fold(tb).astype(BF16)
        c["kt"] = fold(tk)
        c["wm"] = fold(wcol)
        c["aa"] = _dot(jnp.concatenate([c["at"].astype(BF16), c["rt_b"]], axis=0),
                       jnp.concatenate([tb, tk], axis=1).astype(BF16))
        lanes = slice(q * LANES, (q + 1) * LANES)
        c["rkr"] = (r * k * rk_ref[:, lanes]).astype(BF16)

    def s_intra(c):
        aa = c.pop("aa")
        a_ab = jnp.where(strict, aa[:cs, :LANES], 0.0)
        c["a_rb"] = jnp.where(incl, aa[cs:, :LANES], 0.0).astype(BF16)
        c["akv_y0"] = _dot(jnp.concatenate([jnp.where(strict, aa[:cs, LANES:], 0.0),
                                            jnp.where(incl, aa[cs:, LANES:], 0.0),
                                            c.pop("kt")], axis=0).astype(BF16), c.pop("vm"))
        a_d = jnp.where(inv_base_mask, a_ab, 0.0)
        c["a_ab"] = a_ab
        c["xk"] = _dot(a_d.astype(BF16), stack(a_d).astype(BF16))
        c["tk"] = eye + a_d

    def s_base_square(c):
        x_bd = stack(c["xk"]).astype(BF16)
        xt = _dot(jnp.concatenate([c["xk"], c["tk"]], axis=0).astype(BF16), x_bd)
        c["xk"] = xt[:cs]
        c["tk"] = c["tk"] + xt[cs:]

    def s_base_last(c):
        c["tk"] = c["tk"] + _dot(c["tk"].astype(BF16), stack(c.pop("xk")).astype(BF16))

    def s_merge_bt(c):
        a_ab = c.pop("a_ab")
        b_off = jnp.concatenate([jnp.where(mk, a_ab, 0.0) for mk in inv_merge_masks], axis=0)
        g = _dot(b_off.astype(BF16), stack(c["tk"]).astype(BF16))
        c["g"] = [g[i * cs:(i + 1) * cs] for i in range(len(inv_merge_masks))]

    def s_merge_tbt(c):
        g = c["g"]
        x = _dot(jnp.concatenate([c["tk"]] + g[1:], axis=0).astype(BF16),
                 stack(g[0]).astype(BF16))
        c["tk"] = c["tk"] + x[:cs]
        c["g"] = [g[i] + x[i * cs:(i + 1) * cs] for i in range(1, len(g))]

    def s_solve(c):
        del c["g"]
        c["pq"] = _dot(c.pop("tk").astype(BF16),
                       jnp.concatenate([stack(c.pop("at")), stack(c["akv_y0"][:cs])],
                                       axis=1).astype(BF16))

    def s_u(c):
        c["st"] = st_ref[c["j"]]
        pq = c.pop("pq")
        ps = _dot(jnp.concatenate([pq[:, :LANES].astype(BF16), c.pop("rt_b")], axis=0),
                  stack(c["st"]).astype(BF16))
        c["u_b"] = stack(ps[:cs] + pq[:, LANES:]).astype(BF16)
        c["rs"] = ps[cs:]

    def s_y(c):
        au = _dot(jnp.concatenate([c.pop("a_rb"), c.pop("bt")], axis=0), c.pop("u_b"))
        akv_y0 = c.pop("akv_y0")
        c["y"] = c.pop("rs") + au[:cs] + akv_y0[cs:2 * cs]
        st_ref[c["j"]] = c.pop("wm") * (c.pop("st") + au[cs:] + akv_y0[2 * cs:])

    def s_bonus(cstate):
        rk_sum = _dot(jnp.concatenate([c.pop("rkr") for c in cstate], axis=0), head_ones)
        for c in cstate:
            c["rk_sum"] = rk_sum[c["j"] * cs:(c["j"] + 1) * cs]

    def per_chain(stage):
        def run(cstate):
            for c in cstate:
                stage(c)
        return run

    def finish(cstate):
        per_chain(s_u)(cstate)
        gsz = max(1, len(cstate) // GN_BATCHES)
        batches = [cstate[i:i + gsz] for i in range(0, len(cstate), gsz)]
        d = {}
        for t in range(len(batches) + 2):
            if t < len(batches):
                per_chain(s_y)(batches[t])
            if 0 <= t - 1 < len(batches):
                y_b = jnp.concatenate([c.pop("y") for c in batches[t - 1]], axis=0)
                d[t - 1] = y_b - _dot(y_b.astype(BF16), head_mean)
            if 0 <= t - 2 < len(batches):
                d_b = d.pop(t - 2)
                yn = d_b * lax.rsqrt(_dot((d_b * d_b).astype(BF16), head_mean) + GN_EPS)
                for i, c in enumerate(batches[t - 2]):
                    b, q, rows = c["b"], c["q"], c["rows"]
                    lanes = slice(q * LANES, (q + 1) * LANES)
                    y_ref[b, q, rows, :] = (
                        yn[i * cs:(i + 1) * cs] * gng_ref[:, lanes] + gnb_ref[:, lanes]
                        + c.pop("rk_sum") * sc_ref[b, I_V, q, rows, :]).astype(y_ref.dtype)

    n_base_sq = (INV_BASE - 1).bit_length() - 1
    pre = ([per_chain(s_cumsum), per_chain(s_operands), per_chain(s_intra), s_bonus]
           + [per_chain(s_base_square)] * (n_base_sq - 1) + [per_chain(s_base_last)])
    pre += [per_chain(s_merge_bt)] + [per_chain(s_merge_tbt)] * len(inv_merge_masks)
    pre.append(per_chain(s_solve))
    for t in range(len(pre) + SCAN_LAG * (len(chunks) - 1)):
        for ck, cstate in enumerate(chunks):
            k = t - ck * SCAN_LAG
            if 0 <= k < len(pre):
                pre[k](cstate)
                if k == len(pre) - 1:
                    finish(cstate)


def _scan(sc, rk, gng, gnb):
    bsz, _, pp, seq, _ = sc.shape
    blk = pl.BlockSpec((bsz, pp, SCAN_TILE, LANES), lambda s: (0, 0, s, 0))
    vec = pl.BlockSpec((1, pp * LANES), lambda s: (0, 0))
    return pl.pallas_call(
        _scan_kernel,
        grid=(seq // SCAN_TILE,),
        in_specs=[pl.BlockSpec((bsz, 6, pp, SCAN_TILE, LANES), lambda s: (0, 0, 0, s, 0))] + [vec] * 3,
        out_specs=blk,
        out_shape=jax.ShapeDtypeStruct((bsz, pp, seq, LANES), BF16),
        scratch_shapes=[pltpu.VMEM((bsz * pp, SCAN_CHUNK, LANES), F32)],
        compiler_params=pltpu.CompilerParams(
            dimension_semantics=("arbitrary",), vmem_limit_bytes=VMEM_LIMIT),
        name="scan",
    )(sc, rk, gng, gnb)


def _post_kernel(x_ref, mod_ref, yb_ref, g_ref, gb_ref, ma_ref, wbb_ref, wout_ref, bout_ref,
                 ln1g_ref, ln1b_ref, w1_ref, b1_ref, w2_ref, b2_ref, ln2g_ref, ln2b_ref, o_ref):
    tl = x_ref.shape[1]
    gt1 = mod_ref[0, 2:3, :]
    sh2 = mod_ref[0, 3:4, :]
    sc2 = mod_ref[0, 4:5, :]
    gt2 = mod_ref[0, 5:6, :]
    groups = [slice(i * tl // POST_GROUPS, (i + 1) * tl // POST_GROUPS) for i in range(POST_GROUPS)]
    pb = []
    for rows in groups:
        yb = (jnp.concatenate([yb_ref[0, q, rows, :] for q in range(PAIRS)], axis=1).astype(F32)
              * g_ref[0, rows, :].astype(F32))
        pb.append(_dot(yb.astype(BF16), wbb_ref[...]))
    h1 = []
    for i, rows in enumerate(groups):
        merged = ma_ref[0, rows, :].astype(F32) + gb_ref[0, rows, :].astype(F32) * pb[i]
        mix = _dot(merged.astype(BF16), wout_ref[...]) + bout_ref[...]
        h1.append(_layer_norm(ALPHA * x_ref[0, rows, :] + gt1 * mix,
                              ln1g_ref[...], ln1b_ref[...], LN_EPS))
    t = []
    for i in range(POST_GROUPS):
        h = (h1[i] * (1.0 + sc2) + sh2).astype(BF16)
        t.append(jnp.maximum(_dot(h, w1_ref[...]) + b1_ref[...], 0.0))
    for i, rows in enumerate(groups):
        ff = _dot((t[i] * t[i]).astype(BF16), w2_ref[...]) + b2_ref[...]
        o_ref[0, rows, :] = _layer_norm(ALPHA * h1[i] + gt2 * ff,
                                        ln2g_ref[...], ln2b_ref[...], LN_EPS).astype(o_ref.dtype)


def _post(x, mod, yb, g, gb, ma, p, tl, out_dtype):
    bsz, seq, _ = x.shape
    row = lambda b, s: (b, s, 0)
    consts = [p["wbb"], p["wout"], p["bout"], p["ln1g"], p["ln1b"],
              p["w1"], p["b1"], p["w2"], p["b2"], p["ln2g"], p["ln2b"]]
    return pl.pallas_call(
        _post_kernel,
        grid=(bsz, seq // tl),
        in_specs=[pl.BlockSpec((1, tl, D_MODEL), row),
                  pl.BlockSpec((1, 6, D_MODEL), lambda b, s: (b, 0, 0)),
                  pl.BlockSpec((1, PAIRS, tl, LANES), lambda b, s: (b, 0, s, 0)),
                  pl.BlockSpec((1, tl, R_WIDTH), row),
                  pl.BlockSpec((1, tl, D_MODEL), row),
                  pl.BlockSpec((1, tl, D_MODEL), row)]
                 + [_const_spec(c.shape) for c in consts],
        out_specs=pl.BlockSpec((1, tl, D_MODEL), row),
        out_shape=jax.ShapeDtypeStruct((bsz, seq, D_MODEL), out_dtype),
        compiler_params=pltpu.CompilerParams(
            dimension_semantics=("arbitrary", "arbitrary"), vmem_limit_bytes=VMEM_LIMIT),
        name="post",
    )(x, mod, yb, g, gb, ma, *consts)


W_IN_SPLIT = 2 * G_WIDTH + 3 * R_WIDTH + LORA_COLS
WPREP_COLS = 2 * LANES
GAP_BLOCK = W_IN_SPLIT // WPREP_COLS


def _relayout_w_in_kernel(wt_ref, o_ref):
    j = pl.program_id(0)
    rows = lax.broadcasted_iota(jnp.int32, wt_ref.shape, 0)
    keep = jnp.logical_or(j != GAP_BLOCK, rows < W_IN_SPLIT % WPREP_COLS)
    o_ref[...] = jnp.where(keep, wt_ref[...], 0.0).T.astype(BF16)


def _relayout_w_in(w_in, l):
    wt = jnp.transpose(w_in[l])
    gap = COLS_R.stop - W_IN_SPLIT

    unit = math.gcd(WPREP_COLS, gap)

    def src_row(j):
        return (j * (WPREP_COLS // unit) - jnp.where(j <= GAP_BLOCK, 0, gap // unit)) * unit

    return pl.pallas_call(
        _relayout_w_in_kernel,
        grid=(COLS_GB.stop // WPREP_COLS,),
        in_specs=[pl.BlockSpec((pl.Element(WPREP_COLS), pl.Element(D_MODEL)),
                               lambda j: (src_row(j), 0))],
        out_specs=pl.BlockSpec((D_MODEL, WPREP_COLS), lambda j: (0, j)),
        out_shape=jax.ShapeDtypeStruct((D_MODEL, COLS_GB.stop), BF16),
        name="wprep",
    )(wt)


def _prepare_params(l, w_in, b_in, g_ln_v, b_ln_v, w_spatial, b_spatial, mu_shift, w0,
                    w_decay_up, a0, w_aaa_up, w_gate_up, k_k, k_a, r_k, gn_gain, gn_bias,
                    w_branch_a, w_branch_b, w_out, b_out, ln1_g, ln1_b, w_ff1, b_ff1, w_ff2,
                    b_ff2, ln2_g, ln2_b):
    g_end = 2 * G_WIDTH
    rkv_end = g_end + 3 * R_WIDTH
    r_end = rkv_end + LORA_COLS
    pad = LORA_PAD - LORA_COLS
    row2 = lambda t: t.reshape(1, -1)
    wi, bi = w_in[l], b_in[l]
    b_all = jnp.concatenate([bi[:r_end], jnp.zeros((pad,), F32), bi[r_end:]])
    mu = jnp.concatenate([mu_shift[l], jnp.zeros((pad,), F32)])
    wup = jnp.zeros((LORA_PAD, 3 * R_WIDTH), F32)
    wup = wup.at[0:DECAY_LORA, 0:R_WIDTH].set(w_decay_up[l])
    wup = wup.at[DECAY_LORA:DECAY_LORA + AAA_LORA, R_WIDTH:2 * R_WIDTH].set(w_aaa_up[l])
    wup = wup.at[DECAY_LORA + AAA_LORA:LORA_COLS, 2 * R_WIDTH:].set(w_gate_up[l])
    ws = w_spatial[l].reshape(PAIRS, 2, CHUNK, CHUNK).transpose(0, 2, 1, 3).reshape(PAIRS, CHUNK, 2 * CHUNK)
    bs = jnp.repeat(b_spatial[l].T, G_WIDTH // G_GROUPS, axis=1)
    hid = jnp.arange(R_WIDTH) // R_HEAD
    bd = (hid[:, None] == hid[None, :]).astype(BF16)
    return dict(
        win=_relayout_w_in(w_in, l), bin=row2(b_all),
        glnv=row2(g_ln_v[l]), blnv=row2(b_ln_v[l]), ws=ws, bs=bs, mu=row2(mu),
        w0=row2(w0[l]), a0=row2(a0[l]), wup=wup.astype(BF16), kk=row2(k_k[l]), ka=row2(k_a[l]),
        bd=bd, wba=w_branch_a[l].astype(BF16),
        rk=row2(r_k[l]), gng=row2(gn_gain[l]), gnb=row2(gn_bias[l]),
        wbb=w_branch_b[l].astype(BF16), wout=w_out[l].astype(BF16), bout=row2(b_out[l]),
        ln1g=row2(ln1_g[l]), ln1b=row2(ln1_b[l]),
        w1=w_ff1[l].astype(BF16), b1=row2(b_ff1[l]), w2=w_ff2[l].astype(BF16), b2=row2(b_ff2[l]),
        ln2g=row2(ln2_g[l]), ln2b=row2(ln2_b[l]),
    )


def _tile(seq, want):
    t = min(want, seq)
    while seq % t:
        t //= 2
    return t


def kernel(x, c, w_ada, b_ada, w_in, b_in, g_ln_v, b_ln_v, w_spatial, b_spatial, mu_shift, w0, w_decay_up, a0, w_aaa_up, w_gate_up, k_k, k_a, r_k, gn_gain, gn_bias, w_branch_a, w_branch_b, w_out, b_out, ln1_g, ln1_b, w_ff1, b_ff1, w_ff2, b_ff2, ln2_g, ln2_b):
    bsz, seq, _ = x.shape
    assert seq % CHUNK == 0 and x.shape[2] == D_MODEL
    out_dtype = x.dtype
    h_res = x.astype(F32)
    tl = _tile(seq, 512)
    for l in range(DEPTH):
        p = _prepare_params(l, w_in, b_in, g_ln_v, b_ln_v, w_spatial, b_spatial, mu_shift, w0,
                            w_decay_up, a0, w_aaa_up, w_gate_up, k_k, k_a, r_k, gn_gain, gn_bias,
                            w_branch_a, w_branch_b, w_out, b_out, ln1_g, ln1_b, w_ff1, b_ff1,
                            w_ff2, b_ff2, ln2_g, ln2_b)
        mod = _modulation(c.astype(F32), w_ada, b_ada, l).reshape(bsz, 6, D_MODEL)
        ma, gb, sc, g = _inproj(h_res, mod, p, tl)
        yb = _scan(sc, p["rk"], p["gng"], p["gnb"])
        h_res = _post(h_res, mod, yb, g, gb, ma, p, tl, F32)
    return h_res.astype(out_dtype)
```

```python
import math

import jax
import jax.numpy as jnp
from jax import lax
from jax.experimental import pallas as pl
from jax.experimental.pallas import tpu as pltpu

D_MODEL = 1024
G_GROUPS = 8
G_WIDTH = 512
CHUNK = 128
R_WIDTH = 512
R_HEAD = 64
DECAY_LORA = 32
AAA_LORA = 32
GATE_LORA = 96
LORA_COLS = DECAY_LORA + AAA_LORA + GATE_LORA
DEPTH = 1
ALPHA = (2.0 * DEPTH) ** 0.25
LN_EPS = 1e-5
GN_EPS = 64e-5
DECAY_SCALE = math.exp(-0.5)

LANES = 128
PAIRS = R_WIDTH // LANES
LORA_PAD = 2 * LANES
R_COLS = 3 * R_WIDTH + LORA_PAD
COLS_U = slice(0, G_WIDTH)
COLS_V = slice(G_WIDTH, 2 * G_WIDTH)
COLS_R = slice(2 * G_WIDTH, 2 * G_WIDTH + R_COLS)
COLS_GA = slice(COLS_R.stop, COLS_R.stop + D_MODEL)
COLS_GB = slice(COLS_GA.stop, COLS_GA.stop + D_MODEL)
SCAN_CHUNK = 64
INV_BASE = 8
POST_GROUPS = 2
IN_GROUPS = 2
I_R, I_LW, I_K, I_V, I_A, I_B = range(6)
GN_BATCHES = 4
SCAN_TILE = 2 * SCAN_CHUNK
SCAN_LAG = 3
VMEM_LIMIT = 56 * 1024 * 1024

F32 = jnp.float32
BF16 = jnp.bfloat16


def _dot(a, b):
    return jnp.dot(a, b, preferred_element_type=F32)


def _sigmoid(x):
    return 1.0 / (1.0 + jnp.exp(-x))


def _gelu_tanh(x):
    c = 0.7978845608028654
    hx = 0.5 * x
    return hx + hx * jnp.tanh(x * (c + (c * 0.044715) * (x * x)))


def _layer_norm(x, g, b, eps):
    mu = jnp.mean(x, axis=-1, keepdims=True)
    d = x - mu
    var = jnp.mean(d * d, axis=-1, keepdims=True)
    return d * lax.rsqrt(var + eps) * g + b


def _const_spec(shape):
    n = len(shape)
    return pl.BlockSpec(shape, lambda *_: (0,) * n)


def _mod_kernel(c_ref, w_ref, b_ref, o_ref):
    c = c_ref[...]
    c_act = c * _sigmoid(c)
    bsz = c.shape[0]
    ch = c_act.astype(BF16)
    cl = (c_act - ch.astype(F32)).astype(BF16)
    w = w_ref[...]
    wh = w.astype(BF16)
    wl = (w - wh.astype(F32)).astype(BF16)
    hh = _dot(jnp.concatenate([ch, cl], axis=0), wh)
    o_ref[...] = hh[:bsz] + hh[bsz:] + _dot(ch, wl) + b_ref[...]


def _modulation(c, w_ada, b_ada, l):
    bsz = c.shape[0]
    n = w_ada.shape[2]
    tn = D_MODEL
    return pl.pallas_call(
        _mod_kernel,
        grid=(n // tn,),
        in_specs=[pl.BlockSpec((bsz, D_MODEL), lambda j: (0, 0)),
                  pl.BlockSpec((None, D_MODEL, tn), lambda j: (l, 0, j)),
                  pl.BlockSpec((None, 1, tn), lambda j: (l, 0, j))],
        out_specs=pl.BlockSpec((bsz, tn), lambda j: (0, j)),
        out_shape=jax.ShapeDtypeStruct((bsz, n), F32),
        name="mod",
    )(c, w_ada, b_ada.reshape(b_ada.shape[0], 1, n))


def _inproj_kernel(x_ref, mod_ref, win_ref, bin_ref,
                   glnv_ref, blnv_ref, ws_ref, bs_ref, mu_ref, w0_ref, a0_ref, wup_ref,
                   kk_ref, ka_ref, bd_ref, wba_ref,
                   ma_ref, gb_ref, sc_ref, g_ref,
                   zsh_ref):
    tl = x_ref.shape[1]

    @pl.when(pl.program_id(1) == 0)
    def _():
        zsh_ref[0:8, :] = jnp.zeros((8, R_COLS), F32)

    sh1 = mod_ref[0, 0:1, :]
    sc1 = mod_ref[0, 1:2, :]
    gsz = tl // IN_GROUPS
    groups = [slice(i * gsz, (i + 1) * gsz) for i in range(IN_GROUPS)]
    gi = range(IN_GROUPS)
    h = [(x_ref[0, rows, :] * (1.0 + sc1) + sh1).astype(BF16) for rows in groups]
    def proj(i, cols):
        return _dot(h[i], win_ref[:, cols]) + bin_ref[:, cols]

    zr = [proj(i, COLS_R) for i in gi]
    for i, rows in enumerate(groups):
        zsh_ref[8 + rows.start:8 + rows.stop, :] = zr[i]
    zu = [proj(i, COLS_U) for i in gi]
    zv = [proj(i, COLS_V) for i in gi]

    z = []
    for i, rows in enumerate(groups):
        prev = zsh_ref[7 + rows.start:7 + rows.stop, :]
        z.append(zr[i] + (prev - zr[i]) * mu_ref[...])
    zsh_ref[7:8, :] = zr[-1][gsz - 1:gsz, :]
    llane = lax.broadcasted_iota(jnp.int32, (gsz, LORA_PAD), 1)
    lin = []
    for i in gi:
        xl = z[i][:, 3 * R_WIDTH:]
        lin.append(jnp.where(llane < DECAY_LORA, jnp.tanh(xl),
                             jnp.where(llane < DECAY_LORA + AAA_LORA, xl, _sigmoid(xl))).astype(BF16))
    zgb = [proj(i, COLS_GB) for i in gi]
    up = [_dot(lin[i], wup_ref[...]) for i in gi]

    u = [_gelu_tanh(zu[i]) for i in gi]
    v = [_layer_norm(_gelu_tanh(zv[i]), glnv_ref[...], blnv_ref[...], LN_EPS) for i in gi]

    kk, n2 = [], []
    for i, rows in enumerate(groups):
        lw = -DECAY_SCALE * _sigmoid(w0_ref[...] + up[i][:, 0:R_WIDTH])
        g_ref[0, rows, :] = up[i][:, 2 * R_WIDTH:].astype(g_ref.dtype)
        kk.append(z[i][:, R_WIDTH:2 * R_WIDTH] * kk_ref[...])
        n2.append(_dot((kk[i] * kk[i]).astype(BF16), bd_ref[...]))
        for q in range(PAIRS):
            sl = slice(q * LANES, (q + 1) * LANES)
            sc_ref[0, I_LW, q, rows, :] = lw[:, sl]
            sc_ref[0, I_R, q, rows, :] = z[i][:, sl]
            sc_ref[0, I_V, q, rows, :] = z[i][:, 2 * R_WIDTH + q * LANES:2 * R_WIDTH + (q + 1) * LANES]

    lane = lax.broadcasted_iota(jnp.int32, (CHUNK, LANES), 1)
    first_head = lane < R_HEAD
    trow = lax.broadcasted_iota(jnp.int32, (CHUNK, 2 * CHUNK), 0)
    scol = lax.broadcasted_iota(jnp.int32, (CHUNK, 2 * CHUNK), 1) % CHUNK
    causal = trow >= scol
    ws = [jnp.where(causal, ws_ref[q], 0.0).astype(BF16) for q in range(PAIRS)]
    ya = []
    for i in gi:
        ya_rows = []
        for c in range(gsz // CHUNK):
            vc = v[i][c * CHUNK:(c + 1) * CHUNK, :]
            s_parts = []
            for q in range(PAIRS):
                v2 = vc[:, q * LANES:(q + 1) * LANES]
                vm = jnp.concatenate([jnp.where(first_head, v2, 0.0),
                                      jnp.where(first_head, 0.0, v2)], axis=0).astype(BF16)
                s_parts.append(_dot(ws[q], vm))
            s = jnp.concatenate(s_parts, axis=1) + bs_ref[...]
            ya_rows.append(u[i][c * CHUNK:(c + 1) * CHUNK, :] * s)
        ya.append(jnp.concatenate(ya_rows, axis=0).astype(BF16))
    zga = [proj(i, COLS_GA) for i in gi]
    pa = [_dot(ya[i], wba_ref[...]) for i in gi]

    for i, rows in enumerate(groups):
        gb_ref[0, rows, :] = _sigmoid(zgb[i]).astype(gb_ref.dtype)
        a = _sigmoid(a0_ref[...] + up[i][:, R_WIDTH:2 * R_WIDTH])
        kkn = kk[i] * lax.rsqrt(jnp.maximum(n2[i], 1e-24))
        k2 = z[i][:, R_WIDTH:2 * R_WIDTH] * (1.0 + (a - 1.0) * ka_ref[...])
        nkk = -kkn
        kka = kkn * a
        for q in range(PAIRS):
            sl = slice(q * LANES, (q + 1) * LANES)
            sc_ref[0, I_K, q, rows, :] = k2[:, sl]
            sc_ref[0, I_A, q, rows, :] = nkk[:, sl]
            sc_ref[0, I_B, q, rows, :] = kka[:, sl]
        ma_ref[0, rows, :] = (_sigmoid(zga[i]) * pa[i]).astype(ma_ref.dtype)


def _inproj(x, mod, p, tl):
    bsz, seq, _ = x.shape
    grid = (bsz, seq // tl)
    row = lambda b, s: (b, s, 0)
    sc_shape = jax.ShapeDtypeStruct((bsz, 6, PAIRS, seq, LANES), F32)
    sc_spec = pl.BlockSpec((1, 6, PAIRS, tl, LANES), lambda b, s: (b, 0, 0, s, 0))
    consts = [p["win"], p["bin"], p["glnv"], p["blnv"],
              p["ws"], p["bs"], p["mu"], p["w0"], p["a0"], p["wup"], p["kk"], p["ka"], p["bd"],
              p["wba"]]
    return pl.pallas_call(
        _inproj_kernel,
        grid=grid,
        in_specs=[pl.BlockSpec((1, tl, D_MODEL), row),
                  pl.BlockSpec((1, 6, D_MODEL), lambda b, s: (b, 0, 0))]
                 + [_const_spec(c.shape) for c in consts],
        out_specs=[pl.BlockSpec((1, tl, D_MODEL), row),
                   pl.BlockSpec((1, tl, D_MODEL), row)]
                  + [sc_spec]
                  + [pl.BlockSpec((1, tl, R_WIDTH), row)],
        out_shape=[jax.ShapeDtypeStruct((bsz, seq, D_MODEL), BF16),
                   jax.ShapeDtypeStruct((bsz, seq, D_MODEL), BF16)]
                  + [sc_shape]
                  + [jax.ShapeDtypeStruct((bsz, seq, R_WIDTH), BF16)],
        scratch_shapes=[pltpu.VMEM((tl + 8, R_COLS), F32)],
        compiler_params=pltpu.CompilerParams(
            dimension_semantics=("arbitrary", "arbitrary"), vmem_limit_bytes=VMEM_LIMIT),
        name="inproj",
    )(x, mod, *consts)


def _scan_kernel(sc_ref, rk_ref, gng_ref, gnb_ref, y_ref, st_ref):
    bt, _, pp, lt, _ = sc_ref.shape
    cs = SCAN_CHUNK

    @pl.when(pl.program_id(0) == 0)
    def _():
        st_ref[...] = jnp.zeros(st_ref.shape, F32)

    lane = lax.broadcasted_iota(jnp.int32, (cs, LANES), 1)
    first_head = lane < R_HEAD

    def stack(t):
        return jnp.concatenate([jnp.where(first_head, t, 0.0),
                                jnp.where(first_head, 0.0, t)], axis=0)

    def fold(t):
        return jnp.where(first_head, t[:cs], t[cs:])

    ri = lax.broadcasted_iota(jnp.int32, (cs, cs), 0)
    ci = lax.broadcasted_iota(jnp.int32, (cs, cs), 1)
    tri_ones = jnp.where(ri >= ci, 1.0, 0.0).astype(BF16)
    si = lax.broadcasted_iota(jnp.int32, (cs, LANES), 0)
    sj = lane % cs
    strict = si > sj
    incl = si >= sj
    eye = jnp.where(si == sj, 1.0, 0.0).astype(F32)
    inv_base_mask = (si // INV_BASE) == (sj // INV_BASE)
    inv_merge_masks = []
    m = INV_BASE
    while m < cs:
        inv_merge_masks.append(((si // (2 * m)) == (sj // (2 * m)))
                               & ((si % (2 * m)) >= m) & ((sj % (2 * m)) < m))
        m *= 2
    bsi =lax.broadcasted_iota(jnp.int32, (LANES, LANES), 0) // R_HEAD
    bsj = lax.broadcasted_iota(jnp.int32, (LANES, LANES), 1) // R_HEAD
    head_ones = jnp.where(bsi == bsj, 1.0, 0.0).astype(BF16)
    head_mean = jnp.where(bsi == bsj, 1.0 / R_HEAD, 0.0).astype(BF16)

    chains = [(b, q) for q in range(pp) for b in range(bt)]
    chunks = [[dict(j=j, b=b, q=q, rows=slice(ck * cs, (ck + 1) * cs))
               for j, (b, q) in enumerate(chains)] for ck in range(lt // cs)]

    def s_cumsum(c):
        lw = sc_ref[c["b"], I_LW, c["q"], c["rows"], :]
        p1 = lw.astype(BF16)
        p2 = (lw - p1.astype(F32)).astype(BF16)
        c["lw"] = lw
        c["cum"] = _dot(tri_ones, p1) + _dot(tri_ones, p2)

    def s_operands(c):
        b, q, rows, cum, lw = c["b"], c["q"], c["rows"], c["cum"], c["lw"]
        last = cum[cs - 1:cs, :]
        e_pos = jnp.exp(cum)
        e_pos_ex = jnp.exp(cum - lw)
        e_neg = 1.0 / e_pos
        wcol = jnp.broadcast_to(jnp.exp(last), (LANES, LANES)).T
        r = sc_ref[b, I_R, q, rows, :]
        k = sc_ref[b, I_K, q, rows, :]
        bb = sc_ref[b, I_B, q, rows, :]
        c["at"] = sc_ref[b, I_A, q, rows, :] * e_pos_ex
        c["rt_b"] = (r * e_pos).astype(BF16)
        tb = stack(bb * e_neg).T
        tk = stack(k * e_neg).T
        c["vm"] = stack(sc_ref[b, I_V, q, rows, :]).astype(BF16)
        c["bt"] = fold(tb).astype(BF16)
        c["kt"] = fold(tk)
        c["wm"] = fold(wcol)
        c["aa"] = _dot(jnp.concatenate([c["at"].astype(BF16), c["rt_b"]], axis=0),
                       jnp.concatenate([tb, tk], axis=1).astype(BF16))
        lanes = slice(q * LANES, (q + 1) * LANES)
        c["rkr"] = (r * k * rk_ref[:, lanes]).astype(BF16)

    def s_intra(c):
        aa = c.pop("aa")
        a_ab = jnp.where(strict, aa[:cs, :LANES], 0.0)
        c["a_rb"] = jnp.where(incl, aa[cs:, :LANES], 0.0).astype(BF16)
        c["akv_y0"] = _dot(jnp.concatenate([jnp.where(strict, aa[:cs, LANES:], 0.0),
                                            jnp.where(incl, aa[cs:, LANES:], 0.0),
                                            c.pop("kt")], axis=0).astype(BF16), c.pop("vm"))
        a_d = jnp.where(inv_base_mask, a_ab, 0.0)
        c["a_ab"] = a_ab
        c["xk"] = _dot(a_d.astype(BF16), stack(a_d).astype(BF16))
        c["tk"] = eye + a_d

    def s_base_square(c):
        x_bd = stack(c["xk"]).astype(BF16)
        xt = _dot(jnp.concatenate([c["xk"], c["tk"]], axis=0).astype(BF16), x_bd)
        c["xk"] = xt[:cs]
        c["tk"] = c["tk"] + xt[cs:]

    def s_base_last(c):
        c["tk"] = c["tk"] + _dot(c["tk"].astype(BF16), stack(c.pop("xk")).astype(BF16))

    half = cs // 2
    sub = 8

    def lower_pieces(level):
        m = INV_BASE << level
        return [p for p in range(cs // sub) if (p * sub) % (2 * m) >= m]

    def lower(t, level):
        return jnp.concatenate([t[p * sub:(p + 1) * sub] for p in lower_pieces(level)], axis=0)

    def expand(x, level, base=None):
        pieces = lower_pieces(level)
        out = []
        for p in range(cs // sub):
            rest = jnp.zeros((sub, LANES), F32) if base is None else base[p * sub:(p + 1) * sub]
            if p in pieces:
                i = pieces.index(p)
                piece = x[i * sub:(i + 1) * sub]
                rest = piece if base is None else rest + piece
            out.append(rest)
        return jnp.concatenate(out, axis=0)

    def s_merge_bt(c):
        a_ab = c.pop("a_ab")
        b_off = jnp.concatenate([lower(jnp.where(mk, a_ab, 0.0), level)
                                 for level, mk in enumerate(inv_merge_masks)], axis=0)
        g = _dot(b_off.astype(BF16), stack(c["tk"]).astype(BF16))
        c["g"] = [g[i * half:(i + 1) * half] for i in range(len(inv_merge_masks))]

    def s_merge_tbt(level):
        def run(c):
            g = c["g"]
            x = _dot(jnp.concatenate([lower(c["tk"], level)] + g[1:], axis=0).astype(BF16),
                     stack(expand(g[0], level)).astype(BF16))
            c["tk"] = expand(x[:half], level, base=c["tk"])
            c["g"] = [g[i] + x[i * half:(i + 1) * half] for i in range(1, len(g))]
        return run

    def s_solve(c):
        del c["g"]
        c["pq"] = _dot(c.pop("tk").astype(BF16),
                       jnp.concatenate([stack(c.pop("at")), stack(c["akv_y0"][:cs])],
                                       axis=1).astype(BF16))

    def s_u(c):
        c["st"] = st_ref[c["j"]]
        pq = c.pop("pq")
        ps = _dot(jnp.concatenate([pq[:, :LANES].astype(BF16), c.pop("rt_b")], axis=0),
                  stack(c["st"]).astype(BF16))
        c["u_b"] = stack(ps[:cs] + pq[:, LANES:]).astype(BF16)
        c["rs"] = ps[cs:]

    def s_y(c):
        au = _dot(jnp.concatenate([c.pop("a_rb"), c.pop("bt")], axis=0), c.pop("u_b"))
        akv_y0 = c.pop("akv_y0")
        c["y"] = c.pop("rs") + au[:cs] + akv_y0[cs:2 * cs]
        st_ref[c["j"]] = c.pop("wm") * (c.pop("st") + au[cs:] + akv_y0[2 * cs:])

    def s_bonus(cstate):
        rk_sum = _dot(jnp.concatenate([c.pop("rkr") for c in cstate], axis=0), head_ones)
        for c in cstate:
            c["rk_sum"] = rk_sum[c["j"] * cs:(c["j"] + 1) * cs]

    def per_chain(stage):
        def run(cstate):
            for c in cstate:
                stage(c)
        return run

    def finish(cstate):
        per_chain(s_u)(cstate)
        gsz = max(1, len(cstate) // GN_BATCHES)
        batches = [cstate[i:i + gsz] for i in range(0, len(cstate), gsz)]
        d = {}
        for t in range(len(batches) + 2):
            if t < len(batches):
                per_chain(s_y)(batches[t])
            if 0 <= t - 1 < len(batches):
                y_b = jnp.concatenate([c.pop("y") for c in batches[t - 1]], axis=0)
                d[t - 1] = y_b - _dot(y_b.astype(BF16), head_mean)
            if 0 <= t - 2 < len(batches):
                d_b = d.pop(t - 2)
                yn = d_b * lax.rsqrt(_dot((d_b * d_b).astype(BF16), head_mean) + GN_EPS)
                for i, c in enumerate(batches[t - 2]):
                    b, q, rows = c["b"], c["q"], c["rows"]
                    lanes = slice(q * LANES, (q + 1) * LANES)
                    y_ref[b, q, rows, :] = (
                        yn[i * cs:(i + 1) * cs] * gng_ref[:, lanes] + gnb_ref[:, lanes]
                        + c.pop("rk_sum") * sc_ref[b, I_V, q, rows, :]).astype(y_ref.dtype)

    n_base_sq = (INV_BASE - 1).bit_length() - 1
    pre = ([per_chain(s_cumsum), per_chain(s_operands), per_chain(s_intra), s_bonus]
           + [per_chain(s_base_square)] * (n_base_sq - 1) + [per_chain(s_base_last)])
    pre += [per_chain(s_merge_bt)]
    pre += [per_chain(s_merge_tbt(level)) for level in range(len(inv_merge_masks))]
    pre.append(per_chain(s_solve))
    for t in range(len(pre) + SCAN_LAG * (len(chunks) - 1)):
        for ck, cstate in enumerate(chunks):
            k = t - ck * SCAN_LAG
            if 0 <= k < len(pre):
                pre[k](cstate)
                if k == len(pre) - 1:
                    finish(cstate)


def _scan(sc, rk, gng, gnb):
    bsz, _, pp, seq, _ = sc.shape
    blk = pl.BlockSpec((bsz, pp, SCAN_TILE, LANES), lambda s: (0, 0, s, 0))
    vec = pl.BlockSpec((1, pp * LANES), lambda s: (0, 0))
    return pl.pallas_call(
        _scan_kernel,
        grid=(seq // SCAN_TILE,),
        in_specs=[pl.BlockSpec((bsz, 6, pp, SCAN_TILE, LANES), lambda s: (0, 0, 0, s, 0))] + [vec] * 3,
        out_specs=blk,
        out_shape=jax.ShapeDtypeStruct((bsz, pp, seq, LANES), BF16),
        scratch_shapes=[pltpu.VMEM((bsz * pp, SCAN_CHUNK, LANES), F32)],
        compiler_params=pltpu.CompilerParams(
            dimension_semantics=("arbitrary",), vmem_limit_bytes=VMEM_LIMIT),
        name="scan",
    )(sc, rk, gng, gnb)


def _post_kernel(x_ref, mod_ref, yb_ref, g_ref, gb_ref, ma_ref, wbb_ref, wout_ref, bout_ref,
                 ln1g_ref, ln1b_ref, w1_ref, b1_ref, w2_ref, b2_ref, ln2g_ref, ln2b_ref, o_ref):
    tl = x_ref.shape[1]
    gt1 = mod_ref[0, 2:3, :]
    sh2 = mod_ref[0, 3:4, :]
    sc2 = mod_ref[0, 4:5, :]
    gt2 = mod_ref[0, 5:6, :]
    groups = [slice(i * tl // POST_GROUPS, (i + 1) * tl // POST_GROUPS) for i in range(POST_GROUPS)]
    pb = []
    for rows in groups:
        yb = (jnp.concatenate([yb_ref[0, q, rows, :] for q in range(PAIRS)], axis=1).astype(F32)
              * g_ref[0, rows, :].astype(F32))
        pb.append(_dot(yb.astype(BF16), wbb_ref[...]))
    h1 = []
    for i, rows in enumerate(groups):
        merged = ma_ref[0, rows, :].astype(F32) + gb_ref[0, rows, :].astype(F32) * pb[i]
        mix = _dot(merged.astype(BF16), wout_ref[...]) + bout_ref[...]
        h1.append(_layer_norm(ALPHA * x_ref[0, rows, :] + gt1 * mix,
                              ln1g_ref[...], ln1b_ref[...], LN_EPS))
    t = []
    for i in range(POST_GROUPS):
        h = (h1[i] * (1.0 + sc2) + sh2).astype(BF16)
        t.append(jnp.maximum(_dot(h, w1_ref[...]) + b1_ref[...], 0.0))
    for i, rows in enumerate(groups):
        ff = _dot((t[i] * t[i]).astype(BF16), w2_ref[...]) + b2_ref[...]
        o_ref[0, rows, :] = _layer_norm(ALPHA * h1[i] + gt2 * ff,
                                        ln2g_ref[...], ln2b_ref[...], LN_EPS).astype(o_ref.dtype)


def _post(x, mod, yb, g, gb, ma, p, tl, out_dtype):
    bsz, seq, _ = x.shape
    row = lambda b, s: (b, s, 0)
    consts = [p["wbb"], p["wout"], p["bout"], p["ln1g"], p["ln1b"],
              p["w1"], p["b1"], p["w2"], p["b2"], p["ln2g"], p["ln2b"]]
    return pl.pallas_call(
        _post_kernel,
        grid=(bsz, seq // tl),
        in_specs=[pl.BlockSpec((1, tl, D_MODEL), row),
                  pl.BlockSpec((1, 6, D_MODEL), lambda b, s: (b, 0, 0)),
                  pl.BlockSpec((1, PAIRS, tl, LANES), lambda b, s: (b, 0, s, 0)),
                  pl.BlockSpec((1, tl, R_WIDTH), row),
                  pl.BlockSpec((1, tl, D_MODEL), row),
                  pl.BlockSpec((1, tl, D_MODEL), row)]
                 + [_const_spec(c.shape) for c in consts],
        out_specs=pl.BlockSpec((1, tl, D_MODEL), row),
        out_shape=jax.ShapeDtypeStruct((bsz, seq, D_MODEL), out_dtype),
        compiler_params=pltpu.CompilerParams(
            dimension_semantics=("arbitrary", "arbitrary"), vmem_limit_bytes=VMEM_LIMIT),
        name="post",
    )(x, mod, yb, g, gb, ma, *consts)


W_IN_SPLIT = 2 * G_WIDTH + 3 * R_WIDTH + LORA_COLS
WPREP_COLS = 2 * LANES
GAP_BLOCK = W_IN_SPLIT // WPREP_COLS


def _relayout_w_in_kernel(wt_ref, o_ref):
    j = pl.program_id(0)
    rows = lax.broadcasted_iota(jnp.int32, wt_ref.shape, 0)
    keep = jnp.logical_or(j != GAP_BLOCK, rows < W_IN_SPLIT % WPREP_COLS)
    o_ref[...] = jnp.where(keep, wt_ref[...], 0.0).T.astype(BF16)


def _relayout_w_in(w_in, l):
    wt = jnp.transpose(w_in[l])
    gap = COLS_R.stop - W_IN_SPLIT

    unit = math.gcd(WPREP_COLS, gap)

    def src_row(j):
        return (j * (WPREP_COLS // unit) - jnp.where(j <= GAP_BLOCK, 0, gap // unit)) * unit

    return pl.pallas_call(
        _relayout_w_in_kernel,
        grid=(COLS_GB.stop // WPREP_COLS,),
        in_specs=[pl.BlockSpec((pl.Element(WPREP_COLS), pl.Element(D_MODEL)),
                               lambda j: (src_row(j), 0))],
        out_specs=pl.BlockSpec((D_MODEL, WPREP_COLS), lambda j: (0, j)),
        out_shape=jax.ShapeDtypeStruct((D_MODEL, COLS_GB.stop), BF16),
        name="wprep",
    )(wt)


def _prepare_params(l, w_in, b_in, g_ln_v, b_ln_v, w_spatial, b_spatial, mu_shift, w0,
                    w_decay_up, a0, w_aaa_up, w_gate_up, k_k, k_a, r_k, gn_gain, gn_bias,
                    w_branch_a, w_branch_b, w_out, b_out, ln1_g, ln1_b, w_ff1, b_ff1, w_ff2,
                    b_ff2, ln2_g, ln2_b):
    g_end = 2 * G_WIDTH
    rkv_end = g_end + 3 * R_WIDTH
    r_end = rkv_end + LORA_COLS
    pad = LORA_PAD - LORA_COLS
    row2 = lambda t: t.reshape(1, -1)
    wi, bi = w_in[l], b_in[l]
    b_all = jnp.concatenate([bi[:r_end], jnp.zeros((pad,), F32), bi[r_end:]])
    mu = jnp.concatenate([mu_shift[l], jnp.zeros((pad,), F32)])
    wup = jnp.zeros((LORA_PAD, 3 * R_WIDTH), F32)
    wup = wup.at[0:DECAY_LORA, 0:R_WIDTH].set(w_decay_up[l])
    wup = wup.at[DECAY_LORA:DECAY_LORA + AAA_LORA, R_WIDTH:2 * R_WIDTH].set(w_aaa_up[l])
    wup = wup.at[DECAY_LORA + AAA_LORA:LORA_COLS, 2 * R_WIDTH:].set(w_gate_up[l])
    ws = w_spatial[l].reshape(PAIRS, 2, CHUNK, CHUNK).transpose(0, 2, 1, 3).reshape(PAIRS, CHUNK, 2 * CHUNK)
    bs = jnp.repeat(b_spatial[l].T, G_WIDTH // G_GROUPS, axis=1)
    hid = jnp.arange(R_WIDTH) // R_HEAD
    bd = (hid[:, None] == hid[None, :]).astype(BF16)
    return dict(
        win=_relayout_w_in(w_in, l), bin=row2(b_all),
        glnv=row2(g_ln_v[l]), blnv=row2(b_ln_v[l]), ws=ws, bs=bs, mu=row2(mu),
        w0=row2(w0[l]), a0=row2(a0[l]), wup=wup.astype(BF16), kk=row2(k_k[l]), ka=row2(k_a[l]),
        bd=bd, wba=w_branch_a[l].astype(BF16),
        rk=row2(r_k[l]), gng=row2(gn_gain[l]), gnb=row2(gn_bias[l]),
        wbb=w_branch_b[l].astype(BF16), wout=w_out[l].astype(BF16), bout=row2(b_out[l]),
        ln1g=row2(ln1_g[l]), ln1b=row2(ln1_b[l]),
        w1=w_ff1[l].astype(BF16), b1=row2(b_ff1[l]), w2=w_ff2[l].astype(BF16), b2=row2(b_ff2[l]),
        ln2g=row2(ln2_g[l]), ln2b=row2(ln2_b[l]),
    )


def _tile(seq, want):
    t = min(want, seq)
    while seq % t:
        t //= 2
    return t


def kernel(x, c, w_ada, b_ada, w_in, b_in, g_ln_v, b_ln_v, w_spatial, b_spatial, mu_shift, w0, w_decay_up, a0, w_aaa_up, w_gate_up, k_k, k_a, r_k, gn_gain, gn_bias, w_branch_a, w_branch_b, w_out, b_out, ln1_g, ln1_b, w_ff1, b_ff1, w_ff2, b_ff2, ln2_g, ln2_b):
    bsz, seq, _ = x.shape
    assert seq % CHUNK == 0 and x.shape[2] == D_MODEL
    out_dtype = x.dtype
    h_res = x.astype(F32)
    tl = _tile(seq, 512)
    for l in range(DEPTH):
        p = _prepare_params(l, w_in, b_in, g_ln_v, b_ln_v, w_spatial, b_spatial, mu_shift, w0,
                            w_decay_up, a0, w_aaa_up, w_gate_up, k_k, k_a, r_k, gn_gain, gn_bias,
                            w_branch_a, w_branch_b, w_out, b_out, ln1_g, ln1_b, w_ff1, b_ff1,
                            w_ff2, b_ff2, ln2_g, ln2_b)
        mod = _modulation(c.astype(F32), w_ada, b_ada, l).reshape(bsz, 6, D_MODEL)
        ma, gb, sc, g = _inproj(h_res, mod, p, tl)
        yb = _scan(sc, p["rk"], p["gng"], p["gnb"])
        h_res = _post(h_res, mod, yb, g, gb, ma, p, tl, F32)
    return h_res.astype(out_dtype)
```

```python
import math

import jax
import jax.numpy as jnp
from jax import lax
from jax.experimental import pallas as pl
from jax.experimental.pallas import tpu as pltpu

D_MODEL = 1024
G_GROUPS = 8
G_WIDTH = 512
CHUNK = 128
R_WIDTH = 512
R_HEAD = 64
DECAY_LORA = 32
AAA_LORA = 32
GATE_LORA = 96
LORA_COLS = DECAY_LORA + AAA_LORA + GATE_LORA
DEPTH = 1
ALPHA = (2.0 * DEPTH) ** 0.25
LN_EPS = 1e-5
GN_EPS = 64e-5
DECAY_SCALE = math.exp(-0.5)

LANES = 128
PAIRS = R_WIDTH // LANES
LORA_PAD = 2 * LANES
R_COLS = 3 * R_WIDTH + LORA_PAD
COLS_U = slice(0, G_WIDTH)
COLS_V = slice(G_WIDTH, 2 * G_WIDTH)
COLS_R = slice(2 * G_WIDTH, 2 * G_WIDTH + R_COLS)
COLS_GA = slice(COLS_R.stop, COLS_R.stop + D_MODEL)
COLS_GB = slice(COLS_GA.stop, COLS_GA.stop + D_MODEL)
SCAN_CHUNK = 64
INV_BASE = 8
POST_GROUPS = 2
IN_GROUPS = 2
I_R, I_LW, I_K, I_V, I_A, I_B = range(6)
GN_BATCHES = 4
SCAN_TILE = 2 * SCAN_CHUNK
SCAN_LAG = 5
VMEM_LIMIT = 56 * 1024 * 1024

F32 = jnp.float32
BF16 = jnp.bfloat16


def _dot(a, b):
    return jnp.dot(a, b, preferred_element_type=F32)


def _sigmoid(x):
    return 1.0 / (1.0 + jnp.exp(-x))


def _gelu_tanh(x):
    c = 0.7978845608028654
    hx = 0.5 * x
    return hx + hx * jnp.tanh(x * (c + (c * 0.044715) * (x * x)))


def _layer_norm(x, g, b, eps):
    mu = jnp.mean(x, axis=-1, keepdims=True)
    d = x - mu
    var = jnp.mean(d * d, axis=-1, keepdims=True)
    return d * lax.rsqrt(var + eps) * g + b


def _const_spec(shape):
    n = len(shape)
    return pl.BlockSpec(shape, lambda *_: (0,) * n)


def _mod_kernel(c_ref, w_ref, b_ref, o_ref):
    c = c_ref[...]
    c_act = c * _sigmoid(c)
    bsz = c.shape[0]
    ch = c_act.astype(BF16)
    cl = (c_act - ch.astype(F32)).astype(BF16)
    w = w_ref[...]
    wh = w.astype(BF16)
    wl = (w - wh.astype(F32)).astype(BF16)
    hh = _dot(jnp.concatenate([ch, cl], axis=0), wh)
    o_ref[...] = hh[:bsz] + hh[bsz:] + _dot(ch, wl) + b_ref[...]


def _modulation(c, w_ada, b_ada, l):
    bsz = c.shape[0]
    n = w_ada.shape[2]
    tn = D_MODEL
    return pl.pallas_call(
        _mod_kernel,
        grid=(n // tn,),
        in_specs=[pl.BlockSpec((bsz, D_MODEL), lambda j: (0, 0)),
                  pl.BlockSpec((None, D_MODEL, tn), lambda j: (l, 0, j)),
                  pl.BlockSpec((None, 1, tn), lambda j: (l, 0, j))],
        out_specs=pl.BlockSpec((bsz, tn), lambda j: (0, j)),
        out_shape=jax.ShapeDtypeStruct((bsz, n), F32),
        name="mod",
    )(c, w_ada, b_ada.reshape(b_ada.shape[0], 1, n))


def _inproj_kernel(x_ref, mod_ref, win_ref, bin_ref,
                   glnv_ref, blnv_ref, ws_ref, bs_ref, mu_ref, w0_ref, a0_ref, wup_ref,
                   kk_ref, ka_ref, bd_ref, wba_ref,
                   ma_ref, gb_ref, sc_ref, g_ref,
                   zsh_ref):
    tl = x_ref.shape[1]

    @pl.when(pl.program_id(1) == 0)
    def _():
        zsh_ref[0:8, :] = jnp.zeros((8, R_COLS), F32)

    sh1 = mod_ref[0, 0:1, :]
    sc1 = mod_ref[0, 1:2, :]
    gsz = tl // IN_GROUPS
    groups = [slice(i * gsz, (i + 1) * gsz) for i in range(IN_GROUPS)]
    gi = range(IN_GROUPS)
    h = [(x_ref[0, rows, :] * (1.0 + sc1) + sh1).astype(BF16) for rows in groups]
    def proj(i, cols):
        return _dot(h[i], win_ref[:, cols]) + bin_ref[:, cols]

    zr = [proj(i, COLS_R) for i in gi]
    for i, rows in enumerate(groups):
        zsh_ref[8 + rows.start:8 + rows.stop, :] = zr[i]
    zu = [proj(i, COLS_U) for i in gi]
    zv = [proj(i, COLS_V) for i in gi]

    z = []
    for i, rows in enumerate(groups):
        prev = zsh_ref[7 + rows.start:7 + rows.stop, :]
        z.append(zr[i] + (prev - zr[i]) * mu_ref[...])
    zsh_ref[7:8, :] = zr[-1][gsz - 1:gsz, :]
    llane = lax.broadcasted_iota(jnp.int32, (gsz, LORA_PAD), 1)
    lin = []
    for i in gi:
        xl = z[i][:, 3 * R_WIDTH:]
        lin.append(jnp.where(llane < DECAY_LORA, jnp.tanh(xl),
                             jnp.where(llane < DECAY_LORA + AAA_LORA, xl, _sigmoid(xl))).astype(BF16))
    zgb = [proj(i, COLS_GB) for i in gi]
    up = [_dot(lin[i], wup_ref[...]) for i in gi]

    u = [_gelu_tanh(zu[i]) for i in gi]
    v = [_layer_norm(_gelu_tanh(zv[i]), glnv_ref[...], blnv_ref[...], LN_EPS) for i in gi]

    kk, n2 = [], []
    for i, rows in enumerate(groups):
        lw = -DECAY_SCALE * _sigmoid(w0_ref[...] + up[i][:, 0:R_WIDTH])
        g_ref[0, rows, :] = up[i][:, 2 * R_WIDTH:].astype(g_ref.dtype)
        kk.append(z[i][:, R_WIDTH:2 * R_WIDTH] * kk_ref[...])
        n2.append(_dot((kk[i] * kk[i]).astype(BF16), bd_ref[...]))
        for q in range(PAIRS):
            sl = slice(q * LANES, (q + 1) * LANES)
            sc_ref[0, I_LW, q, rows, :] = lw[:, sl]
            sc_ref[0, I_R, q, rows, :] = z[i][:, sl]
            sc_ref[0, I_V, q, rows, :] = z[i][:, 2 * R_WIDTH + q * LANES:2 * R_WIDTH + (q + 1) * LANES]

    lane = lax.broadcasted_iota(jnp.int32, (CHUNK, LANES), 1)
    first_head = lane < R_HEAD
    trow = lax.broadcasted_iota(jnp.int32, (CHUNK, 2 * CHUNK), 0)
    scol = lax.broadcasted_iota(jnp.int32, (CHUNK, 2 * CHUNK), 1) % CHUNK
    causal = trow >= scol
    ws = [jnp.where(causal, ws_ref[q], 0.0).astype(BF16) for q in range(PAIRS)]
    ya = []
    for i in gi:
        ya_rows = []
        for c in range(gsz // CHUNK):
            vc = v[i][c * CHUNK:(c + 1) * CHUNK, :]
            s_parts = []
            for q in range(PAIRS):
                v2 = vc[:, q * LANES:(q + 1) * LANES]
                vm = jnp.concatenate([jnp.where(first_head, v2, 0.0),
                                      jnp.where(first_head, 0.0, v2)], axis=0).astype(BF16)
                s_parts.append(_dot(ws[q], vm))
            s = jnp.concatenate(s_parts, axis=1) + bs_ref[...]
            ya_rows.append(u[i][c * CHUNK:(c + 1) * CHUNK, :] * s)
        ya.append(jnp.concatenate(ya_rows, axis=0).astype(BF16))
    zga = [proj(i, COLS_GA) for i in gi]
    pa = [_dot(ya[i], wba_ref[...]) for i in gi]

    for i, rows in enumerate(groups):
        gb_ref[0, rows, :] = _sigmoid(zgb[i]).astype(gb_ref.dtype)
        a = _sigmoid(a0_ref[...] + up[i][:, R_WIDTH:2 * R_WIDTH])
        kkn = kk[i] * lax.rsqrt(jnp.maximum(n2[i], 1e-24))
        k2 = z[i][:, R_WIDTH:2 * R_WIDTH] * (1.0 + (a - 1.0) * ka_ref[...])
        nkk = -kkn
        kka = kkn * a
        for q in range(PAIRS):
            sl = slice(q * LANES, (q + 1) * LANES)
            sc_ref[0, I_K, q, rows, :] = k2[:, sl]
            sc_ref[0, I_A, q, rows, :] = nkk[:, sl]
            sc_ref[0, I_B, q, rows, :] = kka[:, sl]
        ma_ref[0, rows, :] = (_sigmoid(zga[i]) * pa[i]).astype(ma_ref.dtype)


def _inproj(x, mod, p, tl):
    bsz, seq, _ = x.shape
    grid = (bsz, seq // tl)
    row = lambda b, s: (b, s, 0)
    sc_shape = jax.ShapeDtypeStruct((bsz, 6, PAIRS, seq, LANES), F32)
    sc_spec = pl.BlockSpec((1, 6, PAIRS, tl, LANES), lambda b, s: (b, 0, 0, s, 0))
    consts = [p["win"], p["bin"], p["glnv"], p["blnv"],
              p["ws"], p["bs"], p["mu"], p["w0"], p["a0"], p["wup"], p["kk"], p["ka"], p["bd"],
              p["wba"]]
    return pl.pallas_call(
        _inproj_kernel,
        grid=grid,
        in_specs=[pl.BlockSpec((1, tl, D_MODEL), row),
                  pl.BlockSpec((1, 6, D_MODEL), lambda b, s: (b, 0, 0))]
                 + [_const_spec(c.shape) for c in consts],
        out_specs=[pl.BlockSpec((1, tl, D_MODEL), row),
                   pl.BlockSpec((1, tl, D_MODEL), row)]
                  + [sc_spec]
                  + [pl.BlockSpec((1, tl, R_WIDTH), row)],
        out_shape=[jax.ShapeDtypeStruct((bsz, seq, D_MODEL), BF16),
                   jax.ShapeDtypeStruct((bsz, seq, D_MODEL), BF16)]
                  + [sc_shape]
                  + [jax.ShapeDtypeStruct((bsz, seq, R_WIDTH), BF16)],
        scratch_shapes=[pltpu.VMEM((tl + 8, R_COLS), F32)],
        compiler_params=pltpu.CompilerParams(
            dimension_semantics=("arbitrary", "arbitrary"), vmem_limit_bytes=VMEM_LIMIT),
        name="inproj",
    )(x, mod, *consts)


def _scan_kernel(sc_ref, rk_ref, gng_ref, gnb_ref, y_ref, st_ref):
    bt, _, pp, lt, _ = sc_ref.shape
    cs = SCAN_CHUNK

    @pl.when(pl.program_id(0) == 0)
    def _():
        st_ref[...] = jnp.zeros(st_ref.shape, F32)

    lane = lax.broadcasted_iota(jnp.int32, (cs, LANES), 1)
    first_head = lane < R_HEAD

    def stack(t):
        return jnp.concatenate([jnp.where(first_head, t, 0.0),
                                jnp.where(first_head, 0.0, t)], axis=0)

    def fold(t):
        return jnp.where(first_head, t[:cs], t[cs:])

    ri = lax.broadcasted_iota(jnp.int32, (cs, cs), 0)
    ci = lax.broadcasted_iota(jnp.int32, (cs, cs), 1)
    tri_ones = jnp.where(ri >= ci, 1.0, 0.0).astype(BF16)
    si = lax.broadcasted_iota(jnp.int32, (cs, LANES), 0)
    sj = lane % cs
    strict = si > sj
    incl = si >= sj
    eye = jnp.where(si == sj, 1.0, 0.0).astype(F32)
    inv_base_mask = (si // INV_BASE) == (sj // INV_BASE)
    inv_merge_masks = []
    m = INV_BASE
    while m < cs:
        inv_merge_masks.append(((si // (2 * m)) == (sj // (2 * m)))
                               & ((si % (2 * m)) >= m) & ((sj % (2 * m)) < m))
        m *= 2
    bsi =lax.broadcasted_iota(jnp.int32, (LANES, LANES), 0) // R_HEAD
    bsj = lax.broadcasted_iota(jnp.int32, (LANES, LANES), 1) // R_HEAD
    head_ones = jnp.where(bsi == bsj, 1.0, 0.0).astype(BF16)
    head_mean = jnp.where(bsi == bsj, 1.0 / R_HEAD, 0.0).astype(BF16)

    chains = [(b, q) for q in range(pp) for b in range(bt)]
    chunks = [[dict(j=j, b=b, q=q, rows=slice(ck * cs, (ck + 1) * cs))
               for j, (b, q) in enumerate(chains)] for ck in range(lt // cs)]

    def s_cumsum(c):
        lw = sc_ref[c["b"], I_LW, c["q"], c["rows"], :]
        p1 = lw.astype(BF16)
        p2 = (lw - p1.astype(F32)).astype(BF16)
        c["lw"] = lw
        c["cum"] = _dot(tri_ones, p1) + _dot(tri_ones, p2)

    def s_operands(c):
        b, q, rows, cum, lw = c["b"], c["q"], c["rows"], c["cum"], c["lw"]
        last = cum[cs - 1:cs, :]
        e_pos = jnp.exp(cum)
        e_pos_ex = jnp.exp(cum - lw)
        e_neg = 1.0 / e_pos
        wcol = jnp.broadcast_to(jnp.exp(last), (LANES, LANES)).T
        r = sc_ref[b, I_R, q, rows, :]
        k = sc_ref[b, I_K, q, rows, :]
        bb = sc_ref[b, I_B, q, rows, :]
        c["at"] = sc_ref[b, I_A, q, rows, :] * e_pos_ex
        c["rt_b"] = (r * e_pos).astype(BF16)
        tb = stack(bb * e_neg).T
        tk = stack(k * e_neg).T
        c["vm"] = stack(sc_ref[b, I_V, q, rows, :]).astype(BF16)
        c["bt"] = fold(tb).astype(BF16)
        c["kt"] = fold(tk)
        c["wm"] = fold(wcol)
        c["aa"] = _dot(jnp.concatenate([c["at"].astype(BF16), c["rt_b"]], axis=0),
                       jnp.concatenate([tb, tk], axis=1).astype(BF16))
        lanes = slice(q * LANES, (q + 1) * LANES)
        c["rkr"] = (r * k * rk_ref[:, lanes]).astype(BF16)

    def s_intra(c):
        aa = c.pop("aa")
        a_ab = jnp.where(strict, aa[:cs, :LANES], 0.0)
        c["a_rb"] = jnp.where(incl, aa[cs:, :LANES], 0.0).astype(BF16)
        c["akv_y0"] = _dot(jnp.concatenate([jnp.where(strict, aa[:cs, LANES:], 0.0),
                                            jnp.where(incl, aa[cs:, LANES:], 0.0),
                                            c.pop("kt")], axis=0).astype(BF16), c.pop("vm"))
        a_d = jnp.where(inv_base_mask, a_ab, 0.0)
        c["a_ab"] = a_ab
        c["xk"] = _dot(a_d.astype(BF16), stack(a_d).astype(BF16))
        c["tk"] = eye + a_d

    def s_base_square(c):
        x_bd = stack(c["xk"]).astype(BF16)
        xt = _dot(jnp.concatenate([c["xk"], c["tk"]], axis=0).astype(BF16), x_bd)
        c["xk"] = xt[:cs]
        c["tk"] = c["tk"] + xt[cs:]

    def s_base_last(c):
        c["tk"] = c["tk"] + _dot(c["tk"].astype(BF16), stack(c.pop("xk")).astype(BF16))

    half = cs // 2
    sub = 8

    def lower_pieces(level):
        m = INV_BASE << level
        return [p for p in range(cs // sub) if (p * sub) % (2 * m) >= m]

    def lower(t, level):
        return jnp.concatenate([t[p * sub:(p + 1) * sub] for p in lower_pieces(level)], axis=0)

    def expand(x, level, base=None):
        pieces = lower_pieces(level)
        out = []
        for p in range(cs // sub):
            rest = jnp.zeros((sub, LANES), F32) if base is None else base[p * sub:(p + 1) * sub]
            if p in pieces:
                i = pieces.index(p)
                piece = x[i * sub:(i + 1) * sub]
                rest = piece if base is None else rest + piece
            out.append(rest)
        return jnp.concatenate(out, axis=0)

    def s_merge_bt(c):
        a_ab = c.pop("a_ab")
        b_off = jnp.concatenate([lower(jnp.where(mk, a_ab, 0.0), level)
                                 for level, mk in enumerate(inv_merge_masks)], axis=0)
        g = _dot(b_off.astype(BF16), stack(c["tk"]).astype(BF16))
        c["g"] = [g[i * half:(i + 1) * half] for i in range(len(inv_merge_masks))]

    def s_merge_tbt(level):
        def run(c):
            g = c["g"]
            x = _dot(jnp.concatenate([lower(c["tk"], level)] + g[1:], axis=0).astype(BF16),
                     stack(expand(g[0], level)).astype(BF16))
            c["tk"] = expand(x[:half], level, base=c["tk"])
            c["g"] = [g[i] + x[i * half:(i + 1) * half] for i in range(1, len(g))]
        return run

    def s_solve(c):
        del c["g"]
        c["pq"] = _dot(c.pop("tk").astype(BF16),
                       jnp.concatenate([stack(c.pop("at")), stack(c["akv_y0"][:cs])],
                                       axis=1).astype(BF16))

    def s_u(c):
        c["st"] = st_ref[c["j"]]
        pq = c.pop("pq")
        ps = _dot(jnp.concatenate([pq[:, :LANES].astype(BF16), c.pop("rt_b")], axis=0),
                  stack(c["st"]).astype(BF16))
        c["u_b"] = stack(ps[:cs] + pq[:, LANES:]).astype(BF16)
        c["rs"] = ps[cs:]

    def s_y(c):
        au = _dot(jnp.concatenate([c.pop("a_rb"), c.pop("bt")], axis=0), c.pop("u_b"))
        akv_y0 = c.pop("akv_y0")
        c["y"] = c.pop("rs") + au[:cs] + akv_y0[cs:2 * cs]
        st_ref[c["j"]] = c.pop("wm") * (c.pop("st") + au[cs:] + akv_y0[2 * cs:])

    def s_bonus(cstate):
        rk_sum = _dot(jnp.concatenate([c.pop("rkr") for c in cstate], axis=0), head_ones)
        for c in cstate:
            c["rk_sum"] = rk_sum[c["j"] * cs:(c["j"] + 1) * cs]

    def per_chain(stage):
        def run(cstate):
            for c in cstate:
                stage(c)
        return run

    def finish_stages(cstate):
        gsz = max(1, len(cstate) // GN_BATCHES)
        batches = [cstate[i:i + gsz] for i in range(0, len(cstate), gsz)]
        d = {}

        def tail(t):
            if t < len(batches):
                per_chain(s_y)(batches[t])
            if 0 <= t - 1 < len(batches):
                y_b = jnp.concatenate([c.pop("y") for c in batches[t - 1]], axis=0)
                d[t - 1] = y_b - _dot(y_b.astype(BF16), head_mean)
            if 0 <= t - 2 < len(batches):
                d_b = d.pop(t - 2)
                yn = d_b * lax.rsqrt(_dot((d_b * d_b).astype(BF16), head_mean) + GN_EPS)
                for i, c in enumerate(batches[t - 2]):
                    b, q, rows = c["b"], c["q"], c["rows"]
                    lanes = slice(q * LANES, (q + 1) * LANES)
                    y_ref[b, q, rows, :] = (
                        yn[i * cs:(i + 1) * cs] * gng_ref[:, lanes] + gnb_ref[:, lanes]
                        + c.pop("rk_sum") * sc_ref[b, I_V, q, rows, :]).astype(y_ref.dtype)

        return ([lambda: per_chain(s_u)(cstate)]
                + [lambda t=t: tail(t) for t in range(len(batches) + 2)])

    n_base_sq = (INV_BASE - 1).bit_length() - 1
    pre = ([per_chain(s_cumsum), per_chain(s_operands), per_chain(s_intra), s_bonus]
           + [per_chain(s_base_square)] * (n_base_sq - 1) + [per_chain(s_base_last)])
    pre += [per_chain(s_merge_bt)]
    pre += [per_chain(s_merge_tbt(level)) for level in range(len(inv_merge_masks))]
    pre.append(per_chain(s_solve))
    fin = [finish_stages(cstate) for cstate in chunks]
    fin_lag = max(SCAN_LAG, GN_BATCHES + 1)
    for t in range(len(pre) + fin_lag * (len(chunks) - 1) + len(fin[0])):
        for ck, cstate in enumerate(chunks):
            k = t - ck * SCAN_LAG
            if 0 <= k < len(pre):
                pre[k](cstate)
            f = t - len(pre) - ck * fin_lag
            if 0 <= f < len(fin[ck]):
                fin[ck][f]()


def _scan(sc, rk, gng, gnb):
    bsz, _, pp, seq, _ = sc.shape
    blk = pl.BlockSpec((bsz, pp, SCAN_TILE, LANES), lambda s: (0, 0, s, 0))
    vec = pl.BlockSpec((1, pp * LANES), lambda s: (0, 0))
    return pl.pallas_call(
        _scan_kernel,
        grid=(seq // SCAN_TILE,),
        in_specs=[pl.BlockSpec((bsz, 6, pp, SCAN_TILE, LANES), lambda s: (0, 0, 0, s, 0))] + [vec] * 3,
        out_specs=blk,
        out_shape=jax.ShapeDtypeStruct((bsz, pp, seq, LANES), BF16),
        scratch_shapes=[pltpu.VMEM((bsz * pp, SCAN_CHUNK, LANES), F32)],
        compiler_params=pltpu.CompilerParams(
            dimension_semantics=("arbitrary",), vmem_limit_bytes=VMEM_LIMIT),
        name="scan",
    )(sc, rk, gng, gnb)


def _post_kernel(x_ref, mod_ref, yb_ref, g_ref, gb_ref, ma_ref, wbb_ref, wout_ref, bout_ref,
                 ln1g_ref, ln1b_ref, w1_ref, b1_ref, w2_ref, b2_ref, ln2g_ref, ln2b_ref, o_ref):
    tl = x_ref.shape[1]
    gt1 = mod_ref[0, 2:3, :]
    sh2 = mod_ref[0, 3:4, :]
    sc2 = mod_ref[0, 4:5, :]
    gt2 = mod_ref[0, 5:6, :]
    groups = [slice(i * tl // POST_GROUPS, (i + 1) * tl // POST_GROUPS) for i in range(POST_GROUPS)]
    pb = []
    for rows in groups:
        yb = (jnp.concatenate([yb_ref[0, q, rows, :] for q in range(PAIRS)], axis=1).astype(F32)
              * g_ref[0, rows, :].astype(F32))
        pb.append(_dot(yb.astype(BF16), wbb_ref[...]))
    h1 = []
    for i, rows in enumerate(groups):
        merged = ma_ref[0, rows, :].astype(F32) + gb_ref[0, rows, :].astype(F32) * pb[i]
        mix = _dot(merged.astype(BF16), wout_ref[...]) + bout_ref[...]
        h1.append(_layer_norm(ALPHA * x_ref[0, rows, :] + gt1 * mix,
                              ln1g_ref[...], ln1b_ref[...], LN_EPS))
    t = []
    for i in range(POST_GROUPS):
        h = (h1[i] * (1.0 + sc2) + sh2).astype(BF16)
        t.append(jnp.maximum(_dot(h, w1_ref[...]) + b1_ref[...], 0.0))
    for i, rows in enumerate(groups):
        ff = _dot((t[i] * t[i]).astype(BF16), w2_ref[...]) + b2_ref[...]
        o_ref[0, rows, :] = _layer_norm(ALPHA * h1[i] + gt2 * ff,
                                        ln2g_ref[...], ln2b_ref[...], LN_EPS).astype(o_ref.dtype)


def _post(x, mod, yb, g, gb, ma, p, tl, out_dtype):
    bsz, seq, _ = x.shape
    row = lambda b, s: (b, s, 0)
    consts = [p["wbb"], p["wout"], p["bout"], p["ln1g"], p["ln1b"],
              p["w1"], p["b1"], p["w2"], p["b2"], p["ln2g"], p["ln2b"]]
    return pl.pallas_call(
        _post_kernel,
        grid=(bsz, seq // tl),
        in_specs=[pl.BlockSpec((1, tl, D_MODEL), row),
                  pl.BlockSpec((1, 6, D_MODEL), lambda b, s: (b, 0, 0)),
                  pl.BlockSpec((1, PAIRS, tl, LANES), lambda b, s: (b, 0, s, 0)),
                  pl.BlockSpec((1, tl, R_WIDTH), row),
                  pl.BlockSpec((1, tl, D_MODEL), row),
                  pl.BlockSpec((1, tl, D_MODEL), row)]
                 + [_const_spec(c.shape) for c in consts],
        out_specs=pl.BlockSpec((1, tl, D_MODEL), row),
        out_shape=jax.ShapeDtypeStruct((bsz, seq, D_MODEL), out_dtype),
        compiler_params=pltpu.CompilerParams(
            dimension_semantics=("arbitrary", "arbitrary"), vmem_limit_bytes=VMEM_LIMIT),
        name="post",
    )(x, mod, yb, g, gb, ma, *consts)


W_IN_SPLIT = 2 * G_WIDTH + 3 * R_WIDTH + LORA_COLS
WPREP_COLS = 2 * LANES
GAP_BLOCK = W_IN_SPLIT // WPREP_COLS


def _relayout_w_in_kernel(wt_ref, o_ref):
    j = pl.program_id(0)
    rows = lax.broadcasted_iota(jnp.int32, wt_ref.shape, 0)
    keep = jnp.logical_or(j != GAP_BLOCK, rows < W_IN_SPLIT % WPREP_COLS)
    o_ref[...] = jnp.where(keep, wt_ref[...], 0.0).T.astype(BF16)


def _relayout_w_in(w_in, l):
    wt = jnp.transpose(w_in[l])
    gap = COLS_R.stop - W_IN_SPLIT

    unit = math.gcd(WPREP_COLS, gap)

    def src_row(j):
        return (j * (WPREP_COLS // unit) - jnp.where(j <= GAP_BLOCK, 0, gap // unit)) * unit

    return pl.pallas_call(
        _relayout_w_in_kernel,
        grid=(COLS_GB.stop // WPREP_COLS,),
        in_specs=[pl.BlockSpec((pl.Element(WPREP_COLS), pl.Element(D_MODEL)),
                               lambda j: (src_row(j), 0))],
        out_specs=pl.BlockSpec((D_MODEL, WPREP_COLS), lambda j: (0, j)),
        out_shape=jax.ShapeDtypeStruct((D_MODEL, COLS_GB.stop), BF16),
        name="wprep",
    )(wt)


def _prepare_params(l, w_in, b_in, g_ln_v, b_ln_v, w_spatial, b_spatial, mu_shift, w0,
                    w_decay_up, a0, w_aaa_up, w_gate_up, k_k, k_a, r_k, gn_gain, gn_bias,
                    w_branch_a, w_branch_b, w_out, b_out, ln1_g, ln1_b, w_ff1, b_ff1, w_ff2,
                    b_ff2, ln2_g, ln2_b):
    g_end = 2 * G_WIDTH
    rkv_end = g_end + 3 * R_WIDTH
    r_end = rkv_end + LORA_COLS
    pad = LORA_PAD - LORA_COLS
    row2 = lambda t: t.reshape(1, -1)
    wi, bi = w_in[l], b_in[l]
    b_all = jnp.concatenate([bi[:r_end], jnp.zeros((pad,), F32), bi[r_end:]])
    mu = jnp.concatenate([mu_shift[l], jnp.zeros((pad,), F32)])
    wup = jnp.zeros((LORA_PAD, 3 * R_WIDTH), F32)
    wup = wup.at[0:DECAY_LORA, 0:R_WIDTH].set(w_decay_up[l])
    wup = wup.at[DECAY_LORA:DECAY_LORA + AAA_LORA, R_WIDTH:2 * R_WIDTH].set(w_aaa_up[l])
    wup = wup.at[DECAY_LORA + AAA_LORA:LORA_COLS, 2 * R_WIDTH:].set(w_gate_up[l])
    ws = w_spatial[l].reshape(PAIRS, 2, CHUNK, CHUNK).transpose(0, 2, 1, 3).reshape(PAIRS, CHUNK, 2 * CHUNK)
    bs = jnp.repeat(b_spatial[l].T, G_WIDTH // G_GROUPS, axis=1)
    hid = jnp.arange(R_WIDTH) // R_HEAD
    bd = (hid[:, None] == hid[None, :]).astype(BF16)
    return dict(
        win=_relayout_w_in(w_in, l), bin=row2(b_all),
        glnv=row2(g_ln_v[l]), blnv=row2(b_ln_v[l]), ws=ws, bs=bs, mu=row2(mu),
        w0=row2(w0[l]), a0=row2(a0[l]), wup=wup.astype(BF16), kk=row2(k_k[l]), ka=row2(k_a[l]),
        bd=bd, wba=w_branch_a[l].astype(BF16),
        rk=row2(r_k[l]), gng=row2(gn_gain[l]), gnb=row2(gn_bias[l]),
        wbb=w_branch_b[l].astype(BF16), wout=w_out[l].astype(BF16), bout=row2(b_out[l]),
        ln1g=row2(ln1_g[l]), ln1b=row2(ln1_b[l]),
        w1=w_ff1[l].astype(BF16), b1=row2(b_ff1[l]), w2=w_ff2[l].astype(BF16), b2=row2(b_ff2[l]),
        ln2g=row2(ln2_g[l]), ln2b=row2(ln2_b[l]),
    )


def _tile(seq, want):
    t = min(want, seq)
    while seq % t:
        t //= 2
    return t


def kernel(x, c, w_ada, b_ada, w_in, b_in, g_ln_v, b_ln_v, w_spatial, b_spatial, mu_shift, w0, w_decay_up, a0, w_aaa_up, w_gate_up, k_k, k_a, r_k, gn_gain, gn_bias, w_branch_a, w_branch_b, w_out, b_out, ln1_g, ln1_b, w_ff1, b_ff1, w_ff2, b_ff2, ln2_g, ln2_b):
    bsz, seq, _ = x.shape
    assert seq % CHUNK == 0 and x.shape[2] == D_MODEL
    out_dtype = x.dtype
    h_res = x.astype(F32)
    tl = _tile(seq, 512)
    for l in range(DEPTH):
        p = _prepare_params(l, w_in, b_in, g_ln_v, b_ln_v, w_spatial, b_spatial, mu_shift, w0,
                            w_decay_up, a0, w_aaa_up, w_gate_up, k_k, k_a, r_k, gn_gain, gn_bias,
                            w_branch_a, w_branch_b, w_out, b_out, ln1_g, ln1_b, w_ff1, b_ff1,
                            w_ff2, b_ff2, ln2_g, ln2_b)
        mod = _modulation(c.astype(F32), w_ada, b_ada, l).reshape(bsz, 6, D_MODEL)
        ma, gb, sc, g = _inproj(h_res, mod, p, tl)
        yb = _scan(sc, p["rk"], p["gng"], p["gnb"])
        h_res = _post(h_res, mod, yb, g, gb, ma, p, tl, F32)
    return h_res.astype(out_dtype)
```
